```python
import math
import jax, jax.numpy as jnp
from jax import lax
import numpy as np

D_MODEL = 1024
BATCH = 8
SEQ = 4096
DEPTH = 1

MIX_WIDTH = D_MODEL
DIFF_WIDTH = MIX_WIDTH // 2
FOX_WIDTH = MIX_WIDTH - DIFF_WIDTH
DIFF_QK_DIM = 64
DIFF_V_DIM = 2 * DIFF_QK_DIM
N_DIFF_HEADS = DIFF_WIDTH // DIFF_V_DIM
FOX_HEAD_DIM = 64
N_FOX_HEADS = FOX_WIDTH // FOX_HEAD_DIM
N_IN = 3 * DIFF_WIDTH + 3 * FOX_WIDTH + N_FOX_HEADS
D_FF = ((8 * D_MODEL // 3 + 255) // 256) * 256
CONV_WIDTH = 3
NUM_BUCKETS = 32
MAX_EXACT = NUM_BUCKETS // 2
MAX_DISTANCE = 128
BLOCK_Q = 128
N_MOD = 6
NORM_EPS = 1e-6
NEG_INF = -1e30

kernel_name = "hybrid_diff_fox_convffn_adaln"


def _rmsnorm(x, g):
    xf = x.astype(jnp.float32)
    y = xf * lax.rsqrt(jnp.mean(xf * xf, axis=-1, keepdims=True) + NORM_EPS)
    return (y * g.astype(jnp.float32)).astype(x.dtype)


def _modulate(x, g, shift, scale):
    return _rmsnorm(x, g) * (1 + scale[:, None, :]) + shift[:, None, :]


def _t5_causal_bucket(n):
    nf = jnp.maximum(n, 1).astype(jnp.float32)
    large = MAX_EXACT + (jnp.log(nf / MAX_EXACT) / math.log(MAX_DISTANCE / MAX_EXACT)
                         * (NUM_BUCKETS - MAX_EXACT)).astype(jnp.int32)
    large = jnp.minimum(large, NUM_BUCKETS - 1)
    return jnp.where(n < MAX_EXACT, n, large)


def _diff_attention(q, k, v, bias_dist, lam):
    B, S, H, _, Dk = q.shape
    Dv = v.shape[-1]
    scale = Dk ** -0.5
    kpos = jnp.arange(S)

    def block(i):
        q0 = i * BLOCK_Q
        qb = lax.dynamic_slice_in_dim(q, q0, BLOCK_Q, axis=1)
        s = jnp.einsum('bqhmd,bkhmd->bhmqk', qb, k,
                       preferred_element_type=jnp.float32) * scale
        dist = (q0 + jnp.arange(BLOCK_Q))[:, None] - kpos[None, :]
        bias = jnp.transpose(bias_dist[jnp.maximum(dist, 0)], (2, 0, 1)).astype(jnp.float32)
        s = jnp.where(dist >= 0, s + bias[None, :, None], NEG_INF)
        p = jax.nn.softmax(s, axis=-1)
        a = p[:, :, 0] - lam * p[:, :, 1]
        return jnp.einsum('bhqk,bkhd->bqhd', a.astype(v.dtype), v)

    out = lax.map(block, jnp.arange(S // BLOCK_Q))
    return jnp.moveaxis(out, 0, 1).reshape(B, S, H, Dv)


def _forgetting_attention(q, k, v, cum_logf):
    B, S, H, D = q.shape
    scale = D ** -0.5
    kpos = jnp.arange(S)
    cum_k = jnp.transpose(cum_logf, (0, 2, 1))

    def block(i):
        q0 = i * BLOCK_Q
        qb = lax.dynamic_slice_in_dim(q, q0, BLOCK_Q, axis=1)
        cq = lax.dynamic_slice_in_dim(cum_k, q0, BLOCK_Q, axis=2)
        s = jnp.einsum('bqhd,bkhd->bhqk', qb, k,
                       preferred_element_type=jnp.float32) * scale
        decay = cq[..., :, None] - cum_k[..., None, :]
        causal = ((q0 + jnp.arange(BLOCK_Q))[:, None] >= kpos[None, :])
        s = jnp.where(causal[None, None], s + decay, NEG_INF)
        p = jax.nn.softmax(s, axis=-1)
        return jnp.einsum('bhqk,bkhd->bqhd', p.astype(v.dtype), v)

    out = lax.map(block, jnp.arange(S // BLOCK_Q))
    return jnp.moveaxis(out, 0, 1).reshape(B, S, H, D)


def _causal_depthwise_conv(u, w, b):
    C = u.shape[-1]
    y = lax.conv_general_dilated(u, w[:, None, :].astype(u.dtype), window_strides=(1,),
                                 padding=[(CONV_WIDTH - 1, 0)],
                                 dimension_numbers=('NWC', 'WIO', 'NWC'),
                                 feature_group_count=C)
    return y + b


def setup_inputs(seed: int = 0) -> dict:
    key = jax.random.key(seed)
    ks = jax.random.split(key, 24)
    f32 = jnp.float32
    nrm = lambda k, shape, s: (jax.random.normal(k, shape, f32) * s)
    D = D_MODEL
    return {
        "x": nrm(ks[0], (BATCH, SEQ, D), 1.0),
        "c": nrm(ks[1], (BATCH, D), 1.0),
        "ada_w": nrm(ks[2], (DEPTH, D, N_MOD * D), D ** -0.5),
        "ada_b": nrm(ks[3], (DEPTH, N_MOD * D), 0.02),
        "attn_norm_g": 1.0 + nrm(ks[4], (DEPTH, D), 0.02),
        "w_in": nrm(ks[5], (DEPTH, D, N_IN), D ** -0.5),
        "forget_b": jax.random.uniform(ks[6], (DEPTH, N_FOX_HEADS), f32, 1.0, 4.0),
        "lambda_q1": nrm(ks[7], (DEPTH, DIFF_QK_DIM), 0.1),
        "lambda_k1": nrm(ks[8], (DEPTH, DIFF_QK_DIM), 0.1),
        "lambda_q2": nrm(ks[9], (DEPTH, DIFF_QK_DIM), 0.1),
        "lambda_k2": nrm(ks[10], (DEPTH, DIFF_QK_DIM), 0.1),
        "subln_g": 1.0 + nrm(ks[11], (DEPTH, DIFF_V_DIM), 0.02),
        "rel_bias": nrm(ks[12], (NUM_BUCKETS, N_DIFF_HEADS), 0.5),
        "w_out": nrm(ks[13], (DEPTH, MIX_WIDTH, D), MIX_WIDTH ** -0.5),
        "ffn_norm_g": 1.0 + nrm(ks[14], (DEPTH, D), 0.02),
        "w_up": nrm(ks[15], (DEPTH, D, 2 * D_FF), D ** -0.5),
        "conv_w": nrm(ks[16], (DEPTH, CONV_WIDTH, 2 * D_FF), CONV_WIDTH ** -0.5),
        "conv_b": nrm(ks[17], (DEPTH, 2 * D_FF), 0.02),
        "w_down": nrm(ks[18], (DEPTH, D_FF, D), D_FF ** -0.5),
        "final_norm_g": 1.0 + nrm(ks[19], (D,), 0.02),
    }


def reference(x, c, ada_w, ada_b, attn_norm_g, w_in, forget_b, lambda_q1, lambda_k1,
              lambda_q2, lambda_k2, subln_g, rel_bias, w_out, ffn_norm_g, w_up, conv_w,
              conv_b, w_down, final_norm_g):
    B, S, D = x.shape
    bias_dist = rel_bias[_t5_causal_bucket(jnp.arange(S, dtype=jnp.int32))]
    c_act = jax.nn.silu(c)

    for l in range(DEPTH):
        mod = c_act @ ada_w[l] + ada_b[l]
        sh1, sc1, g1, sh2, sc2, g2 = jnp.split(mod, N_MOD, axis=-1)

        h = _modulate(x, attn_norm_g[l], sh1, sc1)
        proj = h @ w_in[l]
        o = np.cumsum([DIFF_WIDTH, DIFF_WIDTH, DIFF_WIDTH, FOX_WIDTH, FOX_WIDTH, FOX_WIDTH]).tolist()
        dq, dk, dv, fq, fk, fv, fl = jnp.split(proj, o, axis=-1)

        dq = dq.reshape(B, S, N_DIFF_HEADS, 2, DIFF_QK_DIM)
        dk = dk.reshape(B, S, N_DIFF_HEADS, 2, DIFF_QK_DIM)
        dv = dv.reshape(B, S, N_DIFF_HEADS, DIFF_V_DIM)
        lambda_init = 0.8 - 0.6 * math.exp(-0.3 * l)
        lam = (jnp.exp(jnp.sum(lambda_q1[l].astype(jnp.float32) * lambda_k1[l].astype(jnp.float32)))
               - jnp.exp(jnp.sum(lambda_q2[l].astype(jnp.float32) * lambda_k2[l].astype(jnp.float32)))
               + lambda_init)
        d_out = _diff_attention(dq, dk, dv, bias_dist, lam)
        d_out = _rmsnorm(d_out, subln_g[l]) * (1.0 - lambda_init)
        d_out = d_out.reshape(B, S, DIFF_WIDTH)

        fq = fq.reshape(B, S, N_FOX_HEADS, FOX_HEAD_DIM)
        fk = fk.reshape(B, S, N_FOX_HEADS, FOX_HEAD_DIM)
        fv = fv.reshape(B, S, N_FOX_HEADS, FOX_HEAD_DIM)
        log_f = jax.nn.log_sigmoid((fl + forget_b[l]).astype(jnp.float32))
        cum_logf = jnp.cumsum(log_f, axis=1)
        f_out = _forgetting_attention(fq, fk, fv, cum_logf).reshape(B, S, FOX_WIDTH)

        mix = jnp.concatenate([d_out, f_out.astype(d_out.dtype)], axis=-1) @ w_out[l]
        x = x + g1[:, None, :] * mix

        h = _modulate(x, ffn_norm_g[l], sh2, sc2)
        u = _causal_depthwise_conv(h @ w_up[l], conv_w[l], conv_b[l])
        gate, val = jnp.split(u, 2, axis=-1)
        y = (jax.nn.silu(gate) * val) @ w_down[l]
        x = x + g2[:, None, :] * y

    return _rmsnorm(x, final_norm_g)
```

```python
import functools
import math

import jax
import jax.numpy as jnp
from jax import lax
from jax.experimental import pallas as pl
from jax.experimental.pallas import tpu as pltpu

F32 = jnp.float32
BF16 = jnp.bfloat16

DIFF_QK_DIM = 64
DIFF_V_DIM = 2 * DIFF_QK_DIM
FOX_HEAD_DIM = 64
CONV_WIDTH = 3
NUM_BUCKETS = 32
MAX_EXACT = NUM_BUCKETS // 2
MAX_DISTANCE = 128
N_MOD = 6
NORM_EPS = 1e-6
NEG_INF = -1e30

LANES = 128
SUBLANES = 8
VMEM_LIMIT_BYTES = 56 * 1024 * 1024

ATTN_TILE = 512
ROW_TILE = 512
FFN_CHUNK = 256
MOD_COL_TILE = 1536


def _params(semantics):
    return pltpu.CompilerParams(dimension_semantics=semantics, vmem_limit_bytes=VMEM_LIMIT_BYTES)


def _rms_scale(x):
    return lax.rsqrt(jnp.mean(x * x, axis=-1, keepdims=True) + NORM_EPS)


def _mod_kernel(c_ref, w_ref, b_ref, o_ref):
    c = c_ref[...]
    act = c * jax.nn.sigmoid(c)
    o_ref[...] = jnp.dot(act.astype(BF16), w_ref[...].astype(BF16),
                         preferred_element_type=F32) + b_ref[...]


def _modulation(c, w, b):
    bsz, d = c.shape
    n = w.shape[1]
    tn = MOD_COL_TILE
    return pl.pallas_call(
        _mod_kernel,
        grid=(n // tn,),
        in_specs=[pl.BlockSpec((bsz, d), lambda j: (0, 0)),
                  pl.BlockSpec((d, tn), lambda j: (0, j)),
                  pl.BlockSpec((1, tn), lambda j: (0, j))],
        out_specs=pl.BlockSpec((bsz, tn), lambda j: (0, j)),
        out_shape=jax.ShapeDtypeStruct((bsz, n), F32),
        compiler_params=_params(("arbitrary",)),
        name="modulation",
    )(c, w, b.reshape(1, n))


def _proj_kernel(x_ref, mod_ref, g_ref, w_ref, dq_ref, dk_ref, dv_ref, fq_ref, fk_ref, fv_ref,
                 fl_ref, *, width, n_fox_heads, qk_scale):
    x = x_ref[0]
    h = (x * _rms_scale(x) * g_ref[...]) * (1.0 + mod_ref[0, 1:2, :]) + mod_ref[0, 0:1, :]
    hb = h.astype(BF16)
    outs = ((dq_ref, qk_scale), (dk_ref, None), (dv_ref, None),
            (fq_ref, qk_scale), (fk_ref, None), (fv_ref, None))
    for idx, (ref, scale) in enumerate(outs):
        r = jnp.dot(hb, w_ref[:, idx * width:(idx + 1) * width], preferred_element_type=F32)
        if scale is not None:
            r = r * scale
        ref[0] = r.astype(BF16)
    r = jnp.dot(hb, w_ref[:, 6 * width:], preferred_element_type=F32)
    fl_ref[0] = r[:, :n_fox_heads]


def _in_projection(x, mod3, g, w_pad, *, width, n_fox_heads):
    bsz, s, d = x.shape
    ts = ROW_TILE
    n_pad = w_pad.shape[1]
    qkv_shape = jax.ShapeDtypeStruct((bsz, s, width), BF16)
    qkv_spec = pl.BlockSpec((1, ts, width), lambda b, i: (b, i, 0))
    kern = functools.partial(_proj_kernel, width=width, n_fox_heads=n_fox_heads,
                             qk_scale=DIFF_QK_DIM ** -0.5)
    return pl.pallas_call(
        kern,
        grid=(bsz, s // ts),
        in_specs=[pl.BlockSpec((1, ts, d), lambda b, i: (b, i, 0)),
                  pl.BlockSpec((1, N_MOD, d), lambda b, i: (b, 0, 0)),
                  pl.BlockSpec((1, d), lambda b, i: (0, 0)),
                  pl.BlockSpec((d, n_pad), lambda b, i: (0, 0))],
        out_specs=[qkv_spec] * 6 + [pl.BlockSpec((1, ts, n_fox_heads), lambda b, i: (b, i, 0))],
        out_shape=[qkv_shape] * 6 + [jax.ShapeDtypeStruct((bsz, s, n_fox_heads), F32)],
        compiler_params=_params(("arbitrary", "arbitrary")),
        name="in_projection",
    )(x, mod3, g.reshape(1, d), w_pad)


def _cum_kernel(fl_ref, fb_ref, o_ref):
    z = fl_ref[0] + fb_ref[...]
    acc = jnp.minimum(z, 0.0) - jnp.log1p(jnp.exp(-jnp.abs(z)))
    s = acc.shape[1]
    pos = lax.broadcasted_iota(jnp.int32, acc.shape, 1)
    shift = 1
    while shift < s:
        acc = acc + jnp.where(pos >= shift, pltpu.roll(acc, shift, 1), 0.0)
        shift *= 2
    o_ref[0] = acc


def _forget_cumsum(fl_t, forget_b):
    bsz, h, s = fl_t.shape
    return pl.pallas_call(
        _cum_kernel,
        grid=(bsz,),
        in_specs=[pl.BlockSpec((1, h, s), lambda b: (b, 0, 0)),
                  pl.BlockSpec((h, 1), lambda b: (0, 0))],
        out_specs=pl.BlockSpec((1, h, s), lambda b: (b, 0, 0)),
        out_shape=jax.ShapeDtypeStruct((bsz, h, s), F32),
        compiler_params=_params(("arbitrary",)),
        name="forget_cumsum",
    )(fl_t, forget_b.reshape(h, 1))


def _bias_kernel(rb_ref, o_ref, *, tile, n_delta):
    h = pl.program_id(0)
    u = lax.broadcasted_iota(jnp.int32, (1, 2 * tile), 1)
    for dd in range(n_delta):
        n = jnp.maximum(dd * tile + tile - u, 0)
        nf = jnp.maximum(n, 1).astype(F32)
        large = MAX_EXACT + (jnp.log(nf / MAX_EXACT) / math.log(MAX_DISTANCE / MAX_EXACT)
                             * (NUM_BUCKETS - MAX_EXACT)).astype(jnp.int32)
        large = jnp.minimum(large, NUM_BUCKETS - 1)
        bucket = jnp.where(n < MAX_EXACT, n, large)
        row = jnp.zeros((1, 2 * tile), F32)
        for b in range(NUM_BUCKETS):
            row = jnp.where(bucket == b, rb_ref[h, b], row)
        full = pltpu.roll(jnp.broadcast_to(row, (tile, 2 * tile)), 0, 1, stride=1, stride_axis=0)
        o_ref[0, dd] = full[:, tile:]


def _bias_tiles(rel_bias_t, seq, tile):
    n_heads = rel_bias_t.shape[0]
    n_delta = seq // tile
    kern = functools.partial(_bias_kernel, tile=tile, n_delta=n_delta)
    return pl.pallas_call(
        kern,
        grid=(n_heads,),
        in_specs=[pl.BlockSpec(memory_space=pltpu.SMEM)],
        out_specs=pl.BlockSpec((1, n_delta, tile, tile), lambda h: (h, 0, 0, 0)),
        out_shape=jax.ShapeDtypeStruct((n_heads, n_delta, tile, tile), F32),
        compiler_params=_params(("arbitrary",)),
        name="bias_tiles",
    )(rel_bias_t)


def _flash_pair(q_ref, k_ref, v_ref, m_scr, l_scr, acc_scr, extra_fn):
    i = pl.program_id(2)
    tile = q_ref.shape[1]
    q = q_ref[0]
    lane = lax.broadcasted_iota(jnp.int32, q.shape, 1)
    zero = jnp.zeros_like(q)
    q_maps = (jnp.where(lane < DIFF_QK_DIM, q, zero), jnp.where(lane >= DIFF_QK_DIM, q, zero))
    m_scr[...] = jnp.full(m_scr.shape, NEG_INF, F32)
    l_scr[...] = jnp.zeros(l_scr.shape, F32)
    acc_scr[...] = jnp.zeros(acc_scr.shape, F32)

    def step(j, diag):
        start = pl.multiple_of(j * tile, tile)
        k = k_ref[0, pl.ds(start, tile), :]
        v = v_ref[0, pl.ds(start, tile), :]
        extras = extra_fn(j, diag)
        if diag:
            row = lax.broadcasted_iota(jnp.int32, (tile, tile), 0)
            col = lax.broadcasted_iota(jnp.int32, (tile, tile), 1)
            causal = row >= col
        for idx in range(2):
            s = lax.dot_general(q_maps[idx], k, (((1,), (1,)), ((), ())),
                                preferred_element_type=F32)
            s = s + extras[idx]
            if diag:
                s = jnp.where(causal, s, NEG_INF)
            m_prev = m_scr[idx]
            m_new = jnp.maximum(m_prev, jnp.max(s, axis=1, keepdims=True))
            p = jnp.exp(s - m_new)
            alpha = jnp.exp(m_prev - m_new)
            l_scr[idx] = alpha * l_scr[idx] + jnp.sum(p, axis=1, keepdims=True)
            acc_scr[idx] = alpha * acc_scr[idx] + jnp.dot(p.astype(BF16), v,
                                                          preferred_element_type=F32)
            m_scr[idx] = m_new

    def body(j, carry):
        step(j, False)
        return carry

    lax.fori_loop(0, i, body, 0)
    step(i, True)


def _diff_attn_kernel(q_ref, k_ref, v_ref, bias_ref, lam_ref, g_ref, o_ref,
                      m_scr, l_scr, acc_scr, *, lambda_init):
    i = pl.program_id(2)

    def extra_fn(j, diag):
        b = bias_ref[0, 0] if diag else bias_ref[0, i - j]
        return (b, b)

    _flash_pair(q_ref, k_ref, v_ref, m_scr, l_scr, acc_scr, extra_fn)
    lam = (jnp.exp(jnp.sum(lam_ref[0:1, :] * lam_ref[1:2, :], axis=-1, keepdims=True))
           - jnp.exp(jnp.sum(lam_ref[2:3, :] * lam_ref[3:4, :], axis=-1, keepdims=True))
           + lambda_init)
    o = acc_scr[0] / l_scr[0] - lam * (acc_scr[1] / l_scr[1])
    o = (o * _rms_scale(o) * g_ref[...]) * (1.0 - lambda_init)
    o_ref[0] = o.astype(o_ref.dtype)


def _fox_attn_kernel(q_ref, k_ref, v_ref, cq_ref, ck_ref, o_ref, m_scr, l_scr, acc_scr):
    g = pl.program_id(0)
    tile = q_ref.shape[1]
    cq = cq_ref[0]

    def extra_fn(j, diag):
        start = pl.multiple_of(j * tile, tile)
        ck = ck_ref[0, :, pl.ds(start, tile)]
        return (cq[:, 0:1] - ck[0:1, :], cq[:, 1:2] - ck[1:2, :])

    del g
    _flash_pair(q_ref, k_ref, v_ref, m_scr, l_scr, acc_scr, extra_fn)
    lane = lax.broadcasted_iota(jnp.int32, acc_scr.shape[1:], 1)
    o = jnp.where(lane < FOX_HEAD_DIM, acc_scr[0] / l_scr[0], acc_scr[1] / l_scr[1])
    o_ref[0] = o.astype(o_ref.dtype)


def _attn_scratch(tile):
    return [pltpu.VMEM((2, tile, 1), F32), pltpu.VMEM((2, tile, 1), F32),
            pltpu.VMEM((2, tile, LANES), F32)]


def _diff_attention(dq, dk, dv, bias_tiles, lam_vecs, subln_g, lambda_init):
    bsz, s, width = dq.shape
    tile = ATTN_TILE
    n_groups = width // LANES
    n_delta = s // tile
    kern = functools.partial(_diff_attn_kernel, lambda_init=lambda_init)
    return pl.pallas_call(
        kern,
        grid=(n_groups, bsz, s // tile),
        in_specs=[pl.BlockSpec((1, tile, LANES), lambda g, b, i: (b, i, g)),
                  pl.BlockSpec((1, s, LANES), lambda g, b, i: (b, 0, g)),
                  pl.BlockSpec((1, s, LANES), lambda g, b, i: (b, 0, g)),
                  pl.BlockSpec((1, n_delta, tile, tile), lambda g, b, i: (g, 0, 0, 0)),
                  pl.BlockSpec(lam_vecs.shape, lambda g, b, i: (0, 0)),
                  pl.BlockSpec((1, LANES), lambda g, b, i: (0, 0))],
        out_specs=pl.BlockSpec((1, tile, LANES), lambda g, b, i: (b, i, g)),
        out_shape=jax.ShapeDtypeStruct((bsz, s, width), BF16),
        scratch_shapes=_attn_scratch(tile),
        compiler_params=_params(("arbitrary", "arbitrary", "arbitrary")),
        name="diff_attention",
    )(dq, dk, dv, bias_tiles, lam_vecs, subln_g.reshape(1, LANES))


def _fox_attention(fq, fk, fv, cum_col, cum_row):
    bsz, s, width = fq.shape
    tile = ATTN_TILE
    n_groups = width // LANES
    return pl.pallas_call(
        _fox_attn_kernel,
        grid=(n_groups, bsz, s // tile),
        in_specs=[pl.BlockSpec((1, tile, LANES), lambda g, b, i: (b, i, g)),
                  pl.BlockSpec((1, s, LANES), lambda g, b, i: (b, 0, g)),
                  pl.BlockSpec((1, s, LANES), lambda g, b, i: (b, 0, g)),
                  pl.BlockSpec((1, tile, 2), lambda g, b, i: (b * n_groups + g, i, 0)),
                  pl.BlockSpec((1, 2, s), lambda g, b, i: (b * n_groups + g, 0, 0))],
        out_specs=pl.BlockSpec((1, tile, LANES), lambda g, b, i: (b, i, g)),
        out_shape=jax.ShapeDtypeStruct((bsz, s, width), BF16),
        scratch_shapes=_attn_scratch(tile),
        compiler_params=_params(("arbitrary", "arbitrary", "arbitrary")),
        name="fox_attention",
    )(fq, fk, fv, cum_col, cum_row)


def _out_kernel(d_ref, f_ref, w_ref, x_ref, mod_ref, g_ref, x1_ref, h_ref, *, width):
    mix = (jnp.dot(d_ref[0], w_ref[:width, :], preferred_element_type=F32)
           + jnp.dot(f_ref[0], w_ref[width:, :], preferred_element_type=F32))
    x1 = x_ref[0] + mod_ref[0, 2:3, :] * mix
    x1_ref[0] = x1
    h = (x1 * _rms_scale(x1) * g_ref[...]) * (1.0 + mod_ref[0, 4:5, :]) + mod_ref[0, 3:4, :]
    h_ref[0] = h.astype(BF16)


def _out_projection(d_out, f_out, w_out, x, mod3, g):
    bsz, s, d = x.shape
    width = d_out.shape[2]
    ts = ROW_TILE
    row_spec = pl.BlockSpec((1, ts, d), lambda b, i: (b, i, 0))
    in_spec = pl.BlockSpec((1, ts, width), lambda b, i: (b, i, 0))
    kern = functools.partial(_out_kernel, width=width)
    return pl.pallas_call(
        kern,
        grid=(bsz, s // ts),
        in_specs=[in_spec, in_spec,
                  pl.BlockSpec(w_out.shape, lambda b, i: (0, 0)),
                  row_spec,
                  pl.BlockSpec((1, N_MOD, d), lambda b, i: (b, 0, 0)),
                  pl.BlockSpec((1, d), lambda b, i: (0, 0))],
        out_specs=[row_spec, row_spec],
        out_shape=[jax.ShapeDtypeStruct((bsz, s, d), F32), jax.ShapeDtypeStruct((bsz, s, d), BF16)],
        compiler_params=_params(("arbitrary", "arbitrary")),
        name="out_projection",
    )(d_out, f_out, w_out, x, mod3, g.reshape(1, d))


def _ffn_kernel(h_ref, x1_ref, mod_ref, wu_ref, cw_ref, cb_ref, wd_ref, fg_ref, o_ref,
                tail_scr, *, d_ff, chunk, final_norm):
    i = pl.program_id(1)
    ts = h_ref.shape[1]

    @pl.when(i == 0)
    def _():
        tail_scr[...] = jnp.zeros(tail_scr.shape, F32)

    hb = h_ref[0]

    def conv_cols(c0):
        u = jnp.dot(hb, wu_ref[:, c0:c0 + chunk], preferred_element_type=F32)
        ext = jnp.concatenate([tail_scr[:, c0:c0 + chunk], u], axis=0)
        tail_scr[:, c0:c0 + chunk] = u[ts - SUBLANES:, :]
        y = cb_ref[:, c0:c0 + chunk]
        for tap in range(CONV_WIDTH):
            lo = SUBLANES - (CONV_WIDTH - 1) + tap
            y = y + cw_ref[tap:tap + 1, c0:c0 + chunk] * ext[lo:lo + ts, :]
        return y

    acc = jnp.zeros((ts, o_ref.shape[2]), F32)
    for c in range(d_ff // chunk):
        gate = conv_cols(c * chunk)
        val = conv_cols(d_ff + c * chunk)
        act = (gate * jax.nn.sigmoid(gate) * val).astype(BF16)
        acc = acc + jnp.dot(act, wd_ref[c * chunk:(c + 1) * chunk, :], preferred_element_type=F32)
    x2 = x1_ref[0] + mod_ref[0, 5:6, :] * acc
    if final_norm:
        x2 = x2 * _rms_scale(x2) * fg_ref[...]
    o_ref[0] = x2


def _ffn(h2, x1, mod3, w_up, conv_w, conv_b, w_down, final_g, final_norm):
    bsz, s, d = x1.shape
    d_ff = w_down.shape[0]
    ts = ROW_TILE
    row_spec = pl.BlockSpec((1, ts, d), lambda b, i: (b, i, 0))
    kern = functools.partial(_ffn_kernel, d_ff=d_ff, chunk=FFN_CHUNK, final_norm=final_norm)
    const = lambda b, i: (0, 0)
    return pl.pallas_call(
        kern,
        grid=(bsz, s // ts),
        in_specs=[row_spec, row_spec,
                  pl.BlockSpec((1, N_MOD, d), lambda b, i: (b, 0, 0)),
                  pl.BlockSpec(w_up.shape, const, pipeline_mode=pl.Buffered(1)),
                  pl.BlockSpec(conv_w.shape, const),
                  pl.BlockSpec((1, 2 * d_ff), const),
                  pl.BlockSpec(w_down.shape, const, pipeline_mode=pl.Buffered(1)),
                  pl.BlockSpec((1, d), const)],
        out_specs=row_spec,
        out_shape=jax.ShapeDtypeStruct((bsz, s, d), F32),
        scratch_shapes=[pltpu.VMEM((SUBLANES, 2 * d_ff), F32)],
        compiler_params=_params(("arbitrary", "arbitrary")),
        name="conv_ffn",
    )(h2, x1, mod3, w_up, conv_w, conv_b.reshape(1, 2 * d_ff), w_down, final_g.reshape(1, d))


def kernel(x, c, ada_w, ada_b, attn_norm_g, w_in, forget_b, lambda_q1, lambda_k1, lambda_q2,
           lambda_k2, subln_g, rel_bias, w_out, ffn_norm_g, w_up, conv_w, conv_b, w_down,
           final_norm_g):
    bsz, s, d = x.shape
    depth = ada_w.shape[0]
    n_fox_heads = forget_b.shape[1]
    width = (w_in.shape[2] - n_fox_heads) // 6
    assert width % LANES == 0 and subln_g.shape[1] == DIFF_V_DIM == LANES
    assert n_fox_heads * FOX_HEAD_DIM == width and s % ATTN_TILE == 0 and s % ROW_TILE == 0
    n_pairs = n_fox_heads // 2

    bias = _bias_tiles(rel_bias.T, s, ATTN_TILE)

    for l in range(depth):
        lambda_init = 0.8 - 0.6 * math.exp(-0.3 * l)
        mod3 = _modulation(c, ada_w[l], ada_b[l]).reshape(bsz, N_MOD, d)

        w_pad = jnp.pad(w_in[l], ((0, 0), (0, LANES - n_fox_heads))).astype(BF16)
        dq, dk, dv, fq, fk, fv, fl = _in_projection(x, mod3, attn_norm_g[l], w_pad,
                                                    width=width, n_fox_heads=n_fox_heads)

        cum_t = _forget_cumsum(jnp.transpose(fl, (0, 2, 1)), forget_b[l])
        cum_row = cum_t.reshape(bsz * n_pairs, 2, s)
        cum_col = jnp.transpose(cum_row, (0, 2, 1))

        lam_vecs = jnp.stack([lambda_q1[l], lambda_k1[l], lambda_q2[l], lambda_k2[l]]).astype(F32)
        d_out = _diff_attention(dq, dk, dv, bias, lam_vecs, subln_g[l], lambda_init)
        f_out = _fox_attention(fq, fk, fv, cum_col, cum_row)

        x1, h2 = _out_projection(d_out, f_out, w_out[l].astype(BF16), x, mod3, ffn_norm_g[l])
        x = _ffn(h2, x1, mod3, w_up[l].astype(BF16), conv_w[l], conv_b[l],
                 w_down[l].astype(BF16), final_norm_g, final_norm=(l == depth - 1))
    return x
```

```python
import functools
import math

import jax
import jax.numpy as jnp
from jax import lax
from jax.experimental import pallas as pl
from jax.experimental.pallas import tpu as pltpu

F32 = jnp.float32
BF16 = jnp.bfloat16

DIFF_QK_DIM = 64
DIFF_V_DIM = 2 * DIFF_QK_DIM
FOX_HEAD_DIM = 64
CONV_WIDTH = 3
NUM_BUCKETS = 32
MAX_EXACT = NUM_BUCKETS // 2
MAX_DISTANCE = 128
N_MOD = 6
NORM_EPS = 1e-6
NEG_INF = -1e30

LANES = 128
SUBLANES = 8
BF16_ROWS = 16
MXU_DEPTH = 256
VMEM_LIMIT_BYTES = 56 * 1024 * 1024

ATTN_TILE = 512
ROW_TILE = 512
FFN_CHUNK = 256
MOD_COL_TILE = 1536

NT_DIMS = (((1,), (1,)), ((), ()))


def _params(semantics):
    return pltpu.CompilerParams(dimension_semantics=semantics, vmem_limit_bytes=VMEM_LIMIT_BYTES)


def _rms_scale(x, axis=-1):
    return lax.rsqrt(jnp.mean(x * x, axis=axis, keepdims=True) + NORM_EPS)


def _mod_kernel(c_ref, w_ref, b_ref, o_ref):
    c = c_ref[...]
    act = c * jax.nn.sigmoid(c)
    o_ref[...] = jnp.dot(act.astype(BF16), w_ref[...].astype(BF16),
                         preferred_element_type=F32) + b_ref[...]


def _modulation(c, w, b):
    bsz, d = c.shape
    n = w.shape[1]
    tn = MOD_COL_TILE
    return pl.pallas_call(
        _mod_kernel,
        grid=(n // tn,),
        in_specs=[pl.BlockSpec((bsz, d), lambda j: (0, 0)),
                  pl.BlockSpec((d, tn), lambda j: (0, j)),
                  pl.BlockSpec((1, tn), lambda j: (0, j))],
        out_specs=pl.BlockSpec((bsz, tn), lambda j: (0, j)),
        out_shape=jax.ShapeDtypeStruct((bsz, n), F32),
        compiler_params=_params(("arbitrary",)),
        name="modulation",
    )(c, w, b.reshape(1, n))


def _proj_kernel(x_ref, mod_ref, g_ref, wt_ref, wk_ref, fm_ref, dk_ref, fk_ref, fl_ref,
                 *, width, n_fox_heads):
    x = x_ref[0]
    h = (x * _rms_scale(x) * g_ref[...]) * (1.0 + mod_ref[0, 1:2, :]) + mod_ref[0, 0:1, :]
    hb = h.astype(BF16)
    fm_ref[0, 0] = lax.dot_general(wt_ref[...], hb, NT_DIMS,
                                   preferred_element_type=F32).astype(BF16)
    r = jnp.dot(hb, wk_ref[...], preferred_element_type=F32)
    dk_ref[0] = r[:, :width].astype(BF16)
    fk_ref[0] = r[:, width:2 * width].astype(BF16)
    fl_ref[0] = r[:, 2 * width:2 * width + n_fox_heads]


def _in_projection(x, mod3, g, w_fm_t, w_tm, *, width, n_fox_heads):
    bsz, s, d = x.shape
    ts = ATTN_TILE
    k_shape = jax.ShapeDtypeStruct((bsz, s, width), BF16)
    k_spec = pl.BlockSpec((1, ts, width), lambda b, i: (b, i, 0))
    kern = functools.partial(_proj_kernel, width=width, n_fox_heads=n_fox_heads)
    return pl.pallas_call(
        kern,
        grid=(bsz, s // ts),
        in_specs=[pl.BlockSpec((1, ts, d), lambda b, i: (b, i, 0)),
                  pl.BlockSpec((1, N_MOD, d), lambda b, i: (b, 0, 0)),
                  pl.BlockSpec((1, d), lambda b, i: (0, 0)),
                  pl.BlockSpec(w_fm_t.shape, lambda b, i: (0, 0)),
                  pl.BlockSpec(w_tm.shape, lambda b, i: (0, 0))],
        out_specs=[pl.BlockSpec((1, 1, 4 * width, ts), lambda b, i: (b, i, 0, 0)),
                   k_spec, k_spec,
                   pl.BlockSpec((1, ts, n_fox_heads), lambda b, i: (b, i, 0))],
        out_shape=[jax.ShapeDtypeStruct((bsz, s // ts, 4 * width, ts), BF16),
                   k_shape, k_shape,
                   jax.ShapeDtypeStruct((bsz, s, n_fox_heads), F32)],
        compiler_params=_params(("arbitrary", "arbitrary")),
        name="in_projection",
    )(x, mod3, g.reshape(1, d), w_fm_t, w_tm)


def _split3(x):
    hi = x.astype(BF16)
    r1 = x - hi.astype(F32)
    mid = r1.astype(BF16)
    lo = (r1 - mid.astype(F32)).astype(BF16)
    return hi, mid, lo


def _cum_kernel(fl_ref, fb_ref, qa_ref, ka_ref):
    z = fl_ref[0] + fb_ref[...]
    acc = jnp.minimum(z, 0.0) - jnp.log1p(jnp.exp(-jnp.abs(z)))
    n_heads, s = acc.shape
    pos = lax.broadcasted_iota(jnp.int32, acc.shape, 1)
    shift = 1
    while shift < s:
        acc = acc + jnp.where(pos >= shift, pltpu.roll(acc, shift, 1), 0.0)
        shift *= 2
    q_parts = _split3(acc)
    k_parts = _split3(-acc)
    ones = jnp.ones((3, s), BF16)
    zeros = jnp.zeros((BF16_ROWS - 6, s), BF16)
    for h in range(n_heads):
        qa_ref[0, h] = jnp.concatenate([p[h:h + 1] for p in q_parts] + [ones, zeros], axis=0)
        ka_ref[0, h] = jnp.concatenate([ones] + [p[h:h + 1] for p in k_parts] + [zeros], axis=0)


def _forget_aug(fl_t, forget_b):
    bsz, h, s = fl_t.shape
    aug_shape = jax.ShapeDtypeStruct((bsz, h, BF16_ROWS, s), BF16)
    aug_spec = pl.BlockSpec((1, h, BF16_ROWS, s), lambda b: (b, 0, 0, 0))
    return pl.pallas_call(
        _cum_kernel,
        grid=(bsz,),
        in_specs=[pl.BlockSpec((1, h, s), lambda b: (b, 0, 0)),
                  pl.BlockSpec((h, 1), lambda b: (0, 0))],
        out_specs=[aug_spec, aug_spec],
        out_shape=[aug_shape, aug_shape],
        compiler_params=_params(("arbitrary",)),
        name="forget_cumsum",
    )(fl_t, forget_b.reshape(h, 1))


def _bias_kernel(rb_ref, o_ref, *, tile, n_delta):
    h = pl.program_id(0)
    u = lax.broadcasted_iota(jnp.int32, (1, 2 * tile), 1)
    for dd in range(n_delta):
        n = jnp.maximum(dd * tile + u - tile, 0)
        nf = jnp.maximum(n, 1).astype(F32)
        large = MAX_EXACT + (jnp.log(nf / MAX_EXACT) / math.log(MAX_DISTANCE / MAX_EXACT)
                             * (NUM_BUCKETS - MAX_EXACT)).astype(jnp.int32)
        large = jnp.minimum(large, NUM_BUCKETS - 1)
        bucket = jnp.where(n < MAX_EXACT, n, large)
        row = jnp.zeros((1, 2 * tile), F32)
        for b in range(NUM_BUCKETS):
            row = jnp.where(bucket == b, rb_ref[h, b], row)
        full = pltpu.roll(jnp.broadcast_to(row, (tile, 2 * tile)), 0, 1, stride=1, stride_axis=0)
        o_ref[0, dd] = full[:, tile:]


def _bias_tiles(rel_bias_t, seq, tile):
    n_heads = rel_bias_t.shape[0]
    n_delta = seq // tile
    kern = functools.partial(_bias_kernel, tile=tile, n_delta=n_delta)
    return pl.pallas_call(
        kern,
        grid=(n_heads,),
        in_specs=[pl.BlockSpec(memory_space=pltpu.SMEM)],
        out_specs=pl.BlockSpec((1, n_delta, tile, tile), lambda h: (h, 0, 0, 0)),
        out_shape=jax.ShapeDtypeStruct((n_heads, n_delta, tile, tile), F32),
        compiler_params=_params(("arbitrary",)),
        name="bias_tiles",
    )(rel_bias_t)


def _flash_pair(w_maps, keys_fn, values_fn, bias_fn, m_scr, l_scr, acc_scr, tile):
    i = pl.program_id(2)
    m_scr[...] = jnp.full(m_scr.shape, NEG_INF, F32)
    l_scr[...] = jnp.zeros(l_scr.shape, F32)
    acc_scr[...] = jnp.zeros(acc_scr.shape, F32)

    def step(j, diag):
        keys = keys_fn(j)
        bias = bias_fn(j, diag)
        if diag:
            key_pos = lax.broadcasted_iota(jnp.int32, (tile, tile), 0)
            qry_pos = lax.broadcasted_iota(jnp.int32, (tile, tile), 1)
            causal = key_pos <= qry_pos
        for idx in range(2):
            st = jnp.dot(keys, w_maps[idx], preferred_element_type=F32)
            if bias is not None:
                st = st + bias
            if diag:
                st = jnp.where(causal, st, NEG_INF)
            m_prev = m_scr[idx]
            m_new = jnp.maximum(m_prev, jnp.max(st, axis=0, keepdims=True))
            p = jnp.exp(st - m_new)
            alpha = jnp.exp(m_prev - m_new)
            l_scr[idx] = alpha * l_scr[idx] + jnp.sum(p, axis=0, keepdims=True)
            acc_scr[idx] = alpha * acc_scr[idx] + jnp.dot(values_fn(j, idx), p.astype(BF16),
                                                          preferred_element_type=F32)
            m_scr[idx] = m_new

    def body(j, carry):
        step(j, False)
        return carry

    lax.fori_loop(0, i, body, 0)
    step(i, True)


def _masked_maps(qt):
    feat = lax.broadcasted_iota(jnp.int32, qt.shape, 0)
    zero = jnp.zeros_like(qt)
    return jnp.where(feat < DIFF_QK_DIM, qt, zero), jnp.where(feat >= DIFF_QK_DIM, qt, zero)


def _diff_attn_kernel(qt_ref, k_ref, vt_ref, bias_ref, lam_ref, g_ref, o_ref,
                      m_scr, l_scr, acc_scr, *, lambda_init):
    i = pl.program_id(2)
    tile = qt_ref.shape[3]
    w_maps = _masked_maps(qt_ref[0, 0])

    def keys_fn(j):
        return k_ref[0, pl.ds(pl.multiple_of(j * tile, tile), tile), :]

    def values_fn(j, idx):
        return vt_ref[0, j]

    def bias_fn(j, diag):
        return bias_ref[0, 0] if diag else bias_ref[0, i - j]

    _flash_pair(w_maps, keys_fn, values_fn, bias_fn, m_scr, l_scr, acc_scr, tile)
    lam = (jnp.exp(jnp.sum(lam_ref[0:1, :] * lam_ref[1:2, :], axis=-1, keepdims=True))
           - jnp.exp(jnp.sum(lam_ref[2:3, :] * lam_ref[3:4, :], axis=-1, keepdims=True))
           + lambda_init)
    o = acc_scr[0] / l_scr[0] - lam * (acc_scr[1] / l_scr[1])
    o = (o * _rms_scale(o, axis=0) * g_ref[...]) * (1.0 - lambda_init)
    o_ref[0] = o.T.astype(o_ref.dtype)


def _fox_attn_kernel(qt_ref, qa_ref, k_ref, ka_ref, vt_ref, o_ref, m_scr, l_scr, acc_scr):
    tile = qt_ref.shape[3]
    q_a, q_b = _masked_maps(qt_ref[0, 0])
    aug_zero = jnp.zeros((BF16_ROWS, tile), BF16)
    pad = jnp.zeros((MXU_DEPTH - LANES - 2 * BF16_ROWS, tile), BF16)
    w_maps = (jnp.concatenate([q_a, qa_ref[0, 0], aug_zero, pad], axis=0),
              jnp.concatenate([q_b, aug_zero, qa_ref[0, 1], pad], axis=0))

    def keys_fn(j):
        rows = pl.ds(pl.multiple_of(j * tile, tile), tile)
        return jnp.concatenate([k_ref[0, rows, :], ka_ref[0, rows, :]], axis=1)

    def values_fn(j, idx):
        return vt_ref[0, j, idx * FOX_HEAD_DIM:(idx + 1) * FOX_HEAD_DIM, :]

    _flash_pair(w_maps, keys_fn, values_fn, lambda j, diag: None, m_scr, l_scr, acc_scr, tile)
    o = jnp.concatenate([acc_scr[0] / l_scr[0], acc_scr[1] / l_scr[1]], axis=0)
    o_ref[0] = o.T.astype(o_ref.dtype)


def _attn_scratch(tile, acc_rows):
    return [pltpu.VMEM((2, 1, tile), F32), pltpu.VMEM((2, 1, tile), F32),
            pltpu.VMEM((2, acc_rows, tile), F32)]


def _diff_attention(fm, dk, bias_tiles, lam_vecs, subln_g, lambda_init):
    bsz, n_tiles, _, tile = fm.shape
    s = n_tiles * tile
    width = dk.shape[2]
    n_groups = width // LANES
    kern = functools.partial(_diff_attn_kernel, lambda_init=lambda_init)
    return pl.pallas_call(
        kern,
        grid=(n_groups, bsz, n_tiles),
        in_specs=[pl.BlockSpec((1, 1, LANES, tile), lambda g, b, i: (b, i, g, 0)),
                  pl.BlockSpec((1, s, LANES), lambda g, b, i: (b, 0, g)),
                  pl.BlockSpec((1, n_tiles, LANES, tile), lambda g, b, i: (b, 0, n_groups + g, 0)),
                  pl.BlockSpec((1, n_tiles, tile, tile), lambda g, b, i: (g, 0, 0, 0)),
                  pl.BlockSpec(lam_vecs.shape, lambda g, b, i: (0, 0)),
                  pl.BlockSpec((LANES, 1), lambda g, b, i: (0, 0))],
        out_specs=pl.BlockSpec((1, tile, LANES), lambda g, b, i: (b, i, g)),
        out_shape=jax.ShapeDtypeStruct((bsz, s, width), BF16),
        scratch_shapes=_attn_scratch(tile, DIFF_V_DIM),
        compiler_params=_params(("arbitrary", "arbitrary", "arbitrary")),
        name="diff_attention",
    )(fm, dk, fm, bias_tiles, lam_vecs, subln_g.reshape(LANES, 1))


def _fox_attention(fm, fk, q_aug, k_aug):
    bsz, n_tiles, _, tile = fm.shape
    s = n_tiles * tile
    width = fk.shape[2]
    n_groups = width // LANES
    return pl.pallas_call(
        _fox_attn_kernel,
        grid=(n_groups, bsz, n_tiles),
        in_specs=[pl.BlockSpec((1, 1, LANES, tile), lambda g, b, i: (b, i, 2 * n_groups + g, 0)),
                  pl.BlockSpec((1, 2, BF16_ROWS, tile), lambda g, b, i: (b, g, 0, i)),
                  pl.BlockSpec((1, s, LANES), lambda g, b, i: (b, 0, g)),
                  pl.BlockSpec((1, s, LANES), lambda g, b, i: (b * n_groups + g, 0, 0)),
                  pl.BlockSpec((1, n_tiles, LANES, tile),
                               lambda g, b, i: (b, 0, 3 * n_groups + g, 0))],
        out_specs=pl.BlockSpec((1, tile, LANES), lambda g, b, i: (b, i, g)),
        out_shape=jax.ShapeDtypeStruct((bsz, s, width), BF16),
        scratch_shapes=_attn_scratch(tile, FOX_HEAD_DIM),
        compiler_params=_params(("arbitrary", "arbitrary", "arbitrary")),
        name="fox_attention",
    )(fm, q_aug, fk, k_aug, fm)


def _out_kernel(d_ref, f_ref, w_ref, x_ref, mod_ref, g_ref, x1_ref, h_ref, *, width):
    mix = (jnp.dot(d_ref[0], w_ref[:width, :], preferred_element_type=F32)
           + jnp.dot(f_ref[0], w_ref[width:, :], preferred_element_type=F32))
    x1 = x_ref[0] + mod_ref[0, 2:3, :] * mix
    x1_ref[0] = x1
    h = (x1 * _rms_scale(x1) * g_ref[...]) * (1.0 + mod_ref[0, 4:5, :]) + mod_ref[0, 3:4, :]
    h_ref[0] = h.astype(BF16)


def _out_projection(d_out, f_out, w_out, x, mod3, g):
    bsz, s, d = x.shape
    width = d_out.shape[2]
    ts = ROW_TILE
    row_spec = pl.BlockSpec((1, ts, d), lambda b, i: (b, i, 0))
    in_spec = pl.BlockSpec((1, ts, width), lambda b, i: (b, i, 0))
    kern = functools.partial(_out_kernel, width=width)
    return pl.pallas_call(
        kern,
        grid=(bsz, s // ts),
        in_specs=[in_spec, in_spec,
                  pl.BlockSpec(w_out.shape, lambda b, i: (0, 0)),
                  row_spec,
                  pl.BlockSpec((1, N_MOD, d), lambda b, i: (b, 0, 0)),
                  pl.BlockSpec((1, d), lambda b, i: (0, 0))],
        out_specs=[row_spec, row_spec],
        out_shape=[jax.ShapeDtypeStruct((bsz, s, d), F32), jax.ShapeDtypeStruct((bsz, s, d), BF16)],
        compiler_params=_params(("arbitrary", "arbitrary")),
        name="out_projection",
    )(d_out, f_out, w_out, x, mod3, g.reshape(1, d))


def _ffn_kernel(h_ref, x1_ref, mod_ref, wu_ref, cw_ref, cb_ref, wd_ref, fg_ref, o_ref,
                tail_scr, *, d_ff, chunk, final_norm):
    i = pl.program_id(1)
    ts = h_ref.shape[1]

    @pl.when(i == 0)
    def _():
        tail_scr[...] = jnp.zeros(tail_scr.shape, F32)

    hb = h_ref[0]

    def conv_cols(c0):
        u = jnp.dot(hb, wu_ref[:, c0:c0 + chunk], preferred_element_type=F32)
        ext = jnp.concatenate([tail_scr[:, c0:c0 + chunk], u], axis=0)
        tail_scr[:, c0:c0 + chunk] = u[ts - SUBLANES:, :]
        y = cb_ref[:, c0:c0 + chunk]
        for tap in range(CONV_WIDTH):
            lo = SUBLANES - (CONV_WIDTH - 1) + tap
            y = y + cw_ref[tap:tap + 1, c0:c0 + chunk] * ext[lo:lo + ts, :]
        return y

    acc = jnp.zeros((ts, o_ref.shape[2]), F32)
    for c in range(d_ff // chunk):
        gate = conv_cols(c * chunk)
        val = conv_cols(d_ff + c * chunk)
        act = (gate * jax.nn.sigmoid(gate) * val).astype(BF16)
        acc = acc + jnp.dot(act, wd_ref[c * chunk:(c + 1) * chunk, :], preferred_element_type=F32)
    x2 = x1_ref[0] + mod_ref[0, 5:6, :] * acc
    if final_norm:
        x2 = x2 * _rms_scale(x2) * fg_ref[...]
    o_ref[0] = x2


def _ffn(h2, x1, mod3, w_up, conv_w, conv_b, w_down, final_g, final_norm):
    bsz, s, d = x1.shape
    d_ff = w_down.shape[0]
    ts = ROW_TILE
    row_spec = pl.BlockSpec((1, ts, d), lambda b, i: (b, i, 0))
    kern = functools.partial(_ffn_kernel, d_ff=d_ff, chunk=FFN_CHUNK, final_norm=final_norm)
    const = lambda b, i: (0, 0)
    return pl.pallas_call(
        kern,
        grid=(bsz, s // ts),
        in_specs=[row_spec, row_spec,
                  pl.BlockSpec((1, N_MOD, d), lambda b, i: (b, 0, 0)),
                  pl.BlockSpec(w_up.shape, const, pipeline_mode=pl.Buffered(1)),
                  pl.BlockSpec(conv_w.shape, const),
                  pl.BlockSpec((1, 2 * d_ff), const),
                  pl.BlockSpec(w_down.shape, const, pipeline_mode=pl.Buffered(1)),
                  pl.BlockSpec((1, d), const)],
        out_specs=row_spec,
        out_shape=jax.ShapeDtypeStruct((bsz, s, d), F32),
        scratch_shapes=[pltpu.VMEM((SUBLANES, 2 * d_ff), F32)],
        compiler_params=_params(("arbitrary", "arbitrary")),
        name="conv_ffn",
    )(h2, x1, mod3, w_up, conv_w, conv_b.reshape(1, 2 * d_ff), w_down, final_g.reshape(1, d))


def kernel(x, c, ada_w, ada_b, attn_norm_g, w_in, forget_b, lambda_q1, lambda_k1, lambda_q2,
           lambda_k2, subln_g, rel_bias, w_out, ffn_norm_g, w_up, conv_w, conv_b, w_down,
           final_norm_g):
    bsz, s, d = x.shape
    depth = ada_w.shape[0]
    n_fox_heads = forget_b.shape[1]
    width = (w_in.shape[2] - n_fox_heads) // 6
    assert width % LANES == 0 and subln_g.shape[1] == DIFF_V_DIM == LANES
    assert n_fox_heads * FOX_HEAD_DIM == width and s % ATTN_TILE == 0 and s % ROW_TILE == 0
    n_pairs = n_fox_heads // 2
    qk_scale = DIFF_QK_DIM ** -0.5

    bias = _bias_tiles(rel_bias.T, s, ATTN_TILE)

    for l in range(depth):
        lambda_init = 0.8 - 0.6 * math.exp(-0.3 * l)
        mod3 = _modulation(c, ada_w[l], ada_b[l]).reshape(bsz, N_MOD, d)

        w = w_in[l]
        cols = lambda n: w[:, n * width:(n + 1) * width]
        w_fm_t = jnp.concatenate([cols(0) * qk_scale, cols(2), cols(3) * qk_scale, cols(5)],
                                 axis=1).T.astype(BF16)
        w_tm = jnp.concatenate([cols(1), cols(4),
                                jnp.pad(w[:, 6 * width:], ((0, 0), (0, LANES - n_fox_heads)))],
                               axis=1).astype(BF16)
        fm, dk, fk, fl = _in_projection(x, mod3, attn_norm_g[l], w_fm_t, w_tm,
                                        width=width, n_fox_heads=n_fox_heads)

        q_aug, k_aug_t = _forget_aug(jnp.transpose(fl, (0, 2, 1)), forget_b[l])
        k_aug = jnp.transpose(k_aug_t.reshape(bsz * n_pairs, 2 * BF16_ROWS, s), (0, 2, 1))
        k_aug = jnp.pad(k_aug, ((0, 0), (0, 0), (0, LANES - 2 * BF16_ROWS)))

        lam_vecs = jnp.stack([lambda_q1[l], lambda_k1[l], lambda_q2[l], lambda_k2[l]]).astype(F32)
        d_out = _diff_attention(fm, dk, bias, lam_vecs, subln_g[l], lambda_init)
        f_out = _fox_attention(fm, fk, q_aug, k_aug)

        x1, h2 = _out_projection(d_out, f_out, w_out[l].astype(BF16), x, mod3, ffn_norm_g[l])
        x = _ffn(h2, x1, mod3, w_up[l].astype(BF16), conv_w[l], conv_b[l],
                 w_down[l].astype(BF16), final_norm_g, final_norm=(l == depth - 1))
    return x
```

```python
import functools
import math

import jax
import jax.numpy as jnp
from jax import lax
from jax.experimental import pallas as pl
from jax.experimental.pallas import tpu as pltpu

F32 = jnp.float32
BF16 = jnp.bfloat16

DIFF_QK_DIM = 64
DIFF_V_DIM = 2 * DIFF_QK_DIM
FOX_HEAD_DIM = 64
CONV_WIDTH = 3
NUM_BUCKETS = 32
MAX_EXACT = NUM_BUCKETS // 2
MAX_DISTANCE = 128
N_MOD = 6
NORM_EPS = 1e-6
NEG_INF = -1e30

LANES = 128
SUBLANES = 8
BF16_ROWS = 16
MXU_DEPTH = 256
VMEM_LIMIT_BYTES = 56 * 1024 * 1024

ATTN_TILE = 512
ROW_TILE = 512
FFN_CHUNK = 256
MOD_COL_TILE = 1536

NT_DIMS = (((1,), (1,)), ((), ()))


def _params(semantics):
    return pltpu.CompilerParams(dimension_semantics=semantics, vmem_limit_bytes=VMEM_LIMIT_BYTES)


def _rms_scale(x, axis=-1):
    return lax.rsqrt(jnp.mean(x * x, axis=axis, keepdims=True) + NORM_EPS)


def _mod_kernel(c_ref, w_ref, b_ref, o_ref):
    c = c_ref[...]
    act = c * jax.nn.sigmoid(c)
    o_ref[...] = jnp.dot(act.astype(BF16), w_ref[...].astype(BF16),
                         preferred_element_type=F32) + b_ref[...]


def _modulation(c, w, b):
    bsz, d = c.shape
    n = w.shape[1]
    tn = MOD_COL_TILE
    return pl.pallas_call(
        _mod_kernel,
        grid=(n // tn,),
        in_specs=[pl.BlockSpec((bsz, d), lambda j: (0, 0)),
                  pl.BlockSpec((d, tn), lambda j: (0, j)),
                  pl.BlockSpec((1, tn), lambda j: (0, j))],
        out_specs=pl.BlockSpec((bsz, tn), lambda j: (0, j)),
        out_shape=jax.ShapeDtypeStruct((bsz, n), F32),
        compiler_params=_params(("arbitrary",)),
        name="modulation",
    )(c, w, b.reshape(1, n))


def _proj_kernel(x_ref, mod_ref, g_ref, wt_ref, wk_ref, fm_ref, dk_ref, fk_ref, fl_ref,
                 *, width, n_fox_heads):
    x = x_ref[0]
    h = (x * _rms_scale(x) * g_ref[...]) * (1.0 + mod_ref[0, 1:2, :]) + mod_ref[0, 0:1, :]
    hb = h.astype(BF16)
    fm_ref[0, 0] = lax.dot_general(wt_ref[...], hb, NT_DIMS,
                                   preferred_element_type=F32).astype(BF16)
    r = jnp.dot(hb, wk_ref[...], preferred_element_type=F32)
    dk_ref[0] = r[:, :width].astype(BF16)
    fk_ref[0] = r[:, width:2 * width].astype(BF16)
    fl_ref[0] = r[:, 2 * width:2 * width + n_fox_heads]


def _in_projection(x, mod3, g, w_fm_t, w_tm, *, width, n_fox_heads):
    bsz, s, d = x.shape
    ts = ATTN_TILE
    k_shape = jax.ShapeDtypeStruct((bsz, s, width), BF16)
    k_spec = pl.BlockSpec((1, ts, width), lambda b, i: (b, i, 0))
    kern = functools.partial(_proj_kernel, width=width, n_fox_heads=n_fox_heads)
    return pl.pallas_call(
        kern,
        grid=(bsz, s // ts),
        in_specs=[pl.BlockSpec((1, ts, d), lambda b, i: (b, i, 0)),
                  pl.BlockSpec((1, N_MOD, d), lambda b, i: (b, 0, 0)),
                  pl.BlockSpec((1, d), lambda b, i: (0, 0)),
                  pl.BlockSpec(w_fm_t.shape, lambda b, i: (0, 0)),
                  pl.BlockSpec(w_tm.shape, lambda b, i: (0, 0))],
        out_specs=[pl.BlockSpec((1, 1, 4 * width, ts), lambda b, i: (b, i, 0, 0)),
                   k_spec, k_spec,
                   pl.BlockSpec((1, ts, n_fox_heads), lambda b, i: (b, i, 0))],
        out_shape=[jax.ShapeDtypeStruct((bsz, s // ts, 4 * width, ts), BF16),
                   k_shape, k_shape,
                   jax.ShapeDtypeStruct((bsz, s, n_fox_heads), F32)],
        compiler_params=_params(("arbitrary", "arbitrary")),
        name="in_projection",
    )(x, mod3, g.reshape(1, d), w_fm_t, w_tm)


def _split3(x):
    hi = x.astype(BF16)
    r1 = x - hi.astype(F32)
    mid = r1.astype(BF16)
    lo = (r1 - mid.astype(F32)).astype(BF16)
    return hi, mid, lo


def _cum_kernel(fl_ref, fb_ref, qa_ref, ka_ref):
    z = fl_ref[0] + fb_ref[...]
    acc = jnp.minimum(z, 0.0) - jnp.log1p(jnp.exp(-jnp.abs(z)))
    n_heads, s = acc.shape
    pos = lax.broadcasted_iota(jnp.int32, acc.shape, 1)
    shift = 1
    while shift < s:
        acc = acc + jnp.where(pos >= shift, pltpu.roll(acc, shift, 1), 0.0)
        shift *= 2
    q_parts = _split3(acc)
    k_parts = _split3(-acc)
    ones = jnp.ones((3, s), BF16)
    zeros = jnp.zeros((BF16_ROWS - 6, s), BF16)
    for h in range(n_heads):
        qa_ref[0, h] = jnp.concatenate([p[h:h + 1] for p in q_parts] + [ones, zeros], axis=0)
        ka_ref[0, h] = jnp.concatenate([ones] + [p[h:h + 1] for p in k_parts] + [zeros], axis=0)


def _forget_aug(fl_t, forget_b):
    bsz, h, s = fl_t.shape
    aug_shape = jax.ShapeDtypeStruct((bsz, h, BF16_ROWS, s), BF16)
    aug_spec = pl.BlockSpec((1, h, BF16_ROWS, s), lambda b: (b, 0, 0, 0))
    return pl.pallas_call(
        _cum_kernel,
        grid=(bsz,),
        in_specs=[pl.BlockSpec((1, h, s), lambda b: (b, 0, 0)),
                  pl.BlockSpec((h, 1), lambda b: (0, 0))],
        out_specs=[aug_spec, aug_spec],
        out_shape=[aug_shape, aug_shape],
        compiler_params=_params(("arbitrary",)),
        name="forget_cumsum",
    )(fl_t, forget_b.reshape(h, 1))


def _bias_kernel(rb_ref, o_ref, *, tile, n_delta):
    h = pl.program_id(0)
    u = lax.broadcasted_iota(jnp.int32, (1, 2 * tile), 1)
    for dd in range(n_delta):
        n = jnp.maximum(dd * tile + u - tile, 0)
        nf = jnp.maximum(n, 1).astype(F32)
        large = MAX_EXACT + (jnp.log(nf / MAX_EXACT) / math.log(MAX_DISTANCE / MAX_EXACT)
                             * (NUM_BUCKETS - MAX_EXACT)).astype(jnp.int32)
        large = jnp.minimum(large, NUM_BUCKETS - 1)
        bucket = jnp.where(n < MAX_EXACT, n, large)
        row = jnp.zeros((1, 2 * tile), F32)
        for b in range(NUM_BUCKETS):
            row = jnp.where(bucket == b, rb_ref[h, b], row)
        full = pltpu.roll(jnp.broadcast_to(row, (tile, 2 * tile)), 0, 1, stride=1, stride_axis=0)
        o_ref[0, dd] = full[:, tile:]


def _bias_tiles(rel_bias_t, seq, tile):
    n_heads = rel_bias_t.shape[0]
    n_delta = seq // tile
    kern = functools.partial(_bias_kernel, tile=tile, n_delta=n_delta)
    return pl.pallas_call(
        kern,
        grid=(n_heads,),
        in_specs=[pl.BlockSpec(memory_space=pltpu.SMEM)],
        out_specs=pl.BlockSpec((1, n_delta, tile, tile), lambda h: (h, 0, 0, 0)),
        out_shape=jax.ShapeDtypeStruct((n_heads, n_delta, tile, tile), F32),
        compiler_params=_params(("arbitrary",)),
        name="bias_tiles",
    )(rel_bias_t)


def _flash_pair(w_maps, keys_fn, values_fn, bias_fn, scratch, tile):
    m_scr, l_scr, acc_scr, s_scr, p_scr, mt_scr, al_scr = scratch
    i = pl.program_id(2)
    m_scr[...] = jnp.full(m_scr.shape, NEG_INF, F32)
    l_scr[...] = jnp.zeros(l_scr.shape, F32)
    acc_scr[...] = jnp.zeros(acc_scr.shape, F32)

    def scores(j):
        keys = keys_fn(j)
        bias = bias_fn(j)
        for idx in range(2):
            st = jnp.dot(keys, w_maps[idx], preferred_element_type=F32)
            if bias is not None:
                st = st + bias
            s_scr[idx] = st
            mt_scr[idx] = jnp.max(st, axis=0, keepdims=True)

    def softmax(diag):
        if diag:
            key_pos = lax.broadcasted_iota(jnp.int32, (tile, tile), 0)
            qry_pos = lax.broadcasted_iota(jnp.int32, (tile, tile), 1)
            causal = key_pos <= qry_pos
        for idx in range(2):
            st = s_scr[idx]
            if diag:
                st = jnp.where(causal, st, NEG_INF)
                m_tile = jnp.max(st, axis=0, keepdims=True)
            else:
                m_tile = mt_scr[idx]
            m_prev = m_scr[idx]
            m_new = jnp.maximum(m_prev, m_tile)
            alpha = jnp.exp(m_prev - m_new)
            p = jnp.exp(st - m_new)
            l_scr[idx] = alpha * l_scr[idx] + jnp.sum(p, axis=0, keepdims=True)
            p_scr[idx] = p.astype(BF16)
            al_scr[idx] = alpha
            m_scr[idx] = m_new

    def accumulate(j):
        for idx in range(2):
            acc_scr[idx] = al_scr[idx] * acc_scr[idx] + jnp.dot(
                values_fn(j, idx), p_scr[idx], preferred_element_type=F32)

    def body(j, carry):
        softmax(False)
        scores(j + 1)
        accumulate(j)
        return carry

    scores(0)
    lax.fori_loop(0, i, body, 0)
    softmax(True)
    accumulate(i)


def _masked_maps(qt):
    feat = lax.broadcasted_iota(jnp.int32, qt.shape, 0)
    zero = jnp.zeros_like(qt)
    return jnp.where(feat < DIFF_QK_DIM, qt, zero), jnp.where(feat >= DIFF_QK_DIM, qt, zero)


def _diff_attn_kernel(qt_ref, k_ref, vt_ref, bias_ref, lam_ref, g_ref, o_ref, *scratch,
                      lambda_init):
    i = pl.program_id(2)
    tile = qt_ref.shape[3]
    w_maps = _masked_maps(qt_ref[0, 0])
    l_scr, acc_scr = scratch[1], scratch[2]

    def keys_fn(j):
        return k_ref[0, pl.ds(pl.multiple_of(j * tile, tile), tile), :]

    def values_fn(j, idx):
        return vt_ref[0, j]

    def bias_fn(j):
        return bias_ref[0, i - j]

    _flash_pair(w_maps, keys_fn, values_fn, bias_fn, scratch, tile)
    lam = (jnp.exp(jnp.sum(lam_ref[0:1, :] * lam_ref[1:2, :], axis=-1, keepdims=True))
           - jnp.exp(jnp.sum(lam_ref[2:3, :] * lam_ref[3:4, :], axis=-1, keepdims=True))
           + lambda_init)
    o = acc_scr[0] / l_scr[0] - lam * (acc_scr[1] / l_scr[1])
    o = (o * _rms_scale(o, axis=0) * g_ref[...]) * (1.0 - lambda_init)
    o_ref[0] = o.T.astype(o_ref.dtype)


def _fox_attn_kernel(qt_ref, qa_ref, k_ref, ka_ref, vt_ref, o_ref, *scratch):
    tile = qt_ref.shape[3]
    l_scr, acc_scr = scratch[1], scratch[2]
    q_a, q_b = _masked_maps(qt_ref[0, 0])
    aug_zero = jnp.zeros((BF16_ROWS, tile), BF16)
    pad = jnp.zeros((MXU_DEPTH - LANES - 2 * BF16_ROWS, tile), BF16)
    w_maps = (jnp.concatenate([q_a, qa_ref[0, 0], aug_zero, pad], axis=0),
              jnp.concatenate([q_b, aug_zero, qa_ref[0, 1], pad], axis=0))

    def keys_fn(j):
        rows = pl.ds(pl.multiple_of(j * tile, tile), tile)
        return jnp.concatenate([k_ref[0, rows, :], ka_ref[0, rows, :]], axis=1)

    def values_fn(j, idx):
        return vt_ref[0, j, idx * FOX_HEAD_DIM:(idx + 1) * FOX_HEAD_DIM, :]

    _flash_pair(w_maps, keys_fn, values_fn, lambda j: None, scratch, tile)
    o = jnp.concatenate([acc_scr[0] / l_scr[0], acc_scr[1] / l_scr[1]], axis=0)
    o_ref[0] = o.T.astype(o_ref.dtype)


def _attn_scratch(tile, acc_rows):
    stat = pltpu.VMEM((2, 1, tile), F32)
    return [stat, stat, pltpu.VMEM((2, acc_rows, tile), F32), pltpu.VMEM((2, tile, tile), F32),
            pltpu.VMEM((2, tile, tile), BF16), stat, stat]


def _diff_attention(fm, dk, bias_tiles, lam_vecs, subln_g, lambda_init):
    bsz, n_tiles, _, tile = fm.shape
    s = n_tiles * tile
    width = dk.shape[2]
    n_groups = width // LANES
    kern = functools.partial(_diff_attn_kernel, lambda_init=lambda_init)
    return pl.pallas_call(
        kern,
        grid=(n_groups, bsz, n_tiles),
        in_specs=[pl.BlockSpec((1, 1, LANES, tile), lambda g, b, i: (b, i, g, 0)),
                  pl.BlockSpec((1, s, LANES), lambda g, b, i: (b, 0, g)),
                  pl.BlockSpec((1, n_tiles, LANES, tile), lambda g, b, i: (b, 0, n_groups + g, 0)),
                  pl.BlockSpec((1, n_tiles, tile, tile), lambda g, b, i: (g, 0, 0, 0)),
                  pl.BlockSpec(lam_vecs.shape, lambda g, b, i: (0, 0)),
                  pl.BlockSpec((LANES, 1), lambda g, b, i: (0, 0))],
        out_specs=pl.BlockSpec((1, tile, LANES), lambda g, b, i: (b, i, g)),
        out_shape=jax.ShapeDtypeStruct((bsz, s, width), BF16),
        scratch_shapes=_attn_scratch(tile, DIFF_V_DIM),
        compiler_params=_params(("arbitrary", "arbitrary", "arbitrary")),
        name="diff_attention",
    )(fm, dk, fm, bias_tiles, lam_vecs, subln_g.reshape(LANES, 1))


def _fox_attention(fm, fk, q_aug, k_aug):
    bsz, n_tiles, _, tile = fm.shape
    s = n_tiles * tile
    width = fk.shape[2]
    n_groups = width // LANES
    return pl.pallas_call(
        _fox_attn_kernel,
        grid=(n_groups, bsz, n_tiles),
        in_specs=[pl.BlockSpec((1, 1, LANES, tile), lambda g, b, i: (b, i, 2 * n_groups + g, 0)),
                  pl.BlockSpec((1, 2, BF16_ROWS, tile), lambda g, b, i: (b, g, 0, i)),
                  pl.BlockSpec((1, s, LANES), lambda g, b, i: (b, 0, g)),
                  pl.BlockSpec((1, s, LANES), lambda g, b, i: (b * n_groups + g, 0, 0)),
                  pl.BlockSpec((1, n_tiles, LANES, tile),
                               lambda g, b, i: (b, 0, 3 * n_groups + g, 0))],
        out_specs=pl.BlockSpec((1, tile, LANES), lambda g, b, i: (b, i, g)),
        out_shape=jax.ShapeDtypeStruct((bsz, s, width), BF16),
        scratch_shapes=_attn_scratch(tile, FOX_HEAD_DIM),
        compiler_params=_params(("arbitrary", "arbitrary", "arbitrary")),
        name="fox_attention",
    )(fm, q_aug, fk, k_aug, fm)


def _out_kernel(d_ref, f_ref, w_ref, x_ref, mod_ref, g_ref, x1_ref, h_ref, *, width):
    mix = (jnp.dot(d_ref[0], w_ref[:width, :], preferred_element_type=F32)
           + jnp.dot(f_ref[0], w_ref[width:, :], preferred_element_type=F32))
    x1 = x_ref[0] + mod_ref[0, 2:3, :] * mix
    x1_ref[0] = x1
    h = (x1 * _rms_scale(x1) * g_ref[...]) * (1.0 + mod_ref[0, 4:5, :]) + mod_ref[0, 3:4, :]
    h_ref[0] = h.astype(BF16)


def _out_projection(d_out, f_out, w_out, x, mod3, g):
    bsz, s, d = x.shape
    width = d_out.shape[2]
    ts = ROW_TILE
    row_spec = pl.BlockSpec((1, ts, d), lambda b, i: (b, i, 0))
    in_spec = pl.BlockSpec((1, ts, width), lambda b, i: (b, i, 0))
    kern = functools.partial(_out_kernel, width=width)
    return pl.pallas_call(
        kern,
        grid=(bsz, s // ts),
        in_specs=[in_spec, in_spec,
                  pl.BlockSpec(w_out.shape, lambda b, i: (0, 0)),
                  row_spec,
                  pl.BlockSpec((1, N_MOD, d), lambda b, i: (b, 0, 0)),
                  pl.BlockSpec((1, d), lambda b, i: (0, 0))],
        out_specs=[row_spec, row_spec],
        out_shape=[jax.ShapeDtypeStruct((bsz, s, d), F32), jax.ShapeDtypeStruct((bsz, s, d), BF16)],
        compiler_params=_params(("arbitrary", "arbitrary")),
        name="out_projection",
    )(d_out, f_out, w_out, x, mod3, g.reshape(1, d))


def _ffn_kernel(h_ref, x1_ref, mod_ref, wu_ref, cw_ref, cb_ref, wd_ref, fg_ref, o_ref,
                tail_scr, *, d_ff, chunk, final_norm):
    i = pl.program_id(1)
    ts = h_ref.shape[1]

    @pl.when(i == 0)
    def _():
        tail_scr[...] = jnp.zeros(tail_scr.shape, F32)

    hb = h_ref[0]

    def conv_cols(c0):
        u = jnp.dot(hb, wu_ref[:, c0:c0 + chunk], preferred_element_type=F32)
        ext = jnp.concatenate([tail_scr[:, c0:c0 + chunk], u], axis=0)
        tail_scr[:, c0:c0 + chunk] = u[ts - SUBLANES:, :]
        y = cb_ref[:, c0:c0 + chunk]
        for tap in range(CONV_WIDTH):
            lo = SUBLANES - (CONV_WIDTH - 1) + tap
            y = y + cw_ref[tap:tap + 1, c0:c0 + chunk] * ext[lo:lo + ts, :]
        return y

    acc = jnp.zeros((ts, o_ref.shape[2]), F32)
    for c in range(d_ff // chunk):
        gate = conv_cols(c * chunk)
        val = conv_cols(d_ff + c * chunk)
        act = (gate * jax.nn.sigmoid(gate) * val).astype(BF16)
        acc = acc + jnp.dot(act, wd_ref[c * chunk:(c + 1) * chunk, :], preferred_element_type=F32)
    x2 = x1_ref[0] + mod_ref[0, 5:6, :] * acc
    if final_norm:
        x2 = x2 * _rms_scale(x2) * fg_ref[...]
    o_ref[0] = x2


def _ffn(h2, x1, mod3, w_up, conv_w, conv_b, w_down, final_g, final_norm):
    bsz, s, d = x1.shape
    d_ff = w_down.shape[0]
    ts = ROW_TILE
    row_spec = pl.BlockSpec((1, ts, d), lambda b, i: (b, i, 0))
    kern = functools.partial(_ffn_kernel, d_ff=d_ff, chunk=FFN_CHUNK, final_norm=final_norm)
    const = lambda b, i: (0, 0)
    return pl.pallas_call(
        kern,
        grid=(bsz, s // ts),
        in_specs=[row_spec, row_spec,
                  pl.BlockSpec((1, N_MOD, d), lambda b, i: (b, 0, 0)),
                  pl.BlockSpec(w_up.shape, const, pipeline_mode=pl.Buffered(1)),
                  pl.BlockSpec(conv_w.shape, const),
                  pl.BlockSpec((1, 2 * d_ff), const),
                  pl.BlockSpec(w_down.shape, const, pipeline_mode=pl.Buffered(1)),
                  pl.BlockSpec((1, d), const)],
        out_specs=row_spec,
        out_shape=jax.ShapeDtypeStruct((bsz, s, d), F32),
        scratch_shapes=[pltpu.VMEM((SUBLANES, 2 * d_ff), F32)],
        compiler_params=_params(("arbitrary", "arbitrary")),
        name="conv_ffn",
    )(h2, x1, mod3, w_up, conv_w, conv_b.reshape(1, 2 * d_ff), w_down, final_g.reshape(1, d))


def kernel(x, c, ada_w, ada_b, attn_norm_g, w_in, forget_b, lambda_q1, lambda_k1, lambda_q2,
           lambda_k2, subln_g, rel_bias, w_out, ffn_norm_g, w_up, conv_w, conv_b, w_down,
           final_norm_g):
    bsz, s, d = x.shape
    depth = ada_w.shape[0]
    n_fox_heads = forget_b.shape[1]
    width = (w_in.shape[2] - n_fox_heads) // 6
    assert width % LANES == 0 and subln_g.shape[1] == DIFF_V_DIM == LANES
    assert n_fox_heads * FOX_HEAD_DIM == width and s % ATTN_TILE == 0 and s % ROW_TILE == 0
    n_pairs = n_fox_heads // 2
    qk_scale = DIFF_QK_DIM ** -0.5

    bias = _bias_tiles(rel_bias.T, s, ATTN_TILE)

    for l in range(depth):
        lambda_init = 0.8 - 0.6 * math.exp(-0.3 * l)
        mod3 = _modulation(c, ada_w[l], ada_b[l]).reshape(bsz, N_MOD, d)

        w = w_in[l]
        cols = lambda n: w[:, n * width:(n + 1) * width]
        w_fm_t = jnp.concatenate([cols(0) * qk_scale, cols(2), cols(3) * qk_scale, cols(5)],
                                 axis=1).T.astype(BF16)
        w_tm = jnp.concatenate([cols(1), cols(4),
                                jnp.pad(w[:, 6 * width:], ((0, 0), (0, LANES - n_fox_heads)))],
                               axis=1).astype(BF16)
        fm, dk, fk, fl = _in_projection(x, mod3, attn_norm_g[l], w_fm_t, w_tm,
                                        width=width, n_fox_heads=n_fox_heads)

        q_aug, k_aug_t = _forget_aug(jnp.transpose(fl, (0, 2, 1)), forget_b[l])
        k_aug = jnp.transpose(k_aug_t.reshape(bsz * n_pairs, 2 * BF16_ROWS, s), (0, 2, 1))
        k_aug = jnp.pad(k_aug, ((0, 0), (0, 0), (0, LANES - 2 * BF16_ROWS)))

        lam_vecs = jnp.stack([lambda_q1[l], lambda_k1[l], lambda_q2[l], lambda_k2[l]]).astype(F32)
        d_out = _diff_attention(fm, dk, bias, lam_vecs, subln_g[l], lambda_init)
        f_out = _fox_attention(fm, fk, q_aug, k_aug)

        x1, h2 = _out_projection(d_out, f_out, w_out[l].astype(BF16), x, mod3, ffn_norm_g[l])
        x = _ffn(h2, x1, mod3, w_up[l].astype(BF16), conv_w[l], conv_b[l],
                 w_down[l].astype(BF16), final_norm_g, final_norm=(l == depth - 1))
    return x
```

```python
import functools
import math

import jax
import jax.numpy as jnp
from jax import lax
from jax.experimental import pallas as pl
from jax.experimental.pallas import tpu as pltpu

F32 = jnp.float32
BF16 = jnp.bfloat16

DIFF_QK_DIM = 64
DIFF_V_DIM = 2 * DIFF_QK_DIM
FOX_HEAD_DIM = 64
CONV_WIDTH = 3
NUM_BUCKETS = 32
MAX_EXACT = NUM_BUCKETS // 2
MAX_DISTANCE = 128
N_MOD = 6
NORM_EPS = 1e-6
NEG_INF = -1e30
LOG2E = math.log2(math.e)

LANES = 128
SUBLANES = 8
BF16_ROWS = 16
MXU_DEPTH = 256
VMEM_LIMIT_BYTES = 56 * 1024 * 1024

ATTN_TILE = 512
ROW_TILE = 512
FFN_CHUNK = 256
MOD_COL_TILE = 1536

NT_DIMS = (((1,), (1,)), ((), ()))


def _params(semantics):
    return pltpu.CompilerParams(dimension_semantics=semantics, vmem_limit_bytes=VMEM_LIMIT_BYTES)


def _rms_scale(x, axis=-1):
    return lax.rsqrt(jnp.mean(x * x, axis=axis, keepdims=True) + NORM_EPS)


def _mod_kernel(c_ref, w_ref, b_ref, o_ref):
    c = c_ref[...]
    act = c * jax.nn.sigmoid(c)
    o_ref[...] = jnp.dot(act.astype(BF16), w_ref[...].astype(BF16),
                         preferred_element_type=F32) + b_ref[...]


def _modulation(c, w, b):
    bsz, d = c.shape
    n = w.shape[1]
    tn = MOD_COL_TILE
    return pl.pallas_call(
        _mod_kernel,
        grid=(n // tn,),
        in_specs=[pl.BlockSpec((bsz, d), lambda j: (0, 0)),
                  pl.BlockSpec((d, tn), lambda j: (0, j)),
                  pl.BlockSpec((1, tn), lambda j: (0, j))],
        out_specs=pl.BlockSpec((bsz, tn), lambda j: (0, j)),
        out_shape=jax.ShapeDtypeStruct((bsz, n), F32),
        compiler_params=_params(("arbitrary",)),
        name="modulation",
    )(c, w, b.reshape(1, n))


def _proj_kernel(x_ref, mod_ref, g_ref, wt_ref, wk_ref, fm_ref, dk_ref, fk_ref, fl_ref,
                 *, width, n_fox_heads):
    x = x_ref[0]
    h = (x * _rms_scale(x) * g_ref[...]) * (1.0 + mod_ref[0, 1:2, :]) + mod_ref[0, 0:1, :]
    hb = h.astype(BF16)
    fm_ref[0, 0] = lax.dot_general(wt_ref[...], hb, NT_DIMS,
                                   preferred_element_type=F32).astype(BF16)
    r = jnp.dot(hb, wk_ref[...], preferred_element_type=F32)
    dk_ref[0] = r[:, :width].astype(BF16)
    fk_ref[0] = r[:, width:2 * width].astype(BF16)
    fl_ref[0] = r[:, 2 * width:2 * width + n_fox_heads]


def _in_projection(x, mod3, g, w_fm_t, w_tm, *, width, n_fox_heads):
    bsz, s, d = x.shape
    ts = ATTN_TILE
    k_shape = jax.ShapeDtypeStruct((bsz, s, width), BF16)
    k_spec = pl.BlockSpec((1, ts, width), lambda b, i: (b, i, 0))
    kern = functools.partial(_proj_kernel, width=width, n_fox_heads=n_fox_heads)
    return pl.pallas_call(
        kern,
        grid=(bsz, s // ts),
        in_specs=[pl.BlockSpec((1, ts, d), lambda b, i: (b, i, 0)),
                  pl.BlockSpec((1, N_MOD, d), lambda b, i: (b, 0, 0)),
                  pl.BlockSpec((1, d), lambda b, i: (0, 0)),
                  pl.BlockSpec(w_fm_t.shape, lambda b, i: (0, 0)),
                  pl.BlockSpec(w_tm.shape, lambda b, i: (0, 0))],
        out_specs=[pl.BlockSpec((1, 1, 4 * width, ts), lambda b, i: (b, i, 0, 0)),
                   k_spec, k_spec,
                   pl.BlockSpec((1, ts, n_fox_heads), lambda b, i: (b, i, 0))],
        out_shape=[jax.ShapeDtypeStruct((bsz, s // ts, 4 * width, ts), BF16),
                   k_shape, k_shape,
                   jax.ShapeDtypeStruct((bsz, s, n_fox_heads), F32)],
        compiler_params=_params(("arbitrary", "arbitrary")),
        name="in_projection",
    )(x, mod3, g.reshape(1, d), w_fm_t, w_tm)


def _split3(x):
    hi = x.astype(BF16)
    r1 = x - hi.astype(F32)
    mid = r1.astype(BF16)
    lo = (r1 - mid.astype(F32)).astype(BF16)
    return hi, mid, lo


def _cum_kernel(fl_ref, fb_ref, qa_ref, ka_ref):
    z = fl_ref[0] + fb_ref[...]
    acc = jnp.minimum(z, 0.0) - jnp.log1p(jnp.exp(-jnp.abs(z)))
    n_heads, s = acc.shape
    pos = lax.broadcasted_iota(jnp.int32, acc.shape, 1)
    shift = 1
    while shift < s:
        acc = acc + jnp.where(pos >= shift, pltpu.roll(acc, shift, 1), 0.0)
        shift *= 2
    acc = acc * LOG2E
    q_parts = _split3(acc)
    k_parts = _split3(-acc)
    ones = jnp.ones((3, s), BF16)
    zeros = jnp.zeros((BF16_ROWS - 6, s), BF16)
    for h in range(n_heads):
        qa_ref[0, h] = jnp.concatenate([p[h:h + 1] for p in q_parts] + [ones, zeros], axis=0)
        ka_ref[0, h] = jnp.concatenate([ones] + [p[h:h + 1] for p in k_parts] + [zeros], axis=0)


def _forget_aug(fl_t, forget_b):
    bsz, h, s = fl_t.shape
    aug_shape = jax.ShapeDtypeStruct((bsz, h, BF16_ROWS, s), BF16)
    aug_spec = pl.BlockSpec((1, h, BF16_ROWS, s), lambda b: (b, 0, 0, 0))
    return pl.pallas_call(
        _cum_kernel,
        grid=(bsz,),
        in_specs=[pl.BlockSpec((1, h, s), lambda b: (b, 0, 0)),
                  pl.BlockSpec((h, 1), lambda b: (0, 0))],
        out_specs=[aug_spec, aug_spec],
        out_shape=[aug_shape, aug_shape],
        compiler_params=_params(("arbitrary",)),
        name="forget_cumsum",
    )(fl_t, forget_b.reshape(h, 1))


def _bias_kernel(rb_ref, o_ref, *, tile, n_delta):
    h = pl.program_id(0)
    u = lax.broadcasted_iota(jnp.int32, (1, 2 * tile), 1)
    for dd in range(n_delta):
        n = jnp.maximum(dd * tile + u - tile, 0)
        nf = jnp.maximum(n, 1).astype(F32)
        large = MAX_EXACT + (jnp.log(nf / MAX_EXACT) / math.log(MAX_DISTANCE / MAX_EXACT)
                             * (NUM_BUCKETS - MAX_EXACT)).astype(jnp.int32)
        large = jnp.minimum(large, NUM_BUCKETS - 1)
        bucket = jnp.where(n < MAX_EXACT, n, large)
        row = jnp.zeros((1, 2 * tile), F32)
        for b in range(NUM_BUCKETS):
            row = jnp.where(bucket == b, rb_ref[h, b], row)
        row = row * LOG2E
        full = pltpu.roll(jnp.broadcast_to(row, (tile, 2 * tile)), 0, 1, stride=1, stride_axis=0)
        o_ref[0, dd] = full[:, tile:]


def _bias_tiles(rel_bias_t, seq, tile):
    n_heads = rel_bias_t.shape[0]
    n_delta = seq // tile
    kern = functools.partial(_bias_kernel, tile=tile, n_delta=n_delta)
    return pl.pallas_call(
        kern,
        grid=(n_heads,),
        in_specs=[pl.BlockSpec(memory_space=pltpu.SMEM)],
        out_specs=pl.BlockSpec((1, n_delta, tile, tile), lambda h: (h, 0, 0, 0)),
        out_shape=jax.ShapeDtypeStruct((n_heads, n_delta, tile, tile), F32),
        compiler_params=_params(("arbitrary",)),
        name="bias_tiles",
    )(rel_bias_t)


def _flash_pair(w_maps, keys_fn, values_fn, bias_fn, scratch, tile):
    m_scr, acc_scr, s_scr, p_scr, mt_scr, al_scr = scratch
    i = pl.program_id(2)
    m_scr[...] = jnp.full(m_scr.shape, NEG_INF, F32)
    acc_scr[...] = jnp.zeros(acc_scr.shape, F32)

    def scores(j):
        keys = keys_fn(j)
        bias = bias_fn(j)
        for idx in range(2):
            st = jnp.dot(keys, w_maps[idx], preferred_element_type=F32)
            if bias is not None:
                st = st + bias
            s_scr[idx] = st
            mt_scr[idx] = jnp.max(st, axis=0, keepdims=True)

    def softmax(diag):
        if diag:
            key_pos = lax.broadcasted_iota(jnp.int32, (tile, tile), 0)
            qry_pos = lax.broadcasted_iota(jnp.int32, (tile, tile), 1)
            causal = key_pos <= qry_pos
        for idx in range(2):
            st = s_scr[idx]
            if diag:
                st = jnp.where(causal, st, NEG_INF)
                m_tile = jnp.max(st, axis=0, keepdims=True)
            else:
                m_tile = mt_scr[idx]
            m_prev = m_scr[idx]
            m_new = jnp.maximum(m_prev, m_tile)
            p_scr[idx] = jnp.exp2(st - m_new).astype(BF16)
            al_scr[idx] = jnp.exp2(m_prev - m_new)
            m_scr[idx] = m_new

    def accumulate(j):
        for idx in range(2):
            acc_scr[idx] = al_scr[idx] * acc_scr[idx] + jnp.dot(
                values_fn(j, idx), p_scr[idx], preferred_element_type=F32)

    def body(j, carry):
        softmax(False)
        scores(j + 1)
        accumulate(j)
        return carry

    scores(0)
    lax.fori_loop(0, i, body, 0)
    softmax(True)
    accumulate(i)


def _masked_maps(qt):
    feat = lax.broadcasted_iota(jnp.int32, qt.shape, 0)
    zero = jnp.zeros_like(qt)
    return jnp.where(feat < DIFF_QK_DIM, qt, zero), jnp.where(feat >= DIFF_QK_DIM, qt, zero)


def _ones_rows(tile):
    row = lax.broadcasted_iota(jnp.int32, (BF16_ROWS, tile), 0)
    return jnp.where(row == 0, 1.0, 0.0).astype(BF16)


def _normalized(acc_ref, idx, rows):
    return acc_ref[idx, :rows, :] / acc_ref[idx, rows:rows + 1, :]


def _diff_attn_kernel(qt_ref, k_ref, vt_ref, bias_ref, lam_ref, g_ref, o_ref, *scratch,
                      lambda_init):
    i = pl.program_id(2)
    tile = qt_ref.shape[3]
    w_maps = _masked_maps(qt_ref[0, 0])
    acc_scr = scratch[1]
    ones = _ones_rows(tile)

    def keys_fn(j):
        return k_ref[0, pl.ds(pl.multiple_of(j * tile, tile), tile), :]

    def values_fn(j, idx):
        return jnp.concatenate([vt_ref[0, j], ones], axis=0)

    def bias_fn(j):
        return bias_ref[0, i - j]

    _flash_pair(w_maps, keys_fn, values_fn, bias_fn, scratch, tile)
    lam = (jnp.exp(jnp.sum(lam_ref[0:1, :] * lam_ref[1:2, :], axis=-1, keepdims=True))
           - jnp.exp(jnp.sum(lam_ref[2:3, :] * lam_ref[3:4, :], axis=-1, keepdims=True))
           + lambda_init)
    o = (_normalized(acc_scr, 0, DIFF_V_DIM)
         - lam * _normalized(acc_scr, 1, DIFF_V_DIM))
    o = (o * _rms_scale(o, axis=0) * g_ref[...]) * (1.0 - lambda_init)
    o_ref[0] = o.T.astype(o_ref.dtype)


def _fox_attn_kernel(qt_ref, qa_ref, k_ref, ka_ref, vt_ref, o_ref, *scratch):
    tile = qt_ref.shape[3]
    acc_scr = scratch[1]
    ones = _ones_rows(tile)
    q_a, q_b = _masked_maps(qt_ref[0, 0])
    aug_zero = jnp.zeros((BF16_ROWS, tile), BF16)
    pad = jnp.zeros((MXU_DEPTH - LANES - 2 * BF16_ROWS, tile), BF16)
    w_maps = (jnp.concatenate([q_a, qa_ref[0, 0], aug_zero, pad], axis=0),
              jnp.concatenate([q_b, aug_zero, qa_ref[0, 1], pad], axis=0))

    def keys_fn(j):
        rows = pl.ds(pl.multiple_of(j * tile, tile), tile)
        return jnp.concatenate([k_ref[0, rows, :], ka_ref[0, rows, :]], axis=1)

    def values_fn(j, idx):
        rows = vt_ref[0, j, idx * FOX_HEAD_DIM:(idx + 1) * FOX_HEAD_DIM, :]
        return jnp.concatenate([rows, ones], axis=0)

    _flash_pair(w_maps, keys_fn, values_fn, lambda j: None, scratch, tile)
    o = jnp.concatenate([_normalized(acc_scr, 0, FOX_HEAD_DIM),
                         _normalized(acc_scr, 1, FOX_HEAD_DIM)], axis=0)
    o_ref[0] = o.T.astype(o_ref.dtype)


def _attn_scratch(tile, value_rows):
    stat = pltpu.VMEM((2, 1, tile), F32)
    return [stat, pltpu.VMEM((2, value_rows + BF16_ROWS, tile), F32),
            pltpu.VMEM((2, tile, tile), F32), pltpu.VMEM((2, tile, tile), BF16), stat, stat]


def _diff_attention(fm, dk, bias_tiles, lam_vecs, subln_g, lambda_init):
    bsz, n_tiles, _, tile = fm.shape
    s = n_tiles * tile
    width = dk.shape[2]
    n_groups = width // LANES
    kern = functools.partial(_diff_attn_kernel, lambda_init=lambda_init)
    return pl.pallas_call(
        kern,
        grid=(n_groups, bsz, n_tiles),
        in_specs=[pl.BlockSpec((1, 1, LANES, tile), lambda g, b, i: (b, i, g, 0)),
                  pl.BlockSpec((1, s, LANES), lambda g, b, i: (b, 0, g)),
                  pl.BlockSpec((1, n_tiles, LANES, tile), lambda g, b, i: (b, 0, n_groups + g, 0)),
                  pl.BlockSpec((1, n_tiles, tile, tile), lambda g, b, i: (g, 0, 0, 0)),
                  pl.BlockSpec(lam_vecs.shape, lambda g, b, i: (0, 0)),
                  pl.BlockSpec((LANES, 1), lambda g, b, i: (0, 0))],
        out_specs=pl.BlockSpec((1, tile, LANES), lambda g, b, i: (b, i, g)),
        out_shape=jax.ShapeDtypeStruct((bsz, s, width), BF16),
        scratch_shapes=_attn_scratch(tile, DIFF_V_DIM),
        compiler_params=_params(("arbitrary", "arbitrary", "arbitrary")),
        name="diff_attention",
    )(fm, dk, fm, bias_tiles, lam_vecs, subln_g.reshape(LANES, 1))


def _fox_attention(fm, fk, q_aug, k_aug):
    bsz, n_tiles, _, tile = fm.shape
    s = n_tiles * tile
    width = fk.shape[2]
    n_groups = width // LANES
    return pl.pallas_call(
        _fox_attn_kernel,
        grid=(n_groups, bsz, n_tiles),
        in_specs=[pl.BlockSpec((1, 1, LANES, tile), lambda g, b, i: (b, i, 2 * n_groups + g, 0)),
                  pl.BlockSpec((1, 2, BF16_ROWS, tile), lambda g, b, i: (b, g, 0, i)),
                  pl.BlockSpec((1, s, LANES), lambda g, b, i: (b, 0, g)),
                  pl.BlockSpec((1, s, LANES), lambda g, b, i: (b * n_groups + g, 0, 0)),
                  pl.BlockSpec((1, n_tiles, LANES, tile),
                               lambda g, b, i: (b, 0, 3 * n_groups + g, 0))],
        out_specs=pl.BlockSpec((1, tile, LANES), lambda g, b, i: (b, i, g)),
        out_shape=jax.ShapeDtypeStruct((bsz, s, width), BF16),
        scratch_shapes=_attn_scratch(tile, FOX_HEAD_DIM),
        compiler_params=_params(("arbitrary", "arbitrary", "arbitrary")),
        name="fox_attention",
    )(fm, q_aug, fk, k_aug, fm)


def _out_kernel(d_ref, f_ref, w_ref, x_ref, mod_ref, g_ref, x1_ref, h_ref, *, width):
    mix = (jnp.dot(d_ref[0], w_ref[:width, :], preferred_element_type=F32)
           + jnp.dot(f_ref[0], w_ref[width:, :], preferred_element_type=F32))
    x1 = x_ref[0] + mod_ref[0, 2:3, :] * mix
    x1_ref[0] = x1
    h = (x1 * _rms_scale(x1) * g_ref[...]) * (1.0 + mod_ref[0, 4:5, :]) + mod_ref[0, 3:4, :]
    h_ref[0] = h.astype(BF16)


def _out_projection(d_out, f_out, w_out, x, mod3, g):
    bsz, s, d = x.shape
    width = d_out.shape[2]
    ts = ROW_TILE
    row_spec = pl.BlockSpec((1, ts, d), lambda b, i: (b, i, 0))
    in_spec = pl.BlockSpec((1, ts, width), lambda b, i: (b, i, 0))
    kern = functools.partial(_out_kernel, width=width)
    return pl.pallas_call(
        kern,
        grid=(bsz, s // ts),
        in_specs=[in_spec, in_spec,
                  pl.BlockSpec(w_out.shape, lambda b, i: (0, 0)),
                  row_spec,
                  pl.BlockSpec((1, N_MOD, d), lambda b, i: (b, 0, 0)),
                  pl.BlockSpec((1, d), lambda b, i: (0, 0))],
        out_specs=[row_spec, row_spec],
        out_shape=[jax.ShapeDtypeStruct((bsz, s, d), F32), jax.ShapeDtypeStruct((bsz, s, d), BF16)],
        compiler_params=_params(("arbitrary", "arbitrary")),
        name="out_projection",
    )(d_out, f_out, w_out, x, mod3, g.reshape(1, d))


def _ffn_kernel(h_ref, x1_ref, mod_ref, wu_ref, cw_ref, cb_ref, wd_ref, fg_ref, o_ref,
                tail_scr, *, d_ff, chunk, final_norm):
    i = pl.program_id(1)
    ts = h_ref.shape[1]

    @pl.when(i == 0)
    def _():
        tail_scr[...] = jnp.zeros(tail_scr.shape, F32)

    hb = h_ref[0]

    def conv_cols(c0):
        u = jnp.dot(hb, wu_ref[:, c0:c0 + chunk], preferred_element_type=F32)
        ext = jnp.concatenate([tail_scr[:, c0:c0 + chunk], u], axis=0)
        tail_scr[:, c0:c0 + chunk] = u[ts - SUBLANES:, :]
        y = cb_ref[:, c0:c0 + chunk]
        for tap in range(CONV_WIDTH):
            lo = SUBLANES - (CONV_WIDTH - 1) + tap
            y = y + cw_ref[tap:tap + 1, c0:c0 + chunk] * ext[lo:lo + ts, :]
        return y

    acc = jnp.zeros((ts, o_ref.shape[2]), F32)
    for c in range(d_ff // chunk):
        gate = conv_cols(c * chunk)
        val = conv_cols(d_ff + c * chunk)
        act = (gate * jax.nn.sigmoid(gate) * val).astype(BF16)
        acc = acc + jnp.dot(act, wd_ref[c * chunk:(c + 1) * chunk, :], preferred_element_type=F32)
    x2 = x1_ref[0] + mod_ref[0, 5:6, :] * acc
    if final_norm:
        x2 = x2 * _rms_scale(x2) * fg_ref[...]
    o_ref[0] = x2


def _ffn(h2, x1, mod3, w_up, conv_w, conv_b, w_down, final_g, final_norm):
    bsz, s, d = x1.shape
    d_ff = w_down.shape[0]
    ts = ROW_TILE
    row_spec = pl.BlockSpec((1, ts, d), lambda b, i: (b, i, 0))
    kern = functools.partial(_ffn_kernel, d_ff=d_ff, chunk=FFN_CHUNK, final_norm=final_norm)
    const = lambda b, i: (0, 0)
    return pl.pallas_call(
        kern,
        grid=(bsz, s // ts),
        in_specs=[row_spec, row_spec,
                  pl.BlockSpec((1, N_MOD, d), lambda b, i: (b, 0, 0)),
                  pl.BlockSpec(w_up.shape, const, pipeline_mode=pl.Buffered(1)),
                  pl.BlockSpec(conv_w.shape, const),
                  pl.BlockSpec((1, 2 * d_ff), const),
                  pl.BlockSpec(w_down.shape, const, pipeline_mode=pl.Buffered(1)),
                  pl.BlockSpec((1, d), const)],
        out_specs=row_spec,
        out_shape=jax.ShapeDtypeStruct((bsz, s, d), F32),
        scratch_shapes=[pltpu.VMEM((SUBLANES, 2 * d_ff), F32)],
        compiler_params=_params(("arbitrary", "arbitrary")),
        name="conv_ffn",
    )(h2, x1, mod3, w_up, conv_w, conv_b.reshape(1, 2 * d_ff), w_down, final_g.reshape(1, d))


def kernel(x, c, ada_w, ada_b, attn_norm_g, w_in, forget_b, lambda_q1, lambda_k1, lambda_q2,
           lambda_k2, subln_g, rel_bias, w_out, ffn_norm_g, w_up, conv_w, conv_b, w_down,
           final_norm_g):
    bsz, s, d = x.shape
    depth = ada_w.shape[0]
    n_fox_heads = forget_b.shape[1]
    width = (w_in.shape[2] - n_fox_heads) // 6
    assert width % LANES == 0 and subln_g.shape[1] == DIFF_V_DIM == LANES
    assert n_fox_heads * FOX_HEAD_DIM == width and s % ATTN_TILE == 0 and s % ROW_TILE == 0
    n_pairs = n_fox_heads // 2
    qk_scale = DIFF_QK_DIM ** -0.5 * LOG2E

    bias = _bias_tiles(rel_bias.T, s, ATTN_TILE)

    for l in range(depth):
        lambda_init = 0.8 - 0.6 * math.exp(-0.3 * l)
        mod3 = _modulation(c, ada_w[l], ada_b[l]).reshape(bsz, N_MOD, d)

        w = w_in[l]
        cols = lambda n: w[:, n * width:(n + 1) * width]
        w_fm_t = jnp.concatenate([cols(0) * qk_scale, cols(2), cols(3) * qk_scale, cols(5)],
                                 axis=1).T.astype(BF16)
        w_tm = jnp.concatenate([cols(1), cols(4),
                                jnp.pad(w[:, 6 * width:], ((0, 0), (0, LANES - n_fox_heads)))],
                               axis=1).astype(BF16)
        fm, dk, fk, fl = _in_projection(x, mod3, attn_norm_g[l], w_fm_t, w_tm,
                                        width=width, n_fox_heads=n_fox_heads)

        q_aug, k_aug_t = _forget_aug(jnp.transpose(fl, (0, 2, 1)), forget_b[l])
        k_aug = jnp.transpose(k_aug_t.reshape(bsz * n_pairs, 2 * BF16_ROWS, s), (0, 2, 1))
        k_aug = jnp.pad(k_aug, ((0, 0), (0, 0), (0, LANES - 2 * BF16_ROWS)))

        lam_vecs = jnp.stack([lambda_q1[l], lambda_k1[l], lambda_q2[l], lambda_k2[l]]).astype(F32)
        d_out = _diff_attention(fm, dk, bias, lam_vecs, subln_g[l], lambda_init)
        f_out = _fox_attention(fm, fk, q_aug, k_aug)

        x1, h2 = _out_projection(d_out, f_out, w_out[l].astype(BF16), x, mod3, ffn_norm_g[l])
        x = _ffn(h2, x1, mod3, w_up[l].astype(BF16), conv_w[l], conv_b[l],
                 w_down[l].astype(BF16), final_norm_g, final_norm=(l == depth - 1))
    return x
```

```python
import functools
import math

import jax
import jax.numpy as jnp
from jax import lax
from jax.experimental import pallas as pl
from jax.experimental.pallas import tpu as pltpu

F32 = jnp.float32
BF16 = jnp.bfloat16

DIFF_QK_DIM = 64
DIFF_V_DIM = 2 * DIFF_QK_DIM
FOX_HEAD_DIM = 64
CONV_WIDTH = 3
NUM_BUCKETS = 32
MAX_EXACT = NUM_BUCKETS // 2
MAX_DISTANCE = 128
N_MOD = 6
NORM_EPS = 1e-6
NEG_INF = -1e30
LOG2E = math.log2(math.e)

LANES = 128
SUBLANES = 8
BF16_ROWS = 16
MXU_DEPTH = 256
VMEM_LIMIT_BYTES = 56 * 1024 * 1024

ATTN_TILE = 512
ROW_TILE = 512
FFN_CHUNK = 256
MOD_COL_TILE = 1536

NT_DIMS = (((1,), (1,)), ((), ()))


def _params(semantics):
    return pltpu.CompilerParams(dimension_semantics=semantics, vmem_limit_bytes=VMEM_LIMIT_BYTES)


def _rms_scale(x, axis=-1):
    return lax.rsqrt(jnp.mean(x * x, axis=axis, keepdims=True) + NORM_EPS)


def _mod_kernel(c_ref, w_ref, b_ref, o_ref):
    c = c_ref[...]
    act = c * jax.nn.sigmoid(c)
    o_ref[...] = jnp.dot(act.astype(BF16), w_ref[...].astype(BF16),
                         preferred_element_type=F32) + b_ref[...]


def _modulation(c, w, b):
    bsz, d = c.shape
    n = w.shape[1]
    tn = MOD_COL_TILE
    return pl.pallas_call(
        _mod_kernel,
        grid=(n // tn,),
        in_specs=[pl.BlockSpec((bsz, d), lambda j: (0, 0)),
                  pl.BlockSpec((d, tn), lambda j: (0, j)),
                  pl.BlockSpec((1, tn), lambda j: (0, j))],
        out_specs=pl.BlockSpec((bsz, tn), lambda j: (0, j)),
        out_shape=jax.ShapeDtypeStruct((bsz, n), F32),
        compiler_params=_params(("arbitrary",)),
        name="modulation",
    )(c, w, b.reshape(1, n))


def _proj_kernel(x_ref, mod_ref, g_ref, wt_ref, wk_ref, fm_ref, dk_ref, fk_ref, fl_ref,
                 *, width, n_fox_heads):
    x = x_ref[0]
    h = (x * _rms_scale(x) * g_ref[...]) * (1.0 + mod_ref[0, 1:2, :]) + mod_ref[0, 0:1, :]
    hb = h.astype(BF16)
    fm_ref[0, 0] = lax.dot_general(wt_ref[...], hb, NT_DIMS,
                                   preferred_element_type=F32).astype(BF16)
    r = jnp.dot(hb, wk_ref[...], preferred_element_type=F32)
    dk_ref[0] = r[:, :width].astype(BF16)
    fk_ref[0] = r[:, width:2 * width].astype(BF16)
    fl_ref[0] = r[:, 2 * width:2 * width + n_fox_heads]


def _in_projection(x, mod3, g, w_fm_t, w_tm, *, width, n_fox_heads):
    bsz, s, d = x.shape
    ts = ATTN_TILE
    k_shape = jax.ShapeDtypeStruct((bsz, s, width), BF16)
    k_spec = pl.BlockSpec((1, ts, width), lambda b, i: (b, i, 0))
    kern = functools.partial(_proj_kernel, width=width, n_fox_heads=n_fox_heads)
    return pl.pallas_call(
        kern,
        grid=(bsz, s // ts),
        in_specs=[pl.BlockSpec((1, ts, d), lambda b, i: (b, i, 0)),
                  pl.BlockSpec((1, N_MOD, d), lambda b, i: (b, 0, 0)),
                  pl.BlockSpec((1, d), lambda b, i: (0, 0)),
                  pl.BlockSpec(w_fm_t.shape, lambda b, i: (0, 0)),
                  pl.BlockSpec(w_tm.shape, lambda b, i: (0, 0))],
        out_specs=[pl.BlockSpec((1, 1, 4 * width, ts), lambda b, i: (b, i, 0, 0)),
                   k_spec, k_spec,
                   pl.BlockSpec((1, ts, n_fox_heads), lambda b, i: (b, i, 0))],
        out_shape=[jax.ShapeDtypeStruct((bsz, s // ts, 4 * width, ts), BF16),
                   k_shape, k_shape,
                   jax.ShapeDtypeStruct((bsz, s, n_fox_heads), F32)],
        compiler_params=_params(("arbitrary", "arbitrary")),
        name="in_projection",
    )(x, mod3, g.reshape(1, d), w_fm_t, w_tm)


def _split3(x):
    hi = x.astype(BF16)
    r1 = x - hi.astype(F32)
    mid = r1.astype(BF16)
    lo = (r1 - mid.astype(F32)).astype(BF16)
    return hi, mid, lo


def _cum_kernel(fl_ref, fb_ref, qa_ref, ka_ref):
    z = fl_ref[0] + fb_ref[...]
    acc = jnp.minimum(z, 0.0) - jnp.log1p(jnp.exp(-jnp.abs(z)))
    n_heads, s = acc.shape
    pos = lax.broadcasted_iota(jnp.int32, acc.shape, 1)
    shift = 1
    while shift < s:
        acc = acc + jnp.where(pos >= shift, pltpu.roll(acc, shift, 1), 0.0)
        shift *= 2
    acc = acc * LOG2E
    q_parts = _split3(acc)
    k_parts = _split3(-acc)
    ones = jnp.ones((3, s), BF16)
    zeros = jnp.zeros((BF16_ROWS - 6, s), BF16)
    for h in range(n_heads):
        qa_ref[0, h] = jnp.concatenate([p[h:h + 1] for p in q_parts] + [ones, zeros], axis=0)
        ka_ref[0, h] = jnp.concatenate([ones] + [p[h:h + 1] for p in k_parts] + [zeros], axis=0)


def _forget_aug(fl_t, forget_b):
    bsz, h, s = fl_t.shape
    aug_shape = jax.ShapeDtypeStruct((bsz, h, BF16_ROWS, s), BF16)
    aug_spec = pl.BlockSpec((1, h, BF16_ROWS, s), lambda b: (b, 0, 0, 0))
    return pl.pallas_call(
        _cum_kernel,
        grid=(bsz,),
        in_specs=[pl.BlockSpec((1, h, s), lambda b: (b, 0, 0)),
                  pl.BlockSpec((h, 1), lambda b: (0, 0))],
        out_specs=[aug_spec, aug_spec],
        out_shape=[aug_shape, aug_shape],
        compiler_params=_params(("arbitrary",)),
        name="forget_cumsum",
    )(fl_t, forget_b.reshape(h, 1))


def _bias_kernel(rb_ref, o_ref, *, tile, n_delta):
    h = pl.program_id(0)
    u = lax.broadcasted_iota(jnp.int32, (1, 2 * tile), 1)
    for dd in range(n_delta):
        n = jnp.maximum(dd * tile + u - tile, 0)
        nf = jnp.maximum(n, 1).astype(F32)
        large = MAX_EXACT + (jnp.log(nf / MAX_EXACT) / math.log(MAX_DISTANCE / MAX_EXACT)
                             * (NUM_BUCKETS - MAX_EXACT)).astype(jnp.int32)
        large = jnp.minimum(large, NUM_BUCKETS - 1)
        bucket = jnp.where(n < MAX_EXACT, n, large)
        row = jnp.zeros((1, 2 * tile), F32)
        for b in range(NUM_BUCKETS):
            row = jnp.where(bucket == b, rb_ref[h, b], row)
        row = row * LOG2E
        full = pltpu.roll(jnp.broadcast_to(row, (tile, 2 * tile)), 0, 1, stride=1, stride_axis=0)
        o_ref[0, dd] = full[:, tile:]


def _bias_tiles(rel_bias_t, seq, tile):
    n_heads = rel_bias_t.shape[0]
    n_delta = seq // tile
    kern = functools.partial(_bias_kernel, tile=tile, n_delta=n_delta)
    return pl.pallas_call(
        kern,
        grid=(n_heads,),
        in_specs=[pl.BlockSpec(memory_space=pltpu.SMEM)],
        out_specs=pl.BlockSpec((1, n_delta, tile, tile), lambda h: (h, 0, 0, 0)),
        out_shape=jax.ShapeDtypeStruct((n_heads, n_delta, tile, tile), F32),
        compiler_params=_params(("arbitrary",)),
        name="bias_tiles",
    )(rel_bias_t)


def _flash_pair(i, w_maps, keys_fn, values_fn, bias_fn, scratch, tile):
    m_scr, acc_scr, s_scr, p_scr, mt_scr, al_scr = scratch
    m_scr[...] = jnp.full(m_scr.shape, NEG_INF, F32)
    acc_scr[...] = jnp.zeros(acc_scr.shape, F32)

    def scores(j):
        keys = keys_fn(j)
        bias = bias_fn(j)
        for idx in range(2):
            st = jnp.dot(keys, w_maps[idx], preferred_element_type=F32)
            if bias is not None:
                st = st + bias
            s_scr[idx] = st
            mt_scr[idx] = jnp.max(st, axis=0, keepdims=True)

    def softmax(diag):
        if diag:
            key_pos = lax.broadcasted_iota(jnp.int32, (tile, tile), 0)
            qry_pos = lax.broadcasted_iota(jnp.int32, (tile, tile), 1)
            causal = key_pos <= qry_pos
        for idx in range(2):
            st = s_scr[idx]
            if diag:
                st = jnp.where(causal, st, NEG_INF)
                m_tile = jnp.max(st, axis=0, keepdims=True)
            else:
                m_tile = mt_scr[idx]
            m_prev = m_scr[idx]
            m_new = jnp.maximum(m_prev, m_tile)
            p_scr[idx] = jnp.exp2(st - m_new).astype(BF16)
            al_scr[idx] = jnp.exp2(m_prev - m_new)
            m_scr[idx] = m_new

    def accumulate(j):
        for idx in range(2):
            acc_scr[idx] = al_scr[idx] * acc_scr[idx] + jnp.dot(
                values_fn(j, idx), p_scr[idx], preferred_element_type=F32)

    def body(j, carry):
        softmax(False)
        scores(j + 1)
        accumulate(j)
        return carry

    scores(0)
    if i > 0:
        lax.fori_loop(0, i, body, 0)
    softmax(True)
    accumulate(i)


def _masked_maps(qt):
    feat = lax.broadcasted_iota(jnp.int32, qt.shape, 0)
    zero = jnp.zeros_like(qt)
    return jnp.where(feat < DIFF_QK_DIM, qt, zero), jnp.where(feat >= DIFF_QK_DIM, qt, zero)


def _ones_rows(tile):
    row = lax.broadcasted_iota(jnp.int32, (BF16_ROWS, tile), 0)
    return jnp.where(row == 0, 1.0, 0.0).astype(BF16)


def _normalized(acc_ref, idx, rows):
    return acc_ref[idx, :rows, :] / acc_ref[idx, rows:rows + 1, :]


def _diff_attn_kernel(qt_ref, k_ref, vt_ref, bias_ref, lam_ref, g_ref, o_ref, *scratch,
                      lambda_init):
    n_tiles, _, tile = qt_ref.shape[1:]
    acc_scr = scratch[1]
    ones = _ones_rows(tile)
    lam = (jnp.exp(jnp.sum(lam_ref[0:1, :] * lam_ref[1:2, :], axis=-1, keepdims=True))
           - jnp.exp(jnp.sum(lam_ref[2:3, :] * lam_ref[3:4, :], axis=-1, keepdims=True))
           + lambda_init)

    def keys_fn(j):
        return k_ref[0, pl.ds(pl.multiple_of(j * tile, tile), tile), :]

    def values_fn(j, idx):
        return jnp.concatenate([vt_ref[0, j], ones], axis=0)

    for i in range(n_tiles):
        w_maps = _masked_maps(qt_ref[0, i])
        _flash_pair(i, w_maps, keys_fn, values_fn, lambda j, i=i: bias_ref[0, i - j], scratch, tile)
        o = (_normalized(acc_scr, 0, DIFF_V_DIM)
             - lam * _normalized(acc_scr, 1, DIFF_V_DIM))
        o = (o * _rms_scale(o, axis=0) * g_ref[...]) * (1.0 - lambda_init)
        o_ref[0, i * tile:(i + 1) * tile, :] = o.T.astype(o_ref.dtype)


def _fox_attn_kernel(qt_ref, qa_ref, k_ref, ka_ref, vt_ref, o_ref, *scratch):
    n_tiles, _, tile = qt_ref.shape[1:]
    acc_scr = scratch[1]
    ones = _ones_rows(tile)
    aug_zero = jnp.zeros((BF16_ROWS, tile), BF16)
    pad = jnp.zeros((MXU_DEPTH - LANES - 2 * BF16_ROWS, tile), BF16)

    def keys_fn(j):
        rows = pl.ds(pl.multiple_of(j * tile, tile), tile)
        return jnp.concatenate([k_ref[0, rows, :], ka_ref[0, rows, :]], axis=1)

    def values_fn(j, idx):
        rows = vt_ref[0, j, idx * FOX_HEAD_DIM:(idx + 1) * FOX_HEAD_DIM, :]
        return jnp.concatenate([rows, ones], axis=0)

    for i in range(n_tiles):
        q_a, q_b = _masked_maps(qt_ref[0, i])
        cols = slice(i * tile, (i + 1) * tile)
        w_maps = (jnp.concatenate([q_a, qa_ref[0, 0, :, cols], aug_zero, pad], axis=0),
                  jnp.concatenate([q_b, aug_zero, qa_ref[0, 1, :, cols], pad], axis=0))
        _flash_pair(i, w_maps, keys_fn, values_fn, lambda j: None, scratch, tile)
        o = jnp.concatenate([_normalized(acc_scr, 0, FOX_HEAD_DIM),
                             _normalized(acc_scr, 1, FOX_HEAD_DIM)], axis=0)
        o_ref[0, cols, :] = o.T.astype(o_ref.dtype)


def _attn_scratch(tile, value_rows):
    stat = pltpu.VMEM((2, 1, tile), F32)
    return [stat, pltpu.VMEM((2, value_rows + BF16_ROWS, tile), F32),
            pltpu.VMEM((2, tile, tile), F32), pltpu.VMEM((2, tile, tile), BF16), stat, stat]


def _diff_attention(fm, dk, bias_tiles, lam_vecs, subln_g, lambda_init):
    bsz, n_tiles, _, tile = fm.shape
    s = n_tiles * tile
    width = dk.shape[2]
    n_groups = width // LANES
    kern = functools.partial(_diff_attn_kernel, lambda_init=lambda_init)
    return pl.pallas_call(
        kern,
        grid=(n_groups, bsz),
        in_specs=[pl.BlockSpec((1, n_tiles, LANES, tile), lambda g, b: (b, 0, g, 0)),
                  pl.BlockSpec((1, s, LANES), lambda g, b: (b, 0, g)),
                  pl.BlockSpec((1, n_tiles, LANES, tile), lambda g, b: (b, 0, n_groups + g, 0)),
                  pl.BlockSpec((1, n_tiles, tile, tile), lambda g, b: (g, 0, 0, 0)),
                  pl.BlockSpec(lam_vecs.shape, lambda g, b: (0, 0)),
                  pl.BlockSpec((LANES, 1), lambda g, b: (0, 0))],
        out_specs=pl.BlockSpec((1, s, LANES), lambda g, b: (b, 0, g)),
        out_shape=jax.ShapeDtypeStruct((bsz, s, width), BF16),
        scratch_shapes=_attn_scratch(tile, DIFF_V_DIM),
        compiler_params=_params(("arbitrary", "arbitrary")),
        name="diff_attention",
    )(fm, dk, fm, bias_tiles, lam_vecs, subln_g.reshape(LANES, 1))


def _fox_attention(fm, fk, q_aug, k_aug):
    bsz, n_tiles, _, tile = fm.shape
    s = n_tiles * tile
    width = fk.shape[2]
    n_groups = width // LANES
    return pl.pallas_call(
        _fox_attn_kernel,
        grid=(n_groups, bsz),
        in_specs=[pl.BlockSpec((1, n_tiles, LANES, tile), lambda g, b: (b, 0, 2 * n_groups + g, 0)),
                  pl.BlockSpec((1, 2, BF16_ROWS, s), lambda g, b: (b, g, 0, 0)),
                  pl.BlockSpec((1, s, LANES), lambda g, b: (b, 0, g)),
                  pl.BlockSpec((1, s, LANES), lambda g, b: (b * n_groups + g, 0, 0)),
                  pl.BlockSpec((1, n_tiles, LANES, tile), lambda g, b: (b, 0, 3 * n_groups + g, 0))],
        out_specs=pl.BlockSpec((1, s, LANES), lambda g, b: (b, 0, g)),
        out_shape=jax.ShapeDtypeStruct((bsz, s, width), BF16),
        scratch_shapes=_attn_scratch(tile, FOX_HEAD_DIM),
        compiler_params=_params(("arbitrary", "arbitrary")),
        name="fox_attention",
    )(fm, q_aug, fk, k_aug, fm)


def _out_kernel(d_ref, f_ref, w_ref, x_ref, mod_ref, g_ref, x1_ref, h_ref, *, width):
    mix = (jnp.dot(d_ref[0], w_ref[:width, :], preferred_element_type=F32)
           + jnp.dot(f_ref[0], w_ref[width:, :], preferred_element_type=F32))
    x1 = x_ref[0] + mod_ref[0, 2:3, :] * mix
    x1_ref[0] = x1
    h = (x1 * _rms_scale(x1) * g_ref[...]) * (1.0 + mod_ref[0, 4:5, :]) + mod_ref[0, 3:4, :]
    h_ref[0] = h.astype(BF16)


def _out_projection(d_out, f_out, w_out, x, mod3, g):
    bsz, s, d = x.shape
    width = d_out.shape[2]
    ts = ROW_TILE
    row_spec = pl.BlockSpec((1, ts, d), lambda b, i: (b, i, 0))
    in_spec = pl.BlockSpec((1, ts, width), lambda b, i: (b, i, 0))
    kern = functools.partial(_out_kernel, width=width)
    return pl.pallas_call(
        kern,
        grid=(bsz, s // ts),
        in_specs=[in_spec, in_spec,
                  pl.BlockSpec(w_out.shape, lambda b, i: (0, 0)),
                  row_spec,
                  pl.BlockSpec((1, N_MOD, d), lambda b, i: (b, 0, 0)),
                  pl.BlockSpec((1, d), lambda b, i: (0, 0))],
        out_specs=[row_spec, row_spec],
        out_shape=[jax.ShapeDtypeStruct((bsz, s, d), F32), jax.ShapeDtypeStruct((bsz, s, d), BF16)],
        compiler_params=_params(("arbitrary", "arbitrary")),
        name="out_projection",
    )(d_out, f_out, w_out, x, mod3, g.reshape(1, d))


def _ffn_kernel(h_ref, x1_ref, mod_ref, wu_ref, cw_ref, cb_ref, wd_ref, fg_ref, o_ref,
                tail_scr, *, d_ff, chunk, final_norm):
    i = pl.program_id(1)
    ts = h_ref.shape[1]

    @pl.when(i == 0)
    def _():
        tail_scr[...] = jnp.zeros(tail_scr.shape, F32)

    hb = h_ref[0]

    def conv_cols(c0):
        u = jnp.dot(hb, wu_ref[:, c0:c0 + chunk], preferred_element_type=F32)
        ext = jnp.concatenate([tail_scr[:, c0:c0 + chunk], u], axis=0)
        tail_scr[:, c0:c0 + chunk] = u[ts - SUBLANES:, :]
        y = cb_ref[:, c0:c0 + chunk]
        for tap in range(CONV_WIDTH):
            lo = SUBLANES - (CONV_WIDTH - 1) + tap
            y = y + cw_ref[tap:tap + 1, c0:c0 + chunk] * ext[lo:lo + ts, :]
        return y

    acc = jnp.zeros((ts, o_ref.shape[2]), F32)
    for c in range(d_ff // chunk):
        gate = conv_cols(c * chunk)
        val = conv_cols(d_ff + c * chunk)
        act = (gate * jax.nn.sigmoid(gate) * val).astype(BF16)
        acc = acc + jnp.dot(act, wd_ref[c * chunk:(c + 1) * chunk, :], preferred_element_type=F32)
    x2 = x1_ref[0] + mod_ref[0, 5:6, :] * acc
    if final_norm:
        x2 = x2 * _rms_scale(x2) * fg_ref[...]
    o_ref[0] = x2


def _ffn(h2, x1, mod3, w_up, conv_w, conv_b, w_down, final_g, final_norm):
    bsz, s, d = x1.shape
    d_ff = w_down.shape[0]
    ts = ROW_TILE
    row_spec = pl.BlockSpec((1, ts, d), lambda b, i: (b, i, 0))
    kern = functools.partial(_ffn_kernel, d_ff=d_ff, chunk=FFN_CHUNK, final_norm=final_norm)
    const = lambda b, i: (0, 0)
    return pl.pallas_call(
        kern,
        grid=(bsz, s // ts),
        in_specs=[row_spec, row_spec,
                  pl.BlockSpec((1, N_MOD, d), lambda b, i: (b, 0, 0)),
                  pl.BlockSpec(w_up.shape, const, pipeline_mode=pl.Buffered(1)),
                  pl.BlockSpec(conv_w.shape, const),
                  pl.BlockSpec((1, 2 * d_ff), const),
                  pl.BlockSpec(w_down.shape, const, pipeline_mode=pl.Buffered(1)),
                  pl.BlockSpec((1, d), const)],
        out_specs=row_spec,
        out_shape=jax.ShapeDtypeStruct((bsz, s, d), F32),
        scratch_shapes=[pltpu.VMEM((SUBLANES, 2 * d_ff), F32)],
        compiler_params=_params(("arbitrary", "arbitrary")),
        name="conv_ffn",
    )(h2, x1, mod3, w_up, conv_w, conv_b.reshape(1, 2 * d_ff), w_down, final_g.reshape(1, d))


def kernel(x, c, ada_w, ada_b, attn_norm_g, w_in, forget_b, lambda_q1, lambda_k1, lambda_q2,
           lambda_k2, subln_g, rel_bias, w_out, ffn_norm_g, w_up, conv_w, conv_b, w_down,
           final_norm_g):
    bsz, s, d = x.shape
    depth = ada_w.shape[0]
    n_fox_heads = forget_b.shape[1]
    width = (w_in.shape[2] - n_fox_heads) // 6
    assert width % LANES == 0 and subln_g.shape[1] == DIFF_V_DIM == LANES
    assert n_fox_heads * FOX_HEAD_DIM == width and s % ATTN_TILE == 0 and s % ROW_TILE == 0
    n_pairs = n_fox_heads // 2
    qk_scale = DIFF_QK_DIM ** -0.5 * LOG2E

    bias = _bias_tiles(rel_bias.T, s, ATTN_TILE)

    for l in range(depth):
        lambda_init = 0.8 - 0.6 * math.exp(-0.3 * l)
        mod3 = _modulation(c, ada_w[l], ada_b[l]).reshape(bsz, N_MOD, d)

        w = w_in[l]
        cols = lambda n: w[:, n * width:(n + 1) * width]
        w_fm_t = jnp.concatenate([cols(0) * qk_scale, cols(2), cols(3) * qk_scale, cols(5)],
                                 axis=1).T.astype(BF16)
        w_tm = jnp.concatenate([cols(1), cols(4),
                                jnp.pad(w[:, 6 * width:], ((0, 0), (0, LANES - n_fox_heads)))],
                               axis=1).astype(BF16)
        fm, dk, fk, fl = _in_projection(x, mod3, attn_norm_g[l], w_fm_t, w_tm,
                                        width=width, n_fox_heads=n_fox_heads)

        q_aug, k_aug_t = _forget_aug(jnp.transpose(fl, (0, 2, 1)), forget_b[l])
        k_aug = jnp.transpose(k_aug_t.reshape(bsz * n_pairs, 2 * BF16_ROWS, s), (0, 2, 1))
        k_aug = jnp.pad(k_aug, ((0, 0), (0, 0), (0, LANES - 2 * BF16_ROWS)))

        lam_vecs = jnp.stack([lambda_q1[l], lambda_k1[l], lambda_q2[l], lambda_k2[l]]).astype(F32)
        d_out = _diff_attention(fm, dk, bias, lam_vecs, subln_g[l], lambda_init)
        f_out = _fox_attention(fm, fk, q_aug, k_aug)

        x1, h2 = _out_projection(d_out, f_out, w_out[l].astype(BF16), x, mod3, ffn_norm_g[l])
        x = _ffn(h2, x1, mod3, w_up[l].astype(BF16), conv_w[l], conv_b[l],
                 w_down[l].astype(BF16), final_norm_g, final_norm=(l == depth - 1))
    return x
```

```python
import functools
import math

import jax
import jax.numpy as jnp
from jax import lax
from jax.experimental import pallas as pl
from jax.experimental.pallas import tpu as pltpu

F32 = jnp.float32
BF16 = jnp.bfloat16

DIFF_QK_DIM = 64
DIFF_V_DIM = 2 * DIFF_QK_DIM
FOX_HEAD_DIM = 64
CONV_WIDTH = 3
NUM_BUCKETS = 32
MAX_EXACT = NUM_BUCKETS // 2
MAX_DISTANCE = 128
N_MOD = 6
NORM_EPS = 1e-6
NEG_INF = -1e30
LOG2E = math.log2(math.e)

LANES = 128
SUBLANES = 8
BF16_ROWS = 16
MXU_DEPTH = 256
VMEM_LIMIT_BYTES = 56 * 1024 * 1024

ATTN_TILE = 512
ROW_TILE = 512
FFN_CHUNK = 256
MOD_COL_TILE = 1536

NT_DIMS = (((1,), (1,)), ((), ()))


def _params(semantics):
    return pltpu.CompilerParams(dimension_semantics=semantics, vmem_limit_bytes=VMEM_LIMIT_BYTES)


def _rms_scale(x, axis=-1):
    return lax.rsqrt(jnp.mean(x * x, axis=axis, keepdims=True) + NORM_EPS)


def _mod_kernel(c_ref, w_ref, b_ref, o_ref):
    c = c_ref[...]
    act = c * jax.nn.sigmoid(c)
    o_ref[...] = jnp.dot(act.astype(BF16), w_ref[...].astype(BF16),
                         preferred_element_type=F32) + b_ref[...]


def _modulation(c, w, b):
    bsz, d = c.shape
    n = w.shape[1]
    tn = MOD_COL_TILE
    return pl.pallas_call(
        _mod_kernel,
        grid=(n // tn,),
        in_specs=[pl.BlockSpec((bsz, d), lambda j: (0, 0)),
                  pl.BlockSpec((d, tn), lambda j: (0, j)),
                  pl.BlockSpec((1, tn), lambda j: (0, j))],
        out_specs=pl.BlockSpec((bsz, tn), lambda j: (0, j)),
        out_shape=jax.ShapeDtypeStruct((bsz, n), F32),
        compiler_params=_params(("arbitrary",)),
        name="modulation",
    )(c, w, b.reshape(1, n))


def _proj_kernel(x_ref, mod_ref, g_ref, wt_ref, wk_ref, fm_ref, dk_ref, fk_ref, fl_ref,
                 *, width, n_fox_heads):
    x = x_ref[0]
    h = (x * _rms_scale(x) * g_ref[...]) * (1.0 + mod_ref[0, 1:2, :]) + mod_ref[0, 0:1, :]
    hb = h.astype(BF16)
    fm_ref[0, 0] = lax.dot_general(wt_ref[...], hb, NT_DIMS,
                                   preferred_element_type=F32).astype(BF16)
    r = jnp.dot(hb, wk_ref[...], preferred_element_type=F32)
    dk_ref[0] = r[:, :width].astype(BF16)
    fk_ref[0] = r[:, width:2 * width].astype(BF16)
    fl_ref[0] = r[:, 2 * width:2 * width + n_fox_heads]


def _in_projection(x, mod3, g, w_fm_t, w_tm, *, width, n_fox_heads):
    bsz, s, d = x.shape
    ts = ATTN_TILE
    k_shape = jax.ShapeDtypeStruct((bsz, s, width), BF16)
    k_spec = pl.BlockSpec((1, ts, width), lambda b, i: (b, i, 0))
    kern = functools.partial(_proj_kernel, width=width, n_fox_heads=n_fox_heads)
    return pl.pallas_call(
        kern,
        grid=(bsz, s // ts),
        in_specs=[pl.BlockSpec((1, ts, d), lambda b, i: (b, i, 0)),
                  pl.BlockSpec((1, N_MOD, d), lambda b, i: (b, 0, 0)),
                  pl.BlockSpec((1, d), lambda b, i: (0, 0)),
                  pl.BlockSpec(w_fm_t.shape, lambda b, i: (0, 0)),
                  pl.BlockSpec(w_tm.shape, lambda b, i: (0, 0))],
        out_specs=[pl.BlockSpec((1, 1, 4 * width, ts), lambda b, i: (b, i, 0, 0)),
                   k_spec, k_spec,
                   pl.BlockSpec((1, ts, n_fox_heads), lambda b, i: (b, i, 0))],
        out_shape=[jax.ShapeDtypeStruct((bsz, s // ts, 4 * width, ts), BF16),
                   k_shape, k_shape,
                   jax.ShapeDtypeStruct((bsz, s, n_fox_heads), F32)],
        compiler_params=_params(("arbitrary", "arbitrary")),
        name="in_projection",
    )(x, mod3, g.reshape(1, d), w_fm_t, w_tm)


def _split3(x):
    hi = x.astype(BF16)
    r1 = x - hi.astype(F32)
    mid = r1.astype(BF16)
    lo = (r1 - mid.astype(F32)).astype(BF16)
    return hi, mid, lo


def _cum_kernel(fl_ref, fb_ref, qa_ref, ka_ref):
    z = fl_ref[0] + fb_ref[...]
    acc = jnp.minimum(z, 0.0) - jnp.log1p(jnp.exp(-jnp.abs(z)))
    n_heads, s = acc.shape
    pos = lax.broadcasted_iota(jnp.int32, acc.shape, 1)
    shift = 1
    while shift < s:
        acc = acc + jnp.where(pos >= shift, pltpu.roll(acc, shift, 1), 0.0)
        shift *= 2
    acc = acc * LOG2E
    q_parts = _split3(acc)
    k_parts = _split3(-acc)
    ones = jnp.ones((3, s), BF16)
    zeros = jnp.zeros((BF16_ROWS - 6, s), BF16)
    for h in range(n_heads):
        qa_ref[0, h] = jnp.concatenate([p[h:h + 1] for p in q_parts] + [ones, zeros], axis=0)
        ka_ref[0, h] = jnp.concatenate([ones] + [p[h:h + 1] for p in k_parts] + [zeros], axis=0)


def _forget_aug(fl_t, forget_b):
    bsz, h, s = fl_t.shape
    aug_shape = jax.ShapeDtypeStruct((bsz, h, BF16_ROWS, s), BF16)
    aug_spec = pl.BlockSpec((1, h, BF16_ROWS, s), lambda b: (b, 0, 0, 0))
    return pl.pallas_call(
        _cum_kernel,
        grid=(bsz,),
        in_specs=[pl.BlockSpec((1, h, s), lambda b: (b, 0, 0)),
                  pl.BlockSpec((h, 1), lambda b: (0, 0))],
        out_specs=[aug_spec, aug_spec],
        out_shape=[aug_shape, aug_shape],
        compiler_params=_params(("arbitrary",)),
        name="forget_cumsum",
    )(fl_t, forget_b.reshape(h, 1))


def _bias_kernel(rb_ref, o_ref, *, tile, n_delta):
    h = pl.program_id(0)
    u = lax.broadcasted_iota(jnp.int32, (1, 2 * tile), 1)
    for dd in range(n_delta):
        n = jnp.maximum(dd * tile + u - tile, 0)
        nf = jnp.maximum(n, 1).astype(F32)
        large = MAX_EXACT + (jnp.log(nf / MAX_EXACT) / math.log(MAX_DISTANCE / MAX_EXACT)
                             * (NUM_BUCKETS - MAX_EXACT)).astype(jnp.int32)
        large = jnp.minimum(large, NUM_BUCKETS - 1)
        bucket = jnp.where(n < MAX_EXACT, n, large)
        row = jnp.zeros((1, 2 * tile), F32)
        for b in range(NUM_BUCKETS):
            row = jnp.where(bucket == b, rb_ref[h, b], row)
        row = row * LOG2E
        full = pltpu.roll(jnp.broadcast_to(row, (tile, 2 * tile)), 0, 1, stride=1, stride_axis=0)
        o_ref[0, dd] = full[:, tile:]


def _bias_tiles(rel_bias_t, seq, tile):
    n_heads = rel_bias_t.shape[0]
    n_delta = seq // tile
    kern = functools.partial(_bias_kernel, tile=tile, n_delta=n_delta)
    return pl.pallas_call(
        kern,
        grid=(n_heads,),
        in_specs=[pl.BlockSpec(memory_space=pltpu.SMEM)],
        out_specs=pl.BlockSpec((1, n_delta, tile, tile), lambda h: (h, 0, 0, 0)),
        out_shape=jax.ShapeDtypeStruct((n_heads, n_delta, tile, tile), F32),
        compiler_params=_params(("arbitrary",)),
        name="bias_tiles",
    )(rel_bias_t)


def _causal_flash(n_tiles, tile, scratch, w_maps_fn, keys_fn, values_fn, bias_fn, finish_fn):
    views = [[r.at[pl.ds(2 * parity, 2)] for r in scratch] for parity in range(2)]

    def start(i):
        m_scr, acc_scr = views[i % 2][:2]
        m_scr[...] = jnp.full(m_scr.shape, NEG_INF, F32)
        acc_scr[...] = jnp.zeros(acc_scr.shape, F32)

    def scores(i, w_maps, j):
        s_scr, mt_scr = views[i % 2][2], views[i % 2][4]
        keys = keys_fn(j)
        bias = bias_fn(i, j)
        for idx in range(2):
            st = jnp.dot(keys, w_maps[idx], preferred_element_type=F32)
            if bias is not None:
                st = st + bias
            s_scr[idx] = st
            mt_scr[idx] = jnp.max(st, axis=0, keepdims=True)

    def softmax(i, diag):
        m_scr, _, s_scr, p_scr, mt_scr, al_scr = views[i % 2]
        if diag:
            key_pos = lax.broadcasted_iota(jnp.int32, (tile, tile), 0)
            qry_pos = lax.broadcasted_iota(jnp.int32, (tile, tile), 1)
            causal = key_pos <= qry_pos
        for idx in range(2):
            st = s_scr[idx]
            if diag:
                st = jnp.where(causal, st, NEG_INF)
                m_tile = jnp.max(st, axis=0, keepdims=True)
            else:
                m_tile = mt_scr[idx]
            m_prev = m_scr[idx]
            m_new = jnp.maximum(m_prev, m_tile)
            p_scr[idx] = jnp.exp2(st - m_new).astype(BF16)
            al_scr[idx] = jnp.exp2(m_prev - m_new)
            m_scr[idx] = m_new

    def accumulate(i, j):
        _, acc_scr, _, p_scr, _, al_scr = views[i % 2]
        for idx in range(2):
            acc_scr[idx] = al_scr[idx] * acc_scr[idx] + jnp.dot(
                values_fn(j, idx), p_scr[idx], preferred_element_type=F32)

    w_maps = w_maps_fn(0)
    start(0)
    scores(0, w_maps, 0)
    for i in range(n_tiles):
        def body(j, carry, i=i, w_maps=w_maps):
            softmax(i, False)
            scores(i, w_maps, j + 1)
            accumulate(i, j)
            return carry

        if i > 0:
            lax.fori_loop(0, i, body, 0)
            finish_fn(i - 1, views[(i - 1) % 2][1])
        if i + 1 < n_tiles:
            w_maps = w_maps_fn(i + 1)
            start(i + 1)
            scores(i + 1, w_maps, 0)
        softmax(i, True)
        accumulate(i, i)
    finish_fn(n_tiles - 1, views[(n_tiles - 1) % 2][1])


def _masked_maps(qt):
    feat = lax.broadcasted_iota(jnp.int32, qt.shape, 0)
    zero = jnp.zeros_like(qt)
    return jnp.where(feat < DIFF_QK_DIM, qt, zero), jnp.where(feat >= DIFF_QK_DIM, qt, zero)


def _ones_rows(tile):
    row = lax.broadcasted_iota(jnp.int32, (BF16_ROWS, tile), 0)
    return jnp.where(row == 0, 1.0, 0.0).astype(BF16)


def _normalized(acc_ref, idx, rows):
    return acc_ref[idx, :rows, :] / acc_ref[idx, rows:rows + 1, :]


def _diff_attn_kernel(qt_ref, k_ref, vt_ref, bias_ref, lam_ref, g_ref, o_ref, *scratch,
                      lambda_init):
    n_tiles, _, tile = qt_ref.shape[1:]
    ones = _ones_rows(tile)
    lam = (jnp.exp(jnp.sum(lam_ref[0:1, :] * lam_ref[1:2, :], axis=-1, keepdims=True))
           - jnp.exp(jnp.sum(lam_ref[2:3, :] * lam_ref[3:4, :], axis=-1, keepdims=True))
           + lambda_init)

    def keys_fn(j):
        return k_ref[0, pl.ds(pl.multiple_of(j * tile, tile), tile), :]

    def values_fn(j, idx):
        return jnp.concatenate([vt_ref[0, j], ones], axis=0)

    def finish(i, acc_scr):
        o = (_normalized(acc_scr, 0, DIFF_V_DIM)
             - lam * _normalized(acc_scr, 1, DIFF_V_DIM))
        o = (o * _rms_scale(o, axis=0) * g_ref[...]) * (1.0 - lambda_init)
        o_ref[0, i * tile:(i + 1) * tile, :] = o.T.astype(o_ref.dtype)

    _causal_flash(n_tiles, tile, scratch,
                  lambda i: _masked_maps(qt_ref[0, i]),
                  keys_fn, values_fn, lambda i, j: bias_ref[0, i - j], finish)


def _fox_attn_kernel(qt_ref, qa_ref, k_ref, ka_ref, vt_ref, o_ref, *scratch):
    n_tiles, _, tile = qt_ref.shape[1:]
    ones = _ones_rows(tile)
    aug_zero = jnp.zeros((BF16_ROWS, tile), BF16)
    pad = jnp.zeros((MXU_DEPTH - LANES - 2 * BF16_ROWS, tile), BF16)

    def keys_fn(j):
        rows = pl.ds(pl.multiple_of(j * tile, tile), tile)
        return jnp.concatenate([k_ref[0, rows, :], ka_ref[0, rows, :]], axis=1)

    def values_fn(j, idx):
        rows = vt_ref[0, j, idx * FOX_HEAD_DIM:(idx + 1) * FOX_HEAD_DIM, :]
        return jnp.concatenate([rows, ones], axis=0)

    def finish(i, acc_scr):
        o = jnp.concatenate([_normalized(acc_scr, 0, FOX_HEAD_DIM),
                             _normalized(acc_scr, 1, FOX_HEAD_DIM)], axis=0)
        o_ref[0, i * tile:(i + 1) * tile, :] = o.T.astype(o_ref.dtype)

    def w_maps_fn(i):
        q_a, q_b = _masked_maps(qt_ref[0, i])
        cols = slice(i * tile, (i + 1) * tile)
        return (jnp.concatenate([q_a, qa_ref[0, 0, :, cols], aug_zero, pad], axis=0),
                jnp.concatenate([q_b, aug_zero, qa_ref[0, 1, :, cols], pad], axis=0))

    _causal_flash(n_tiles, tile, scratch, w_maps_fn, keys_fn, values_fn, lambda i, j: None, finish)


def _attn_scratch(tile, value_rows):
    stat = pltpu.VMEM((4, 1, tile), F32)
    return [stat, pltpu.VMEM((4, value_rows + BF16_ROWS, tile), F32),
            pltpu.VMEM((4, tile, tile), F32), pltpu.VMEM((4, tile, tile), BF16), stat, stat]


def _diff_attention(fm, dk, bias_tiles, lam_vecs, subln_g, lambda_init):
    bsz, n_tiles, _, tile = fm.shape
    s = n_tiles * tile
    width = dk.shape[2]
    n_groups = width // LANES
    kern = functools.partial(_diff_attn_kernel, lambda_init=lambda_init)
    return pl.pallas_call(
        kern,
        grid=(n_groups, bsz),
        in_specs=[pl.BlockSpec((1, n_tiles, LANES, tile), lambda g, b: (b, 0, g, 0)),
                  pl.BlockSpec((1, s, LANES), lambda g, b: (b, 0, g)),
                  pl.BlockSpec((1, n_tiles, LANES, tile), lambda g, b: (b, 0, n_groups + g, 0)),
                  pl.BlockSpec((1, n_tiles, tile, tile), lambda g, b: (g, 0, 0, 0)),
                  pl.BlockSpec(lam_vecs.shape, lambda g, b: (0, 0)),
                  pl.BlockSpec((LANES, 1), lambda g, b: (0, 0))],
        out_specs=pl.BlockSpec((1, s, LANES), lambda g, b: (b, 0, g)),
        out_shape=jax.ShapeDtypeStruct((bsz, s, width), BF16),
        scratch_shapes=_attn_scratch(tile, DIFF_V_DIM),
        compiler_params=_params(("arbitrary", "arbitrary")),
        name="diff_attention",
    )(fm, dk, fm, bias_tiles, lam_vecs, subln_g.reshape(LANES, 1))


def _fox_attention(fm, fk, q_aug, k_aug):
    bsz, n_tiles, _, tile = fm.shape
    s = n_tiles * tile
    width = fk.shape[2]
    n_groups = width // LANES
    return pl.pallas_call(
        _fox_attn_kernel,
        grid=(n_groups, bsz),
        in_specs=[pl.BlockSpec((1, n_tiles, LANES, tile), lambda g, b: (b, 0, 2 * n_groups + g, 0)),
                  pl.BlockSpec((1, 2, BF16_ROWS, s), lambda g, b: (b, g, 0, 0)),
                  pl.BlockSpec((1, s, LANES), lambda g, b: (b, 0, g)),
                  pl.BlockSpec((1, s, LANES), lambda g, b: (b * n_groups + g, 0, 0)),
                  pl.BlockSpec((1, n_tiles, LANES, tile), lambda g, b: (b, 0, 3 * n_groups + g, 0))],
        out_specs=pl.BlockSpec((1, s, LANES), lambda g, b: (b, 0, g)),
        out_shape=jax.ShapeDtypeStruct((bsz, s, width), BF16),
        scratch_shapes=_attn_scratch(tile, FOX_HEAD_DIM),
        compiler_params=_params(("arbitrary", "arbitrary")),
        name="fox_attention",
    )(fm, q_aug, fk, k_aug, fm)


def _out_kernel(d_ref, f_ref, w_ref, x_ref, mod_ref, g_ref, x1_ref, h_ref, *, width):
    mix = (jnp.dot(d_ref[0], w_ref[:width, :], preferred_element_type=F32)
           + jnp.dot(f_ref[0], w_ref[width:, :], preferred_element_type=F32))
    x1 = x_ref[0] + mod_ref[0, 2:3, :] * mix
    x1_ref[0] = x1
    h = (x1 * _rms_scale(x1) * g_ref[...]) * (1.0 + mod_ref[0, 4:5, :]) + mod_ref[0, 3:4, :]
    h_ref[0] = h.astype(BF16)


def _out_projection(d_out, f_out, w_out, x, mod3, g):
    bsz, s, d = x.shape
    width = d_out.shape[2]
    ts = ROW_TILE
    row_spec = pl.BlockSpec((1, ts, d), lambda b, i: (b, i, 0))
    in_spec = pl.BlockSpec((1, ts, width), lambda b, i: (b, i, 0))
    kern = functools.partial(_out_kernel, width=width)
    return pl.pallas_call(
        kern,
        grid=(bsz, s // ts),
        in_specs=[in_spec, in_spec,
                  pl.BlockSpec(w_out.shape, lambda b, i: (0, 0)),
                  row_spec,
                  pl.BlockSpec((1, N_MOD, d), lambda b, i: (b, 0, 0)),
                  pl.BlockSpec((1, d), lambda b, i: (0, 0))],
        out_specs=[row_spec, row_spec],
        out_shape=[jax.ShapeDtypeStruct((bsz, s, d), F32), jax.ShapeDtypeStruct((bsz, s, d), BF16)],
        compiler_params=_params(("arbitrary", "arbitrary")),
        name="out_projection",
    )(d_out, f_out, w_out, x, mod3, g.reshape(1, d))


def _ffn_kernel(h_ref, x1_ref, mod_ref, wu_ref, cw_ref, cb_ref, wd_ref, fg_ref, o_ref,
                tail_scr, *, d_ff, chunk, final_norm):
    i = pl.program_id(1)
    ts = h_ref.shape[1]

    @pl.when(i == 0)
    def _():
        tail_scr[...] = jnp.zeros(tail_scr.shape, F32)

    hb = h_ref[0]

    def conv_cols(c0):
        u = jnp.dot(hb, wu_ref[:, c0:c0 + chunk], preferred_element_type=F32)
        ext = jnp.concatenate([tail_scr[:, c0:c0 + chunk], u], axis=0)
        tail_scr[:, c0:c0 + chunk] = u[ts - SUBLANES:, :]
        y = cb_ref[:, c0:c0 + chunk]
        for tap in range(CONV_WIDTH):
            lo = SUBLANES - (CONV_WIDTH - 1) + tap
            y = y + cw_ref[tap:tap + 1, c0:c0 + chunk] * ext[lo:lo + ts, :]
        return y

    acc = jnp.zeros((ts, o_ref.shape[2]), F32)
    for c in range(d_ff // chunk):
        gate = conv_cols(c * chunk)
        val = conv_cols(d_ff + c * chunk)
        act = (gate * jax.nn.sigmoid(gate) * val).astype(BF16)
        acc = acc + jnp.dot(act, wd_ref[c * chunk:(c + 1) * chunk, :], preferred_element_type=F32)
    x2 = x1_ref[0] + mod_ref[0, 5:6, :] * acc
    if final_norm:
        x2 = x2 * _rms_scale(x2) * fg_ref[...]
    o_ref[0] = x2


def _ffn(h2, x1, mod3, w_up, conv_w, conv_b, w_down, final_g, final_norm):
    bsz, s, d = x1.shape
    d_ff = w_down.shape[0]
    ts = ROW_TILE
    row_spec = pl.BlockSpec((1, ts, d), lambda b, i: (b, i, 0))
    kern = functools.partial(_ffn_kernel, d_ff=d_ff, chunk=FFN_CHUNK, final_norm=final_norm)
    const = lambda b, i: (0, 0)
    return pl.pallas_call(
        kern,
        grid=(bsz, s // ts),
        in_specs=[row_spec, row_spec,
                  pl.BlockSpec((1, N_MOD, d), lambda b, i: (b, 0, 0)),
                  pl.BlockSpec(w_up.shape, const, pipeline_mode=pl.Buffered(1)),
                  pl.BlockSpec(conv_w.shape, const),
                  pl.BlockSpec((1, 2 * d_ff), const),
                  pl.BlockSpec(w_down.shape, const, pipeline_mode=pl.Buffered(1)),
                  pl.BlockSpec((1, d), const)],
        out_specs=row_spec,
        out_shape=jax.ShapeDtypeStruct((bsz, s, d), F32),
        scratch_shapes=[pltpu.VMEM((SUBLANES, 2 * d_ff), F32)],
        compiler_params=_params(("arbitrary", "arbitrary")),
        name="conv_ffn",
    )(h2, x1, mod3, w_up, conv_w, conv_b.reshape(1, 2 * d_ff), w_down, final_g.reshape(1, d))


def kernel(x, c, ada_w, ada_b, attn_norm_g, w_in, forget_b, lambda_q1, lambda_k1, lambda_q2,
           lambda_k2, subln_g, rel_bias, w_out, ffn_norm_g, w_up, conv_w, conv_b, w_down,
           final_norm_g):
    bsz, s, d = x.shape
    depth = ada_w.shape[0]
    n_fox_heads = forget_b.shape[1]
    width = (w_in.shape[2] - n_fox_heads) // 6
    assert width % LANES == 0 and subln_g.shape[1] == DIFF_V_DIM == LANES
    assert n_fox_heads * FOX_HEAD_DIM == width and s % ATTN_TILE == 0 and s % ROW_TILE == 0
    n_pairs = n_fox_heads // 2
    qk_scale = DIFF_QK_DIM ** -0.5 * LOG2E

    bias = _bias_tiles(rel_bias.T, s, ATTN_TILE)

    for l in range(depth):
        lambda_init = 0.8 - 0.6 * math.exp(-0.3 * l)
        mod3 = _modulation(c, ada_w[l], ada_b[l]).reshape(bsz, N_MOD, d)

        w = w_in[l]
        cols = lambda n: w[:, n * width:(n + 1) * width]
        w_fm_t = jnp.concatenate([cols(0) * qk_scale, cols(2), cols(3) * qk_scale, cols(5)],
                                 axis=1).T.astype(BF16)
        w_tm = jnp.concatenate([cols(1), cols(4),
                                jnp.pad(w[:, 6 * width:], ((0, 0), (0, LANES - n_fox_heads)))],
                               axis=1).astype(BF16)
        fm, dk, fk, fl = _in_projection(x, mod3, attn_norm_g[l], w_fm_t, w_tm,
                                        width=width, n_fox_heads=n_fox_heads)

        q_aug, k_aug_t = _forget_aug(jnp.transpose(fl, (0, 2, 1)), forget_b[l])
        k_aug = jnp.transpose(k_aug_t.reshape(bsz * n_pairs, 2 * BF16_ROWS, s), (0, 2, 1))
        k_aug = jnp.pad(k_aug, ((0, 0), (0, 0), (0, LANES - 2 * BF16_ROWS)))

        lam_vecs = jnp.stack([lambda_q1[l], lambda_k1[l], lambda_q2[l], lambda_k2[l]]).astype(F32)
        d_out = _diff_attention(fm, dk, bias, lam_vecs, subln_g[l], lambda_init)
        f_out = _fox_attention(fm, fk, q_aug, k_aug)

        x1, h2 = _out_projection(d_out, f_out, w_out[l].astype(BF16), x, mod3, ffn_norm_g[l])
        x = _ffn(h2, x1, mod3, w_up[l].astype(BF16), conv_w[l], conv_b[l],
                 w_down[l].astype(BF16), final_norm_g, final_norm=(l == depth - 1))
    return x
```

```python
import functools
import math

import jax
import jax.numpy as jnp
from jax import lax
from jax.experimental import pallas as pl
from jax.experimental.pallas import tpu as pltpu

F32 = jnp.float32
BF16 = jnp.bfloat16

DIFF_QK_DIM = 64
DIFF_V_DIM = 2 * DIFF_QK_DIM
FOX_HEAD_DIM = 64
CONV_WIDTH = 3
NUM_BUCKETS = 32
MAX_EXACT = NUM_BUCKETS // 2
MAX_DISTANCE = 128
N_MOD = 6
NORM_EPS = 1e-6
NEG_INF = -1e30
LOG2E = math.log2(math.e)

LANES = 128
SUBLANES = 8
BF16_ROWS = 16
MXU_DEPTH = 256
VMEM_LIMIT_BYTES = 56 * 1024 * 1024

ATTN_TILE = 512
ROW_TILE = 512
FFN_CHUNK = 256
FFN_SLOTS = 4
MOD_COL_TILE = 1536

NT_DIMS = (((1,), (1,)), ((), ()))


def _params(semantics):
    return pltpu.CompilerParams(dimension_semantics=semantics, vmem_limit_bytes=VMEM_LIMIT_BYTES)


def _rms_scale(x, axis=-1):
    return lax.rsqrt(jnp.mean(x * x, axis=axis, keepdims=True) + NORM_EPS)


def _mod_kernel(c_ref, w_ref, b_ref, o_ref):
    c = c_ref[...]
    act = c * jax.nn.sigmoid(c)
    o_ref[...] = jnp.dot(act.astype(BF16), w_ref[...].astype(BF16),
                         preferred_element_type=F32) + b_ref[...]


def _modulation(c, w, b):
    bsz, d = c.shape
    n = w.shape[1]
    tn = MOD_COL_TILE
    return pl.pallas_call(
        _mod_kernel,
        grid=(n // tn,),
        in_specs=[pl.BlockSpec((bsz, d), lambda j: (0, 0)),
                  pl.BlockSpec((d, tn), lambda j: (0, j)),
                  pl.BlockSpec((1, tn), lambda j: (0, j))],
        out_specs=pl.BlockSpec((bsz, tn), lambda j: (0, j)),
        out_shape=jax.ShapeDtypeStruct((bsz, n), F32),
        compiler_params=_params(("arbitrary",)),
        name="modulation",
    )(c, w, b.reshape(1, n))


def _proj_kernel(x_ref, mod_ref, g_ref, wt_ref, wk_ref, fm_ref, dk_ref, fk_ref, fl_ref,
                 *, width, n_fox_heads):
    x = x_ref[0]
    h = (x * _rms_scale(x) * g_ref[...]) * (1.0 + mod_ref[0, 1:2, :]) + mod_ref[0, 0:1, :]
    hb = h.astype(BF16)
    fm_ref[0, 0] = lax.dot_general(wt_ref[...], hb, NT_DIMS,
                                   preferred_element_type=F32).astype(BF16)
    r = jnp.dot(hb, wk_ref[...], preferred_element_type=F32)
    dk_ref[0] = r[:, :width].astype(BF16)
    fk_ref[0] = r[:, width:2 * width].astype(BF16)
    fl_ref[0] = r[:, 2 * width:2 * width + n_fox_heads]


def _in_projection(x, mod3, g, w_fm_t, w_tm, *, width, n_fox_heads):
    bsz, s, d = x.shape
    ts = ATTN_TILE
    k_shape = jax.ShapeDtypeStruct((bsz, s, width), BF16)
    k_spec = pl.BlockSpec((1, ts, width), lambda b, i: (b, i, 0))
    kern = functools.partial(_proj_kernel, width=width, n_fox_heads=n_fox_heads)
    return pl.pallas_call(
        kern,
        grid=(bsz, s // ts),
        in_specs=[pl.BlockSpec((1, ts, d), lambda b, i: (b, i, 0)),
                  pl.BlockSpec((1, N_MOD, d), lambda b, i: (b, 0, 0)),
                  pl.BlockSpec((1, d), lambda b, i: (0, 0)),
                  pl.BlockSpec(w_fm_t.shape, lambda b, i: (0, 0)),
                  pl.BlockSpec(w_tm.shape, lambda b, i: (0, 0))],
        out_specs=[pl.BlockSpec((1, 1, 4 * width, ts), lambda b, i: (b, i, 0, 0)),
                   k_spec, k_spec,
                   pl.BlockSpec((1, ts, n_fox_heads), lambda b, i: (b, i, 0))],
        out_shape=[jax.ShapeDtypeStruct((bsz, s // ts, 4 * width, ts), BF16),
                   k_shape, k_shape,
                   jax.ShapeDtypeStruct((bsz, s, n_fox_heads), F32)],
        compiler_params=_params(("arbitrary", "arbitrary")),
        name="in_projection",
    )(x, mod3, g.reshape(1, d), w_fm_t, w_tm)


def _split3(x):
    hi = x.astype(BF16)
    r1 = x - hi.astype(F32)
    mid = r1.astype(BF16)
    lo = (r1 - mid.astype(F32)).astype(BF16)
    return hi, mid, lo


def _cum_kernel(fl_ref, fb_ref, qa_ref, ka_ref):
    z = fl_ref[0] + fb_ref[...]
    acc = jnp.minimum(z, 0.0) - jnp.log1p(jnp.exp(-jnp.abs(z)))
    n_heads, s = acc.shape
    pos = lax.broadcasted_iota(jnp.int32, acc.shape, 1)
    shift = 1
    while shift < s:
        acc = acc + jnp.where(pos >= shift, pltpu.roll(acc, shift, 1), 0.0)
        shift *= 2
    acc = acc * LOG2E
    q_parts = _split3(acc)
    k_parts = _split3(-acc)
    ones = jnp.ones((3, s), BF16)
    zeros = jnp.zeros((BF16_ROWS - 6, s), BF16)
    for h in range(n_heads):
        qa_ref[0, h] = jnp.concatenate([p[h:h + 1] for p in q_parts] + [ones, zeros], axis=0)
        ka_ref[0, h] = jnp.concatenate([ones] + [p[h:h + 1] for p in k_parts] + [zeros], axis=0)


def _forget_aug(fl_t, forget_b):
    bsz, h, s = fl_t.shape
    aug_shape = jax.ShapeDtypeStruct((bsz, h, BF16_ROWS, s), BF16)
    aug_spec = pl.BlockSpec((1, h, BF16_ROWS, s), lambda b: (b, 0, 0, 0))
    return pl.pallas_call(
        _cum_kernel,
        grid=(bsz,),
        in_specs=[pl.BlockSpec((1, h, s), lambda b: (b, 0, 0)),
                  pl.BlockSpec((h, 1), lambda b: (0, 0))],
        out_specs=[aug_spec, aug_spec],
        out_shape=[aug_shape, aug_shape],
        compiler_params=_params(("arbitrary",)),
        name="forget_cumsum",
    )(fl_t, forget_b.reshape(h, 1))


def _bias_kernel(rb_ref, o_ref, *, tile, n_delta):
    h = pl.program_id(0)
    u = lax.broadcasted_iota(jnp.int32, (1, 2 * tile), 1)
    for dd in range(n_delta):
        n = jnp.maximum(dd * tile + u - tile, 0)
        nf = jnp.maximum(n, 1).astype(F32)
        large = MAX_EXACT + (jnp.log(nf / MAX_EXACT) / math.log(MAX_DISTANCE / MAX_EXACT)
                             * (NUM_BUCKETS - MAX_EXACT)).astype(jnp.int32)
        large = jnp.minimum(large, NUM_BUCKETS - 1)
        bucket = jnp.where(n < MAX_EXACT, n, large)
        row = jnp.zeros((1, 2 * tile), F32)
        for b in range(NUM_BUCKETS):
            row = jnp.where(bucket == b, rb_ref[h, b], row)
        row = row * LOG2E
        full = pltpu.roll(jnp.broadcast_to(row, (tile, 2 * tile)), 0, 1, stride=1, stride_axis=0)
        o_ref[0, dd] = full[:, tile:]


def _bias_tiles(rel_bias_t, seq, tile):
    n_heads = rel_bias_t.shape[0]
    n_delta = seq // tile
    kern = functools.partial(_bias_kernel, tile=tile, n_delta=n_delta)
    return pl.pallas_call(
        kern,
        grid=(n_heads,),
        in_specs=[pl.BlockSpec(memory_space=pltpu.SMEM)],
        out_specs=pl.BlockSpec((1, n_delta, tile, tile), lambda h: (h, 0, 0, 0)),
        out_shape=jax.ShapeDtypeStruct((n_heads, n_delta, tile, tile), F32),
        compiler_params=_params(("arbitrary",)),
        name="bias_tiles",
    )(rel_bias_t)


def _causal_flash(n_tiles, tile, scratch, w_maps_fn, keys_fn, values_fn, bias_fn, finish_fn):
    views = [[r.at[pl.ds(2 * parity, 2)] for r in scratch] for parity in range(2)]

    def start(i):
        m_scr, acc_scr = views[i % 2][:2]
        m_scr[...] = jnp.full(m_scr.shape, NEG_INF, F32)
        acc_scr[...] = jnp.zeros(acc_scr.shape, F32)

    def scores(i, w_maps, j):
        s_scr, mt_scr = views[i % 2][2], views[i % 2][4]
        keys = keys_fn(j)
        bias = bias_fn(i, j)
        for idx in range(2):
            st = jnp.dot(keys, w_maps[idx], preferred_element_type=F32)
            if bias is not None:
                st = st + bias
            s_scr[idx] = st
            mt_scr[idx] = jnp.max(st, axis=0, keepdims=True)

    def softmax(i, diag):
        m_scr, _, s_scr, p_scr, mt_scr, al_scr = views[i % 2]
        if diag:
            key_pos = lax.broadcasted_iota(jnp.int32, (tile, tile), 0)
            qry_pos = lax.broadcasted_iota(jnp.int32, (tile, tile), 1)
            causal = key_pos <= qry_pos
        for idx in range(2):
            st = s_scr[idx]
            if diag:
                st = jnp.where(causal, st, NEG_INF)
                m_tile = jnp.max(st, axis=0, keepdims=True)
            else:
                m_tile = mt_scr[idx]
            m_prev = m_scr[idx]
            m_new = jnp.maximum(m_prev, m_tile)
            p_scr[idx] = jnp.exp2(st - m_new).astype(BF16)
            al_scr[idx] = jnp.exp2(m_prev - m_new)
            m_scr[idx] = m_new

    def accumulate(i, j):
        _, acc_scr, _, p_scr, _, al_scr = views[i % 2]
        for idx in range(2):
            acc_scr[idx] = al_scr[idx] * acc_scr[idx] + jnp.dot(
                values_fn(j, idx), p_scr[idx], preferred_element_type=F32)

    w_maps = w_maps_fn(0)
    start(0)
    scores(0, w_maps, 0)
    for i in range(n_tiles):
        def body(j, carry, i=i, w_maps=w_maps):
            softmax(i, False)
            scores(i, w_maps, j + 1)
            accumulate(i, j)
            return carry

        if i > 0:
            lax.fori_loop(0, i, body, 0)
            finish_fn(i - 1, views[(i - 1) % 2][1])
        if i + 1 < n_tiles:
            w_maps = w_maps_fn(i + 1)
            start(i + 1)
            scores(i + 1, w_maps, 0)
        softmax(i, True)
        accumulate(i, i)
    finish_fn(n_tiles - 1, views[(n_tiles - 1) % 2][1])


def _masked_maps(qt):
    feat = lax.broadcasted_iota(jnp.int32, qt.shape, 0)
    zero = jnp.zeros_like(qt)
    return jnp.where(feat < DIFF_QK_DIM, qt, zero), jnp.where(feat >= DIFF_QK_DIM, qt, zero)


def _ones_rows(tile):
    row = lax.broadcasted_iota(jnp.int32, (BF16_ROWS, tile), 0)
    return jnp.where(row == 0, 1.0, 0.0).astype(BF16)


def _normalized(acc_ref, idx, rows):
    return acc_ref[idx, :rows, :] / acc_ref[idx, rows:rows + 1, :]


def _diff_attn_kernel(qt_ref, k_ref, vt_ref, bias_ref, lam_ref, g_ref, o_ref, *scratch,
                      lambda_init):
    n_tiles, _, tile = qt_ref.shape[1:]
    ones = _ones_rows(tile)
    lam = (jnp.exp(jnp.sum(lam_ref[0:1, :] * lam_ref[1:2, :], axis=-1, keepdims=True))
           - jnp.exp(jnp.sum(lam_ref[2:3, :] * lam_ref[3:4, :], axis=-1, keepdims=True))
           + lambda_init)

    def keys_fn(j):
        return k_ref[0, pl.ds(pl.multiple_of(j * tile, tile), tile), :]

    def values_fn(j, idx):
        return jnp.concatenate([vt_ref[0, j], ones], axis=0)

    def finish(i, acc_scr):
        o = (_normalized(acc_scr, 0, DIFF_V_DIM)
             - lam * _normalized(acc_scr, 1, DIFF_V_DIM))
        o = (o * _rms_scale(o, axis=0) * g_ref[...]) * (1.0 - lambda_init)
        o_ref[0, i * tile:(i + 1) * tile, :] = o.T.astype(o_ref.dtype)

    _causal_flash(n_tiles, tile, scratch,
                  lambda i: _masked_maps(qt_ref[0, i]),
                  keys_fn, values_fn, lambda i, j: bias_ref[0, i - j], finish)


def _fox_attn_kernel(qt_ref, qa_ref, k_ref, ka_ref, vt_ref, o_ref, *scratch):
    n_tiles, _, tile = qt_ref.shape[1:]
    ones = _ones_rows(tile)
    aug_zero = jnp.zeros((BF16_ROWS, tile), BF16)
    pad = jnp.zeros((MXU_DEPTH - LANES - 2 * BF16_ROWS, tile), BF16)

    def keys_fn(j):
        rows = pl.ds(pl.multiple_of(j * tile, tile), tile)
        return jnp.concatenate([k_ref[0, rows, :], ka_ref[0, rows, :]], axis=1)

    def values_fn(j, idx):
        rows = vt_ref[0, j, idx * FOX_HEAD_DIM:(idx + 1) * FOX_HEAD_DIM, :]
        return jnp.concatenate([rows, ones], axis=0)

    def finish(i, acc_scr):
        o = jnp.concatenate([_normalized(acc_scr, 0, FOX_HEAD_DIM),
                             _normalized(acc_scr, 1, FOX_HEAD_DIM)], axis=0)
        o_ref[0, i * tile:(i + 1) * tile, :] = o.T.astype(o_ref.dtype)

    def w_maps_fn(i):
        q_a, q_b = _masked_maps(qt_ref[0, i])
        cols = slice(i * tile, (i + 1) * tile)
        return (jnp.concatenate([q_a, qa_ref[0, 0, :, cols], aug_zero, pad], axis=0),
                jnp.concatenate([q_b, aug_zero, qa_ref[0, 1, :, cols], pad], axis=0))

    _causal_flash(n_tiles, tile, scratch, w_maps_fn, keys_fn, values_fn, lambda i, j: None, finish)


def _attn_scratch(tile, value_rows):
    stat = pltpu.VMEM((4, 1, tile), F32)
    return [stat, pltpu.VMEM((4, value_rows + BF16_ROWS, tile), F32),
            pltpu.VMEM((4, tile, tile), F32), pltpu.VMEM((4, tile, tile), BF16), stat, stat]


def _diff_attention(fm, dk, bias_tiles, lam_vecs, subln_g, lambda_init):
    bsz, n_tiles, _, tile = fm.shape
    s = n_tiles * tile
    width = dk.shape[2]
    n_groups = width // LANES
    kern = functools.partial(_diff_attn_kernel, lambda_init=lambda_init)
    return pl.pallas_call(
        kern,
        grid=(n_groups, bsz),
        in_specs=[pl.BlockSpec((1, n_tiles, LANES, tile), lambda g, b: (b, 0, g, 0)),
                  pl.BlockSpec((1, s, LANES), lambda g, b: (b, 0, g)),
                  pl.BlockSpec((1, n_tiles, LANES, tile), lambda g, b: (b, 0, n_groups + g, 0)),
                  pl.BlockSpec((1, n_tiles, tile, tile), lambda g, b: (g, 0, 0, 0)),
                  pl.BlockSpec(lam_vecs.shape, lambda g, b: (0, 0)),
                  pl.BlockSpec((LANES, 1), lambda g, b: (0, 0))],
        out_specs=pl.BlockSpec((1, s, LANES), lambda g, b: (b, 0, g)),
        out_shape=jax.ShapeDtypeStruct((bsz, s, width), BF16),
        scratch_shapes=_attn_scratch(tile, DIFF_V_DIM),
        compiler_params=_params(("arbitrary", "arbitrary")),
        name="diff_attention",
    )(fm, dk, fm, bias_tiles, lam_vecs, subln_g.reshape(LANES, 1))


def _fox_attention(fm, fk, q_aug, k_aug):
    bsz, n_tiles, _, tile = fm.shape
    s = n_tiles * tile
    width = fk.shape[2]
    n_groups = width // LANES
    return pl.pallas_call(
        _fox_attn_kernel,
        grid=(n_groups, bsz),
        in_specs=[pl.BlockSpec((1, n_tiles, LANES, tile), lambda g, b: (b, 0, 2 * n_groups + g, 0)),
                  pl.BlockSpec((1, 2, BF16_ROWS, s), lambda g, b: (b, g, 0, 0)),
                  pl.BlockSpec((1, s, LANES), lambda g, b: (b, 0, g)),
                  pl.BlockSpec((1, s, LANES), lambda g, b: (b * n_groups + g, 0, 0)),
                  pl.BlockSpec((1, n_tiles, LANES, tile), lambda g, b: (b, 0, 3 * n_groups + g, 0))],
        out_specs=pl.BlockSpec((1, s, LANES), lambda g, b: (b, 0, g)),
        out_shape=jax.ShapeDtypeStruct((bsz, s, width), BF16),
        scratch_shapes=_attn_scratch(tile, FOX_HEAD_DIM),
        compiler_params=_params(("arbitrary", "arbitrary")),
        name="fox_attention",
    )(fm, q_aug, fk, k_aug, fm)


def _out_kernel(d_ref, f_ref, w_ref, x_ref, mod_ref, g_ref, x1_ref, h_ref, *, width):
    mix = (jnp.dot(d_ref[0], w_ref[:width, :], preferred_element_type=F32)
           + jnp.dot(f_ref[0], w_ref[width:, :], preferred_element_type=F32))
    x1 = x_ref[0] + mod_ref[0, 2:3, :] * mix
    x1_ref[0] = x1
    h = (x1 * _rms_scale(x1) * g_ref[...]) * (1.0 + mod_ref[0, 4:5, :]) + mod_ref[0, 3:4, :]
    h_ref[0] = h.astype(BF16)


def _out_projection(d_out, f_out, w_out, x, mod3, g):
    bsz, s, d = x.shape
    width = d_out.shape[2]
    ts = ROW_TILE
    row_spec = pl.BlockSpec((1, ts, d), lambda b, i: (b, i, 0))
    in_spec = pl.BlockSpec((1, ts, width), lambda b, i: (b, i, 0))
    kern = functools.partial(_out_kernel, width=width)
    return pl.pallas_call(
        kern,
        grid=(bsz, s // ts),
        in_specs=[in_spec, in_spec,
                  pl.BlockSpec(w_out.shape, lambda b, i: (0, 0)),
                  row_spec,
                  pl.BlockSpec((1, N_MOD, d), lambda b, i: (b, 0, 0)),
                  pl.BlockSpec((1, d), lambda b, i: (0, 0))],
        out_specs=[row_spec, row_spec],
        out_shape=[jax.ShapeDtypeStruct((bsz, s, d), F32), jax.ShapeDtypeStruct((bsz, s, d), BF16)],
        compiler_params=_params(("arbitrary", "arbitrary")),
        name="out_projection",
    )(d_out, f_out, w_out, x, mod3, g.reshape(1, d))


def _ffn_kernel(h_ref, x1_ref, mod_ref, wu_ref, cw_ref, cb_ref, wd_ref, fg_ref, o_ref,
                tail_scr, u_scr, *, d_ff, chunk, final_norm):
    i = pl.program_id(1)
    ts = h_ref.shape[1]
    lane_tiles = chunk // LANES

    @pl.when(i == 0)
    def _():
        tail_scr[...] = jnp.zeros(tail_scr.shape, F32)

    def conv_cols(c0, slot):
        u = jnp.dot(h_ref[0], wu_ref[:, c0:c0 + chunk], preferred_element_type=F32)
        taps = []
        for t in range(lane_tiles):
            cols = slice(c0 + t * LANES, c0 + (t + 1) * LANES)
            buf = u_scr.at[slot * lane_tiles + t]
            buf[0:SUBLANES, :] = tail_scr[:, cols]
            buf[SUBLANES:, :] = u[:, t * LANES:(t + 1) * LANES]
            tail_scr[:, cols] = u[ts - SUBLANES:, t * LANES:(t + 1) * LANES]
            y = cb_ref[:, cols]
            for tap in range(CONV_WIDTH):
                lo = SUBLANES - (CONV_WIDTH - 1) + tap
                y = y + cw_ref[tap:tap + 1, cols] * buf[pl.ds(lo, ts), :]
            taps.append(y)
        return jnp.concatenate(taps, axis=1)

    acc = jnp.zeros((ts, o_ref.shape[2]), F32)
    for c in range(d_ff // chunk):
        gate = conv_cols(c * chunk, (2 * c) % FFN_SLOTS)
        val = conv_cols(d_ff + c * chunk, (2 * c + 1) % FFN_SLOTS)
        act = (gate * jax.nn.sigmoid(gate) * val).astype(BF16)
        acc = acc + jnp.dot(act, wd_ref[c * chunk:(c + 1) * chunk, :], preferred_element_type=F32)
    x2 = x1_ref[0] + mod_ref[0, 5:6, :] * acc
    if final_norm:
        x2 = x2 * _rms_scale(x2) * fg_ref[...]
    o_ref[0] = x2


def _ffn(h2, x1, mod3, w_up, conv_w, conv_b, w_down, final_g, final_norm):
    bsz, s, d = x1.shape
    d_ff = w_down.shape[0]
    ts = ROW_TILE
    row_spec = pl.BlockSpec((1, ts, d), lambda b, i: (b, i, 0))
    kern = functools.partial(_ffn_kernel, d_ff=d_ff, chunk=FFN_CHUNK, final_norm=final_norm)
    const = lambda b, i: (0, 0)
    return pl.pallas_call(
        kern,
        grid=(bsz, s // ts),
        in_specs=[row_spec, row_spec,
                  pl.BlockSpec((1, N_MOD, d), lambda b, i: (b, 0, 0)),
                  pl.BlockSpec(w_up.shape, const, pipeline_mode=pl.Buffered(1)),
                  pl.BlockSpec(conv_w.shape, const),
                  pl.BlockSpec((1, 2 * d_ff), const),
                  pl.BlockSpec(w_down.shape, const, pipeline_mode=pl.Buffered(1)),
                  pl.BlockSpec((1, d), const)],
        out_specs=row_spec,
        out_shape=jax.ShapeDtypeStruct((bsz, s, d), F32),
        scratch_shapes=[pltpu.VMEM((SUBLANES, 2 * d_ff), F32),
                        pltpu.VMEM((FFN_SLOTS * FFN_CHUNK // LANES, ts + SUBLANES, LANES), F32)],
        compiler_params=_params(("arbitrary", "arbitrary")),
        name="conv_ffn",
    )(h2, x1, mod3, w_up, conv_w, conv_b.reshape(1, 2 * d_ff), w_down, final_g.reshape(1, d))


def kernel(x, c, ada_w, ada_b, attn_norm_g, w_in, forget_b, lambda_q1, lambda_k1, lambda_q2,
           lambda_k2, subln_g, rel_bias, w_out, ffn_norm_g, w_up, conv_w, conv_b, w_down,
           final_norm_g):
    bsz, s, d = x.shape
    depth = ada_w.shape[0]
    n_fox_heads = forget_b.shape[1]
    width = (w_in.shape[2] - n_fox_heads) // 6
    assert width % LANES == 0 and subln_g.shape[1] == DIFF_V_DIM == LANES
    assert n_fox_heads * FOX_HEAD_DIM == width and s % ATTN_TILE == 0 and s % ROW_TILE == 0
    n_pairs = n_fox_heads // 2
    qk_scale = DIFF_QK_DIM ** -0.5 * LOG2E

    bias = _bias_tiles(rel_bias.T, s, ATTN_TILE)

    for l in range(depth):
        lambda_init = 0.8 - 0.6 * math.exp(-0.3 * l)
        mod3 = _modulation(c, ada_w[l], ada_b[l]).reshape(bsz, N_MOD, d)

        w = w_in[l]
        cols = lambda n: w[:, n * width:(n + 1) * width]
        w_fm_t = jnp.concatenate([cols(0) * qk_scale, cols(2), cols(3) * qk_scale, cols(5)],
                                 axis=1).T.astype(BF16)
        w_tm = jnp.concatenate([cols(1), cols(4),
                                jnp.pad(w[:, 6 * width:], ((0, 0), (0, LANES - n_fox_heads)))],
                               axis=1).astype(BF16)
        fm, dk, fk, fl = _in_projection(x, mod3, attn_norm_g[l], w_fm_t, w_tm,
                                        width=width, n_fox_heads=n_fox_heads)

        q_aug, k_aug_t = _forget_aug(jnp.transpose(fl, (0, 2, 1)), forget_b[l])
        k_aug = jnp.transpose(k_aug_t.reshape(bsz * n_pairs, 2 * BF16_ROWS, s), (0, 2, 1))
        k_aug = jnp.pad(k_aug, ((0, 0), (0, 0), (0, LANES - 2 * BF16_ROWS)))

        lam_vecs = jnp.stack([lambda_q1[l], lambda_k1[l], lambda_q2[l], lambda_k2[l]]).astype(F32)
        d_out = _diff_attention(fm, dk, bias, lam_vecs, subln_g[l], lambda_init)
        f_out = _fox_attention(fm, fk, q_aug, k_aug)

        x1, h2 = _out_projection(d_out, f_out, w_out[l].astype(BF16), x, mod3, ffn_norm_g[l])
        x = _ffn(h2, x1, mod3, w_up[l].astype(BF16), conv_w[l], conv_b[l],
                 w_down[l].astype(BF16), final_norm_g, final_norm=(l == depth - 1))
    return x
```

```python
import functools
import math

import jax
import jax.numpy as jnp
from jax import lax
from jax.experimental import pallas as pl
from jax.experimental.pallas import tpu as pltpu

F32 = jnp.float32
BF16 = jnp.bfloat16

DIFF_QK_DIM = 64
DIFF_V_DIM = 2 * DIFF_QK_DIM
FOX_HEAD_DIM = 64
CONV_WIDTH = 3
NUM_BUCKETS = 32
MAX_EXACT = NUM_BUCKETS // 2
MAX_DISTANCE = 128
N_MOD = 6
NORM_EPS = 1e-6
NEG_INF = -1e30
LOG2E = math.log2(math.e)

LANES = 128
SUBLANES = 8
BF16_ROWS = 16
MXU_DEPTH = 256
VMEM_LIMIT_BYTES = 56 * 1024 * 1024

ATTN_TILE = 512
ROW_TILE = 512
FFN_CHUNK = 256
FFN_SLOTS = 4
MOD_COL_TILE = 1536

NT_DIMS = (((1,), (1,)), ((), ()))


def _params(semantics):
    return pltpu.CompilerParams(dimension_semantics=semantics, vmem_limit_bytes=VMEM_LIMIT_BYTES)


def _rms_scale(x, axis=-1):
    return lax.rsqrt(jnp.mean(x * x, axis=axis, keepdims=True) + NORM_EPS)


def _mod_kernel(c_ref, w_ref, b_ref, o_ref):
    c = c_ref[...]
    act = c * jax.nn.sigmoid(c)
    o_ref[...] = jnp.dot(act.astype(BF16), w_ref[...].astype(BF16),
                         preferred_element_type=F32) + b_ref[...]


def _modulation(c, w, b):
    bsz, d = c.shape
    n = w.shape[1]
    tn = MOD_COL_TILE
    return pl.pallas_call(
        _mod_kernel,
        grid=(n // tn,),
        in_specs=[pl.BlockSpec((bsz, d), lambda j: (0, 0)),
                  pl.BlockSpec((d, tn), lambda j: (0, j)),
                  pl.BlockSpec((1, tn), lambda j: (0, j))],
        out_specs=pl.BlockSpec((bsz, tn), lambda j: (0, j)),
        out_shape=jax.ShapeDtypeStruct((bsz, n), F32),
        compiler_params=_params(("arbitrary",)),
        name="modulation",
    )(c, w, b.reshape(1, n))


def _proj_kernel(x_ref, mod_ref, g_ref, wt_ref, wk_ref, fm_ref, dk_ref, fk_ref, fl_ref,
                 *, width, n_fox_heads):
    x = x_ref[0]
    h = (x * _rms_scale(x) * g_ref[...]) * (1.0 + mod_ref[0, 1:2, :]) + mod_ref[0, 0:1, :]
    hb = h.astype(BF16)
    fm_ref[0, 0] = lax.dot_general(wt_ref[...], hb, NT_DIMS,
                                   preferred_element_type=F32).astype(BF16)
    r = jnp.dot(hb, wk_ref[...], preferred_element_type=F32)
    dk_ref[0] = r[:, :width].astype(BF16)
    fk_ref[0] = r[:, width:2 * width].astype(BF16)
    fl_ref[0] = r[:, 2 * width:2 * width + n_fox_heads]


def _in_projection(x, mod3, g, w_fm_t, w_tm, *, width, n_fox_heads):
    bsz, s, d = x.shape
    ts = ATTN_TILE
    k_shape = jax.ShapeDtypeStruct((bsz, s, width), BF16)
    k_spec = pl.BlockSpec((1, ts, width), lambda b, i: (b, i, 0))
    kern = functools.partial(_proj_kernel, width=width, n_fox_heads=n_fox_heads)
    return pl.pallas_call(
        kern,
        grid=(bsz, s // ts),
        in_specs=[pl.BlockSpec((1, ts, d), lambda b, i: (b, i, 0)),
                  pl.BlockSpec((1, N_MOD, d), lambda b, i: (b, 0, 0)),
                  pl.BlockSpec((1, d), lambda b, i: (0, 0)),
                  pl.BlockSpec(w_fm_t.shape, lambda b, i: (0, 0)),
                  pl.BlockSpec(w_tm.shape, lambda b, i: (0, 0))],
        out_specs=[pl.BlockSpec((1, 1, 4 * width, ts), lambda b, i: (b, i, 0, 0)),
                   k_spec, k_spec,
                   pl.BlockSpec((1, ts, n_fox_heads), lambda b, i: (b, i, 0))],
        out_shape=[jax.ShapeDtypeStruct((bsz, s // ts, 4 * width, ts), BF16),
                   k_shape, k_shape,
                   jax.ShapeDtypeStruct((bsz, s, n_fox_heads), F32)],
        compiler_params=_params(("arbitrary", "arbitrary")),
        name="in_projection",
    )(x, mod3, g.reshape(1, d), w_fm_t, w_tm)


def _split3(x):
    hi = x.astype(BF16)
    r1 = x - hi.astype(F32)
    mid = r1.astype(BF16)
    lo = (r1 - mid.astype(F32)).astype(BF16)
    return hi, mid, lo


def _cum_kernel(fl_ref, fb_ref, qa_ref, ka_ref):
    z = fl_ref[0] + fb_ref[...]
    acc = jnp.minimum(z, 0.0) - jnp.log1p(jnp.exp(-jnp.abs(z)))
    n_heads, s = acc.shape
    pos = lax.broadcasted_iota(jnp.int32, acc.shape, 1)
    shift = 1
    while shift < s:
        acc = acc + jnp.where(pos >= shift, pltpu.roll(acc, shift, 1), 0.0)
        shift *= 2
    acc = acc * LOG2E
    q_parts = _split3(acc)
    k_parts = _split3(-acc)
    ones = jnp.ones((3, s), BF16)
    zeros = jnp.zeros((BF16_ROWS - 6, s), BF16)
    for h in range(n_heads):
        qa_ref[0, h] = jnp.concatenate([p[h:h + 1] for p in q_parts] + [ones, zeros], axis=0)
        ka_ref[0, h] = jnp.concatenate([ones] + [p[h:h + 1] for p in k_parts] + [zeros], axis=0)


def _forget_aug(fl_t, forget_b):
    bsz, h, s = fl_t.shape
    aug_shape = jax.ShapeDtypeStruct((bsz, h, BF16_ROWS, s), BF16)
    aug_spec = pl.BlockSpec((1, h, BF16_ROWS, s), lambda b: (b, 0, 0, 0))
    return pl.pallas_call(
        _cum_kernel,
        grid=(bsz,),
        in_specs=[pl.BlockSpec((1, h, s), lambda b: (b, 0, 0)),
                  pl.BlockSpec((h, 1), lambda b: (0, 0))],
        out_specs=[aug_spec, aug_spec],
        out_shape=[aug_shape, aug_shape],
        compiler_params=_params(("arbitrary",)),
        name="forget_cumsum",
    )(fl_t, forget_b.reshape(h, 1))


def _bias_kernel(rb_ref, o_ref, *, tile, n_delta):
    h = pl.program_id(0)
    u = lax.broadcasted_iota(jnp.int32, (1, 2 * tile), 1)
    for dd in range(n_delta):
        n = jnp.maximum(dd * tile + u - tile, 0)
        nf = jnp.maximum(n, 1).astype(F32)
        large = MAX_EXACT + (jnp.log(nf / MAX_EXACT) / math.log(MAX_DISTANCE / MAX_EXACT)
                             * (NUM_BUCKETS - MAX_EXACT)).astype(jnp.int32)
        large = jnp.minimum(large, NUM_BUCKETS - 1)
        bucket = jnp.where(n < MAX_EXACT, n, large)
        row = jnp.zeros((1, 2 * tile), F32)
        for b in range(NUM_BUCKETS):
            row = jnp.where(bucket == b, rb_ref[h, b], row)
        row = row * LOG2E
        full = pltpu.roll(jnp.broadcast_to(row, (tile, 2 * tile)), 0, 1, stride=1, stride_axis=0)
        o_ref[0, dd] = full[:, tile:]


def _bias_tiles(rel_bias_t, seq, tile):
    n_heads = rel_bias_t.shape[0]
    n_delta = seq // tile
    kern = functools.partial(_bias_kernel, tile=tile, n_delta=n_delta)
    return pl.pallas_call(
        kern,
        grid=(n_heads,),
        in_specs=[pl.BlockSpec(memory_space=pltpu.SMEM)],
        out_specs=pl.BlockSpec((1, n_delta, tile, tile), lambda h: (h, 0, 0, 0)),
        out_shape=jax.ShapeDtypeStruct((n_heads, n_delta, tile, tile), F32),
        compiler_params=_params(("arbitrary",)),
        name="bias_tiles",
    )(rel_bias_t)


def _causal_flash(n_tiles, tile, scratch, w_maps_fn, keys_fn, values_fn, bias_fn, finish_fn):
    views = [[r.at[pl.ds(2 * parity, 2)] for r in scratch] for parity in range(2)]

    def start(i):
        m_scr, acc_scr = views[i % 2][:2]
        m_scr[...] = jnp.full(m_scr.shape, NEG_INF, F32)
        acc_scr[...] = jnp.zeros(acc_scr.shape, F32)

    def scores(i, w_maps, j):
        s_scr, mt_scr = views[i % 2][2], views[i % 2][4]
        keys = keys_fn(j)
        bias = bias_fn(i, j)
        for idx in range(2):
            st = jnp.dot(keys, w_maps[idx], preferred_element_type=F32)
            if bias is not None:
                st = st + bias
            s_scr[idx, :, :tile] = st
            mt_scr[idx] = jnp.max(st, axis=0, keepdims=True)

    def softmax(i, diag):
        m_scr, _, s_scr, p_scr, mt_scr, al_scr = views[i % 2]
        if diag:
            key_pos = lax.broadcasted_iota(jnp.int32, (tile, tile), 0)
            qry_pos = lax.broadcasted_iota(jnp.int32, (tile, tile), 1)
            causal = key_pos <= qry_pos
        for idx in range(2):
            st = s_scr[idx, :, :tile]
            if diag:
                st = jnp.where(causal, st, NEG_INF)
                m_tile = jnp.max(st, axis=0, keepdims=True)
            else:
                m_tile = mt_scr[idx]
            m_prev = m_scr[idx]
            m_new = jnp.maximum(m_prev, m_tile)
            p_scr[idx, :, :tile] = jnp.exp2(st - m_new).astype(BF16)
            al_scr[idx] = jnp.exp2(m_prev - m_new)
            m_scr[idx] = m_new

    def accumulate(i, j):
        _, acc_scr, _, p_scr, _, al_scr = views[i % 2]
        for idx in range(2):
            acc_scr[idx] = al_scr[idx] * acc_scr[idx] + jnp.dot(
                values_fn(j, idx), p_scr[idx, :, :tile], preferred_element_type=F32)

    w_maps = w_maps_fn(0)
    start(0)
    scores(0, w_maps, 0)
    for i in range(n_tiles):
        def body(j, carry, i=i, w_maps=w_maps):
            softmax(i, False)
            scores(i, w_maps, j + 1)
            accumulate(i, j)
            return carry

        if i > 0:
            lax.fori_loop(0, i, body, 0)
            finish_fn(i - 1, views[(i - 1) % 2][1])
        if i + 1 < n_tiles:
            w_maps = w_maps_fn(i + 1)
            start(i + 1)
            scores(i + 1, w_maps, 0)
        softmax(i, True)
        accumulate(i, i)
    finish_fn(n_tiles - 1, views[(n_tiles - 1) % 2][1])


def _masked_maps(qt):
    feat = lax.broadcasted_iota(jnp.int32, qt.shape, 0)
    zero = jnp.zeros_like(qt)
    return jnp.where(feat < DIFF_QK_DIM, qt, zero), jnp.where(feat >= DIFF_QK_DIM, qt, zero)


def _ones_rows(tile):
    row = lax.broadcasted_iota(jnp.int32, (BF16_ROWS, tile), 0)
    return jnp.where(row == 0, 1.0, 0.0).astype(BF16)


def _normalized(acc_ref, idx, rows):
    return acc_ref[idx, :rows, :] / acc_ref[idx, rows:rows + 1, :]


def _diff_attn_kernel(qt_ref, k_ref, vt_ref, bias_ref, lam_ref, g_ref, o_ref, *scratch,
                      lambda_init):
    n_tiles, _, tile = qt_ref.shape[1:]
    ones = _ones_rows(tile)
    lam = (jnp.exp(jnp.sum(lam_ref[0:1, :] * lam_ref[1:2, :], axis=-1, keepdims=True))
           - jnp.exp(jnp.sum(lam_ref[2:3, :] * lam_ref[3:4, :], axis=-1, keepdims=True))
           + lambda_init)

    def keys_fn(j):
        return k_ref[0, pl.ds(pl.multiple_of(j * tile, tile), tile), :]

    def values_fn(j, idx):
        return jnp.concatenate([vt_ref[0, j], ones], axis=0)

    def finish(i, acc_scr):
        o = (_normalized(acc_scr, 0, DIFF_V_DIM)
             - lam * _normalized(acc_scr, 1, DIFF_V_DIM))
        o = (o * _rms_scale(o, axis=0) * g_ref[...]) * (1.0 - lambda_init)
        o_ref[0, i * tile:(i + 1) * tile, :] = o.T.astype(o_ref.dtype)

    _causal_flash(n_tiles, tile, scratch,
                  lambda i: _masked_maps(qt_ref[0, i]),
                  keys_fn, values_fn, lambda i, j: bias_ref[0, i - j], finish)


def _fox_attn_kernel(qt_ref, qa_ref, k_ref, ka_ref, vt_ref, o_ref, *scratch):
    n_tiles, _, tile = qt_ref.shape[1:]
    ones = _ones_rows(tile)
    aug_zero = jnp.zeros((BF16_ROWS, tile), BF16)
    pad = jnp.zeros((MXU_DEPTH - LANES - 2 * BF16_ROWS, tile), BF16)

    def keys_fn(j):
        rows = pl.ds(pl.multiple_of(j * tile, tile), tile)
        return jnp.concatenate([k_ref[0, rows, :], ka_ref[0, rows, :]], axis=1)

    def values_fn(j, idx):
        rows = vt_ref[0, j, idx * FOX_HEAD_DIM:(idx + 1) * FOX_HEAD_DIM, :]
        return jnp.concatenate([rows, ones], axis=0)

    def finish(i, acc_scr):
        o = jnp.concatenate([_normalized(acc_scr, 0, FOX_HEAD_DIM),
                             _normalized(acc_scr, 1, FOX_HEAD_DIM)], axis=0)
        o_ref[0, i * tile:(i + 1) * tile, :] = o.T.astype(o_ref.dtype)

    def w_maps_fn(i):
        q_a, q_b = _masked_maps(qt_ref[0, i])
        cols = slice(i * tile, (i + 1) * tile)
        return (jnp.concatenate([q_a, qa_ref[0, 0, :, cols], aug_zero, pad], axis=0),
                jnp.concatenate([q_b, aug_zero, qa_ref[0, 1, :, cols], pad], axis=0))

    _causal_flash(n_tiles, tile, scratch, w_maps_fn, keys_fn, values_fn, lambda i, j: None, finish)


def _attn_scratch(tile, value_rows):
    stat = pltpu.VMEM((4, 1, tile), F32)
    return [stat, pltpu.VMEM((4, value_rows + BF16_ROWS, tile), F32),
            pltpu.VMEM((4, tile, tile + LANES), F32), pltpu.VMEM((4, tile, tile + LANES), BF16),
            stat, stat]


def _diff_attention(fm, dk, bias_tiles, lam_vecs, subln_g, lambda_init):
    bsz, n_tiles, _, tile = fm.shape
    s = n_tiles * tile
    width = dk.shape[2]
    n_groups = width // LANES
    kern = functools.partial(_diff_attn_kernel, lambda_init=lambda_init)
    return pl.pallas_call(
        kern,
        grid=(n_groups, bsz),
        in_specs=[pl.BlockSpec((1, n_tiles, LANES, tile), lambda g, b: (b, 0, g, 0)),
                  pl.BlockSpec((1, s, LANES), lambda g, b: (b, 0, g)),
                  pl.BlockSpec((1, n_tiles, LANES, tile), lambda g, b: (b, 0, n_groups + g, 0)),
                  pl.BlockSpec((1, n_tiles, tile, tile), lambda g, b: (g, 0, 0, 0)),
                  pl.BlockSpec(lam_vecs.shape, lambda g, b: (0, 0)),
                  pl.BlockSpec((LANES, 1), lambda g, b: (0, 0))],
        out_specs=pl.BlockSpec((1, s, LANES), lambda g, b: (b, 0, g)),
        out_shape=jax.ShapeDtypeStruct((bsz, s, width), BF16),
        scratch_shapes=_attn_scratch(tile, DIFF_V_DIM),
        compiler_params=_params(("arbitrary", "arbitrary")),
        name="diff_attention",
    )(fm, dk, fm, bias_tiles, lam_vecs, subln_g.reshape(LANES, 1))


def _fox_attention(fm, fk, q_aug, k_aug):
    bsz, n_tiles, _, tile = fm.shape
    s = n_tiles * tile
    width = fk.shape[2]
    n_groups = width // LANES
    return pl.pallas_call(
        _fox_attn_kernel,
        grid=(n_groups, bsz),
        in_specs=[pl.BlockSpec((1, n_tiles, LANES, tile), lambda g, b: (b, 0, 2 * n_groups + g, 0)),
                  pl.BlockSpec((1, 2, BF16_ROWS, s), lambda g, b: (b, g, 0, 0)),
                  pl.BlockSpec((1, s, LANES), lambda g, b: (b, 0, g)),
                  pl.BlockSpec((1, s, LANES), lambda g, b: (b * n_groups + g, 0, 0)),
                  pl.BlockSpec((1, n_tiles, LANES, tile), lambda g, b: (b, 0, 3 * n_groups + g, 0))],
        out_specs=pl.BlockSpec((1, s, LANES), lambda g, b: (b, 0, g)),
        out_shape=jax.ShapeDtypeStruct((bsz, s, width), BF16),
        scratch_shapes=_attn_scratch(tile, FOX_HEAD_DIM),
        compiler_params=_params(("arbitrary", "arbitrary")),
        name="fox_attention",
    )(fm, q_aug, fk, k_aug, fm)


def _out_kernel(d_ref, f_ref, w_ref, x_ref, mod_ref, g_ref, x1_ref, h_ref, *, width):
    mix = (jnp.dot(d_ref[0], w_ref[:width, :], preferred_element_type=F32)
           + jnp.dot(f_ref[0], w_ref[width:, :], preferred_element_type=F32))
    x1 = x_ref[0] + mod_ref[0, 2:3, :] * mix
    x1_ref[0] = x1
    h = (x1 * _rms_scale(x1) * g_ref[...]) * (1.0 + mod_ref[0, 4:5, :]) + mod_ref[0, 3:4, :]
    h_ref[0] = h.astype(BF16)


def _out_projection(d_out, f_out, w_out, x, mod3, g):
    bsz, s, d = x.shape
    width = d_out.shape[2]
    ts = ROW_TILE
    row_spec = pl.BlockSpec((1, ts, d), lambda b, i: (b, i, 0))
    in_spec = pl.BlockSpec((1, ts, width), lambda b, i: (b, i, 0))
    kern = functools.partial(_out_kernel, width=width)
    return pl.pallas_call(
        kern,
        grid=(bsz, s // ts),
        in_specs=[in_spec, in_spec,
                  pl.BlockSpec(w_out.shape, lambda b, i: (0, 0)),
                  row_spec,
                  pl.BlockSpec((1, N_MOD, d), lambda b, i: (b, 0, 0)),
                  pl.BlockSpec((1, d), lambda b, i: (0, 0))],
        out_specs=[row_spec, row_spec],
        out_shape=[jax.ShapeDtypeStruct((bsz, s, d), F32), jax.ShapeDtypeStruct((bsz, s, d), BF16)],
        compiler_params=_params(("arbitrary", "arbitrary")),
        name="out_projection",
    )(d_out, f_out, w_out, x, mod3, g.reshape(1, d))


def _ffn_kernel(h_ref, x1_ref, mod_ref, wu_ref, cw_ref, cb_ref, wd_ref, fg_ref, o_ref,
                tail_scr, u_scr, *, d_ff, chunk, final_norm):
    i = pl.program_id(1)
    ts = h_ref.shape[1]
    lane_tiles = chunk // LANES

    @pl.when(i == 0)
    def _():
        tail_scr[...] = jnp.zeros(tail_scr.shape, F32)

    def conv_cols(c0, slot):
        u = jnp.dot(h_ref[0], wu_ref[:, c0:c0 + chunk], preferred_element_type=F32)
        taps = []
        for t in range(lane_tiles):
            cols = slice(c0 + t * LANES, c0 + (t + 1) * LANES)
            buf = u_scr.at[slot * lane_tiles + t]
            buf[0:SUBLANES, :] = tail_scr[:, cols]
            buf[SUBLANES:, :] = u[:, t * LANES:(t + 1) * LANES]
            tail_scr[:, cols] = u[ts - SUBLANES:, t * LANES:(t + 1) * LANES]
            y = cb_ref[:, cols]
            for tap in range(CONV_WIDTH):
                lo = SUBLANES - (CONV_WIDTH - 1) + tap
                y = y + cw_ref[tap:tap + 1, cols] * buf[pl.ds(lo, ts), :]
            taps.append(y)
        return jnp.concatenate(taps, axis=1)

    acc = jnp.zeros((ts, o_ref.shape[2]), F32)
    for c in range(d_ff // chunk):
        gate = conv_cols(c * chunk, (2 * c) % FFN_SLOTS)
        val = conv_cols(d_ff + c * chunk, (2 * c + 1) % FFN_SLOTS)
        act = (gate * jax.nn.sigmoid(gate) * val).astype(BF16)
        acc = acc + jnp.dot(act, wd_ref[c * chunk:(c + 1) * chunk, :], preferred_element_type=F32)
    x2 = x1_ref[0] + mod_ref[0, 5:6, :] * acc
    if final_norm:
        x2 = x2 * _rms_scale(x2) * fg_ref[...]
    o_ref[0] = x2


def _ffn(h2, x1, mod3, w_up, conv_w, conv_b, w_down, final_g, final_norm):
    bsz, s, d = x1.shape
    d_ff = w_down.shape[0]
    ts = ROW_TILE
    row_spec = pl.BlockSpec((1, ts, d), lambda b, i: (b, i, 0))
    kern = functools.partial(_ffn_kernel, d_ff=d_ff, chunk=FFN_CHUNK, final_norm=final_norm)
    const = lambda b, i: (0, 0)
    return pl.pallas_call(
        kern,
        grid=(bsz, s // ts),
        in_specs=[row_spec, row_spec,
                  pl.BlockSpec((1, N_MOD, d), lambda b, i: (b, 0, 0)),
                  pl.BlockSpec(w_up.shape, const, pipeline_mode=pl.Buffered(1)),
                  pl.BlockSpec(conv_w.shape, const),
                  pl.BlockSpec((1, 2 * d_ff), const),
                  pl.BlockSpec(w_down.shape, const, pipeline_mode=pl.Buffered(1)),
                  pl.BlockSpec((1, d), const)],
        out_specs=row_spec,
        out_shape=jax.ShapeDtypeStruct((bsz, s, d), F32),
        scratch_shapes=[pltpu.VMEM((SUBLANES, 2 * d_ff), F32),
                        pltpu.VMEM((FFN_SLOTS * FFN_CHUNK // LANES, ts + SUBLANES, LANES), F32)],
        compiler_params=_params(("arbitrary", "arbitrary")),
        name="conv_ffn",
    )(h2, x1, mod3, w_up, conv_w, conv_b.reshape(1, 2 * d_ff), w_down, final_g.reshape(1, d))


def kernel(x, c, ada_w, ada_b, attn_norm_g, w_in, forget_b, lambda_q1, lambda_k1, lambda_q2,
           lambda_k2, subln_g, rel_bias, w_out, ffn_norm_g, w_up, conv_w, conv_b, w_down,
           final_norm_g):
    bsz, s, d = x.shape
    depth = ada_w.shape[0]
    n_fox_heads = forget_b.shape[1]
    width = (w_in.shape[2] - n_fox_heads) // 6
    assert width % LANES == 0 and subln_g.shape[1] == DIFF_V_DIM == LANES
    assert n_fox_heads * FOX_HEAD_DIM == width and s % ATTN_TILE == 0 and s % ROW_TILE == 0
    n_pairs = n_fox_heads // 2
    qk_scale = DIFF_QK_DIM ** -0.5 * LOG2E

    bias = _bias_tiles(rel_bias.T, s, ATTN_TILE)

    for l in range(depth):
        lambda_init = 0.8 - 0.6 * math.exp(-0.3 * l)
        mod3 = _modulation(c, ada_w[l], ada_b[l]).reshape(bsz, N_MOD, d)

        w = w_in[l]
        cols = lambda n: w[:, n * width:(n + 1) * width]
        w_fm_t = jnp.concatenate([cols(0) * qk_scale, cols(2), cols(3) * qk_scale, cols(5)],
                                 axis=1).T.astype(BF16)
        w_tm = jnp.concatenate([cols(1), cols(4),
                                jnp.pad(w[:, 6 * width:], ((0, 0), (0, LANES - n_fox_heads)))],
                               axis=1).astype(BF16)
        fm, dk, fk, fl = _in_projection(x, mod3, attn_norm_g[l], w_fm_t, w_tm,
                                        width=width, n_fox_heads=n_fox_heads)

        q_aug, k_aug_t = _forget_aug(jnp.transpose(fl, (0, 2, 1)), forget_b[l])
        k_aug = jnp.transpose(k_aug_t.reshape(bsz * n_pairs, 2 * BF16_ROWS, s), (0, 2, 1))
        k_aug = jnp.pad(k_aug, ((0, 0), (0, 0), (0, LANES - 2 * BF16_ROWS)))

        lam_vecs = jnp.stack([lambda_q1[l], lambda_k1[l], lambda_q2[l], lambda_k2[l]]).astype(F32)
        d_out = _diff_attention(fm, dk, bias, lam_vecs, subln_g[l], lambda_init)
        f_out = _fox_attention(fm, fk, q_aug, k_aug)

        x1, h2 = _out_projection(d_out, f_out, w_out[l].astype(BF16), x, mod3, ffn_norm_g[l])
        x = _ffn(h2, x1, mod3, w_up[l].astype(BF16), conv_w[l], conv_b[l],
                 w_down[l].astype(BF16), final_norm_g, final_norm=(l == depth - 1))
    return x
```

```python
import functools
import math

import jax
import jax.numpy as jnp
from jax import lax
from jax.experimental import pallas as pl
from jax.experimental.pallas import tpu as pltpu

F32 = jnp.float32
BF16 = jnp.bfloat16

DIFF_QK_DIM = 64
DIFF_V_DIM = 2 * DIFF_QK_DIM
FOX_HEAD_DIM = 64
CONV_WIDTH = 3
NUM_BUCKETS = 32
MAX_EXACT = NUM_BUCKETS // 2
MAX_DISTANCE = 128
N_MOD = 6
NORM_EPS = 1e-6
NEG_INF = -1e30
LOG2E = math.log2(math.e)

LANES = 128
SUBLANES = 8
BF16_ROWS = 16
MXU_DEPTH = 256
VMEM_LIMIT_BYTES = 56 * 1024 * 1024

ATTN_TILE = 512
ROW_TILE = 512
FFN_CHUNK = 256
FFN_SLOTS = 4
MOD_COL_TILE = 1536

NT_DIMS = (((1,), (1,)), ((), ()))


def _params(semantics):
    return pltpu.CompilerParams(dimension_semantics=semantics, vmem_limit_bytes=VMEM_LIMIT_BYTES)


def _rms_scale(x, axis=-1):
    return lax.rsqrt(jnp.mean(x * x, axis=axis, keepdims=True) + NORM_EPS)


def _mod_kernel(c_ref, w_ref, b_ref, o_ref):
    c = c_ref[...]
    act = c * jax.nn.sigmoid(c)
    o_ref[...] = jnp.dot(act.astype(BF16), w_ref[...].astype(BF16),
                         preferred_element_type=F32) + b_ref[...]


def _modulation(c, w, b):
    bsz, d = c.shape
    n = w.shape[1]
    tn = MOD_COL_TILE
    return pl.pallas_call(
        _mod_kernel,
        grid=(n // tn,),
        in_specs=[pl.BlockSpec((bsz, d), lambda j: (0, 0)),
                  pl.BlockSpec((d, tn), lambda j: (0, j)),
                  pl.BlockSpec((1, tn), lambda j: (0, j))],
        out_specs=pl.BlockSpec((bsz, tn), lambda j: (0, j)),
        out_shape=jax.ShapeDtypeStruct((bsz, n), F32),
        compiler_params=_params(("arbitrary",)),
        name="modulation",
    )(c, w, b.reshape(1, n))


def _proj_kernel(x_ref, mod_ref, g_ref, wt_ref, wk_ref, fm_ref, dk_ref, fk_ref, fl_ref,
                 *, width, n_fox_heads):
    x = x_ref[0]
    h = (x * _rms_scale(x) * g_ref[...]) * (1.0 + mod_ref[0, 1:2, :]) + mod_ref[0, 0:1, :]
    hb = h.astype(BF16)
    fm_ref[0, 0] = lax.dot_general(wt_ref[...], hb, NT_DIMS,
                                   preferred_element_type=F32).astype(BF16)
    r = jnp.dot(hb, wk_ref[...], preferred_element_type=F32)
    dk_ref[0] = r[:, :width].astype(BF16)
    fk_ref[0] = r[:, width:2 * width].astype(BF16)
    fl_ref[0] = r[:, 2 * width:2 * width + n_fox_heads]


def _in_projection(x, mod3, g, w_fm_t, w_tm, *, width, n_fox_heads):
    bsz, s, d = x.shape
    ts = ATTN_TILE
    k_shape = jax.ShapeDtypeStruct((bsz, s, width), BF16)
    k_spec = pl.BlockSpec((1, ts, width), lambda b, i: (b, i, 0))
    kern = functools.partial(_proj_kernel, width=width, n_fox_heads=n_fox_heads)
    return pl.pallas_call(
        kern,
        grid=(bsz, s // ts),
        in_specs=[pl.BlockSpec((1, ts, d), lambda b, i: (b, i, 0)),
                  pl.BlockSpec((1, N_MOD, d), lambda b, i: (b, 0, 0)),
                  pl.BlockSpec((1, d), lambda b, i: (0, 0)),
                  pl.BlockSpec(w_fm_t.shape, lambda b, i: (0, 0)),
                  pl.BlockSpec(w_tm.shape, lambda b, i: (0, 0))],
        out_specs=[pl.BlockSpec((1, 1, 4 * width, ts), lambda b, i: (b, i, 0, 0)),
                   k_spec, k_spec,
                   pl.BlockSpec((1, ts, n_fox_heads), lambda b, i: (b, i, 0))],
        out_shape=[jax.ShapeDtypeStruct((bsz, s // ts, 4 * width, ts), BF16),
                   k_shape, k_shape,
                   jax.ShapeDtypeStruct((bsz, s, n_fox_heads), F32)],
        compiler_params=_params(("arbitrary", "arbitrary")),
        name="in_projection",
    )(x, mod3, g.reshape(1, d), w_fm_t, w_tm)


def _split3(x):
    hi = x.astype(BF16)
    r1 = x - hi.astype(F32)
    mid = r1.astype(BF16)
    lo = (r1 - mid.astype(F32)).astype(BF16)
    return hi, mid, lo


def _cum_kernel(fl_ref, fb_ref, qa_ref, ka_ref):
    z = fl_ref[0] + fb_ref[...]
    acc = jnp.minimum(z, 0.0) - jnp.log1p(jnp.exp(-jnp.abs(z)))
    n_heads, s = acc.shape
    pos = lax.broadcasted_iota(jnp.int32, acc.shape, 1)
    shift = 1
    while shift < s:
        acc = acc + jnp.where(pos >= shift, pltpu.roll(acc, shift, 1), 0.0)
        shift *= 2
    acc = acc * LOG2E
    q_parts = _split3(acc)
    k_parts = _split3(-acc)
    ones = jnp.ones((3, s), BF16)
    zeros = jnp.zeros((BF16_ROWS - 6, s), BF16)
    for h in range(n_heads):
        qa_ref[0, h] = jnp.concatenate([p[h:h + 1] for p in q_parts] + [ones, zeros], axis=0)
        ka_ref[0, h] = jnp.concatenate([ones] + [p[h:h + 1] for p in k_parts] + [zeros], axis=0)


def _forget_aug(fl_t, forget_b):
    bsz, h, s = fl_t.shape
    aug_shape = jax.ShapeDtypeStruct((bsz, h, BF16_ROWS, s), BF16)
    aug_spec = pl.BlockSpec((1, h, BF16_ROWS, s), lambda b: (b, 0, 0, 0))
    return pl.pallas_call(
        _cum_kernel,
        grid=(bsz,),
        in_specs=[pl.BlockSpec((1, h, s), lambda b: (b, 0, 0)),
                  pl.BlockSpec((h, 1), lambda b: (0, 0))],
        out_specs=[aug_spec, aug_spec],
        out_shape=[aug_shape, aug_shape],
        compiler_params=_params(("arbitrary",)),
        name="forget_cumsum",
    )(fl_t, forget_b.reshape(h, 1))


def _bias_kernel(rb_ref, o_ref, *, tile, n_delta):
    h = pl.program_id(0)
    u = lax.broadcasted_iota(jnp.int32, (1, 2 * tile), 1)
    for dd in range(n_delta):
        n = jnp.maximum(dd * tile + u - tile, 0)
        nf = jnp.maximum(n, 1).astype(F32)
        large = MAX_EXACT + (jnp.log(nf / MAX_EXACT) / math.log(MAX_DISTANCE / MAX_EXACT)
                             * (NUM_BUCKETS - MAX_EXACT)).astype(jnp.int32)
        large = jnp.minimum(large, NUM_BUCKETS - 1)
        bucket = jnp.where(n < MAX_EXACT, n, large)
        row = jnp.zeros((1, 2 * tile), F32)
        for b in range(NUM_BUCKETS):
            row = jnp.where(bucket == b, rb_ref[h, b], row)
        row = row * LOG2E
        full = pltpu.roll(jnp.broadcast_to(row, (tile, 2 * tile)), 0, 1, stride=1, stride_axis=0)
        o_ref[0, dd] = full[:, tile:]


def _bias_tiles(rel_bias_t, seq, tile):
    n_heads = rel_bias_t.shape[0]
    n_delta = seq // tile
    kern = functools.partial(_bias_kernel, tile=tile, n_delta=n_delta)
    return pl.pallas_call(
        kern,
        grid=(n_heads,),
        in_specs=[pl.BlockSpec(memory_space=pltpu.SMEM)],
        out_specs=pl.BlockSpec((1, n_delta, tile, tile), lambda h: (h, 0, 0, 0)),
        out_shape=jax.ShapeDtypeStruct((n_heads, n_delta, tile, tile), F32),
        compiler_params=_params(("arbitrary",)),
        name="bias_tiles",
    )(rel_bias_t)


def _causal_flash(n_tiles, tile, scratch, w_maps_fn, keys_fn, values_fn, bias_fn, finish_fn):
    views = [[r.at[pl.ds(2 * parity, 2)] for r in scratch] for parity in range(2)]

    def start(i):
        m_scr, acc_scr = views[i % 2][:2]
        m_scr[...] = jnp.full(m_scr.shape, NEG_INF, F32)
        acc_scr[...] = jnp.zeros(acc_scr.shape, F32)

    def scores(i, w_maps, j):
        s_scr, mt_scr = views[i % 2][2], views[i % 2][4]
        keys = keys_fn(j)
        bias = bias_fn(i, j)
        for idx in range(2):
            st = jnp.dot(keys, w_maps[idx], preferred_element_type=F32)
            if bias is not None:
                st = st + bias
            s_scr[idx] = st
            mt_scr[idx] = jnp.max(st, axis=0, keepdims=True)

    def softmax(i, diag):
        m_scr, _, s_scr, p_scr, mt_scr, al_scr = views[i % 2]
        if diag:
            key_pos = lax.broadcasted_iota(jnp.int32, (tile, tile), 0)
            qry_pos = lax.broadcasted_iota(jnp.int32, (tile, tile), 1)
            causal = key_pos <= qry_pos
        for idx in range(2):
            st = s_scr[idx]
            if diag:
                st = jnp.where(causal, st, NEG_INF)
                m_tile = jnp.max(st, axis=0, keepdims=True)
            else:
                m_tile = mt_scr[idx]
            m_prev = m_scr[idx]
            m_new = jnp.maximum(m_prev, m_tile)
            p_scr[idx] = jnp.exp2(st - m_new).astype(BF16)
            al_scr[idx] = jnp.exp2(m_prev - m_new)
            m_scr[idx] = m_new

    def accumulate(i, j):
        _, acc_scr, _, p_scr, _, al_scr = views[i % 2]
        for idx in range(2):
            acc_scr[idx] = al_scr[idx] * acc_scr[idx] + jnp.dot(
                values_fn(j, idx), p_scr[idx], preferred_element_type=F32)

    w_maps = w_maps_fn(0)
    start(0)
    scores(0, w_maps, 0)
    for i in range(n_tiles):
        def body(j, carry, i=i, w_maps=w_maps):
            softmax(i, False)
            scores(i, w_maps, j + 1)
            accumulate(i, j)
            return carry

        if i > 0:
            lax.fori_loop(0, i, body, 0)
            finish_fn(i - 1, views[(i - 1) % 2][1])
        if i + 1 < n_tiles:
            w_maps = w_maps_fn(i + 1)
            start(i + 1)
            scores(i + 1, w_maps, 0)
        softmax(i, True)
        accumulate(i, i)
    finish_fn(n_tiles - 1, views[(n_tiles - 1) % 2][1])


def _masked_maps(qt):
    feat = lax.broadcasted_iota(jnp.int32, qt.shape, 0)
    zero = jnp.zeros_like(qt)
    return jnp.where(feat < DIFF_QK_DIM, qt, zero), jnp.where(feat >= DIFF_QK_DIM, qt, zero)


def _ones_rows(tile):
    row = lax.broadcasted_iota(jnp.int32, (BF16_ROWS, tile), 0)
    return jnp.where(row == 0, 1.0, 0.0).astype(BF16)


def _normalized(acc_ref, idx, rows):
    return acc_ref[idx, :rows, :] / acc_ref[idx, rows:rows + 1, :]


def _diff_attn_kernel(qt_ref, k_ref, vt_ref, bias_ref, lam_ref, g_ref, o_ref, *scratch,
                      lambda_init):
    n_tiles, _, tile = qt_ref.shape[1:]
    ones = _ones_rows(tile)
    lam = (jnp.exp(jnp.sum(lam_ref[0:1, :] * lam_ref[1:2, :], axis=-1, keepdims=True))
           - jnp.exp(jnp.sum(lam_ref[2:3, :] * lam_ref[3:4, :], axis=-1, keepdims=True))
           + lambda_init)

    def keys_fn(j):
        return k_ref[0, pl.ds(pl.multiple_of(j * tile, tile), tile), :]

    def values_fn(j, idx):
        return jnp.concatenate([vt_ref[0, j], ones], axis=0)

    def finish(i, acc_scr):
        o = (_normalized(acc_scr, 0, DIFF_V_DIM)
             - lam * _normalized(acc_scr, 1, DIFF_V_DIM))
        o = (o * _rms_scale(o, axis=0) * g_ref[...]) * (1.0 - lambda_init)
        o_ref[0, i * tile:(i + 1) * tile, :] = o.T.astype(o_ref.dtype)

    _causal_flash(n_tiles, tile, scratch,
                  lambda i: _masked_maps(qt_ref[0, i]),
                  keys_fn, values_fn, lambda i, j: bias_ref[0, i - j], finish)


def _fox_attn_kernel(qt_ref, qa_ref, k_ref, ka_ref, vt_ref, o_ref, *scratch):
    n_tiles, _, tile = qt_ref.shape[1:]
    ones = _ones_rows(tile)
    aug_zero = jnp.zeros((BF16_ROWS, tile), BF16)
    pad = jnp.zeros((MXU_DEPTH - LANES - 2 * BF16_ROWS, tile), BF16)

    def keys_fn(j):
        rows = pl.ds(pl.multiple_of(j * tile, tile), tile)
        return jnp.concatenate([k_ref[0, rows, :], ka_ref[0, rows, :]], axis=1)

    def values_fn(j, idx):
        rows = vt_ref[0, j, idx * FOX_HEAD_DIM:(idx + 1) * FOX_HEAD_DIM, :]
        return jnp.concatenate([rows, ones], axis=0)

    def finish(i, acc_scr):
        o = jnp.concatenate([_normalized(acc_scr, 0, FOX_HEAD_DIM),
                             _normalized(acc_scr, 1, FOX_HEAD_DIM)], axis=0)
        o_ref[0, i * tile:(i + 1) * tile, :] = o.T.astype(o_ref.dtype)

    def w_maps_fn(i):
        q_a, q_b = _masked_maps(qt_ref[0, i])
        cols = slice(i * tile, (i + 1) * tile)
        return (jnp.concatenate([q_a, qa_ref[0, 0, :, cols], aug_zero, pad], axis=0),
                jnp.concatenate([q_b, aug_zero, qa_ref[0, 1, :, cols], pad], axis=0))

    _causal_flash(n_tiles, tile, scratch, w_maps_fn, keys_fn, values_fn, lambda i, j: None, finish)


def _attn_scratch(tile, value_rows):
    stat = pltpu.VMEM((4, 1, tile), F32)
    return [stat, pltpu.VMEM((4, value_rows + BF16_ROWS, tile), F32),
            pltpu.VMEM((4, tile, tile), F32), pltpu.VMEM((4, tile, tile), BF16), stat, stat]


def _diff_attention(fm, dk, bias_tiles, lam_vecs, subln_g, lambda_init):
    bsz, n_tiles, _, tile = fm.shape
    s = n_tiles * tile
    width = dk.shape[2]
    n_groups = width // LANES
    kern = functools.partial(_diff_attn_kernel, lambda_init=lambda_init)
    return pl.pallas_call(
        kern,
        grid=(n_groups, bsz),
        in_specs=[pl.BlockSpec((1, n_tiles, LANES, tile), lambda g, b: (b, 0, g, 0)),
                  pl.BlockSpec((1, s, LANES), lambda g, b: (b, 0, g)),
                  pl.BlockSpec((1, n_tiles, LANES, tile), lambda g, b: (b, 0, n_groups + g, 0)),
                  pl.BlockSpec((1, n_tiles, tile, tile), lambda g, b: (g, 0, 0, 0)),
                  pl.BlockSpec(lam_vecs.shape, lambda g, b: (0, 0)),
                  pl.BlockSpec((LANES, 1), lambda g, b: (0, 0))],
        out_specs=pl.BlockSpec((1, s, LANES), lambda g, b: (b, 0, g)),
        out_shape=jax.ShapeDtypeStruct((bsz, s, width), BF16),
        scratch_shapes=_attn_scratch(tile, DIFF_V_DIM),
        compiler_params=_params(("arbitrary", "arbitrary")),
        name="diff_attention",
    )(fm, dk, fm, bias_tiles, lam_vecs, subln_g.reshape(LANES, 1))


def _fox_attention(fm, fk, q_aug, k_aug):
    bsz, n_tiles, _, tile = fm.shape
    s = n_tiles * tile
    width = fk.shape[2]
    n_groups = width // LANES
    return pl.pallas_call(
        _fox_attn_kernel,
        grid=(n_groups, bsz),
        in_specs=[pl.BlockSpec((1, n_tiles, LANES, tile), lambda g, b: (b, 0, 2 * n_groups + g, 0)),
                  pl.BlockSpec((1, 2, BF16_ROWS, s), lambda g, b: (b, g, 0, 0)),
                  pl.BlockSpec((1, s, LANES), lambda g, b: (b, 0, g)),
                  pl.BlockSpec((1, s, LANES), lambda g, b: (b * n_groups + g, 0, 0)),
                  pl.BlockSpec((1, n_tiles, LANES, tile), lambda g, b: (b, 0, 3 * n_groups + g, 0))],
        out_specs=pl.BlockSpec((1, s, LANES), lambda g, b: (b, 0, g)),
        out_shape=jax.ShapeDtypeStruct((bsz, s, width), BF16),
        scratch_shapes=_attn_scratch(tile, FOX_HEAD_DIM),
        compiler_params=_params(("arbitrary", "arbitrary")),
        name="fox_attention",
    )(fm, q_aug, fk, k_aug, fm)


def _ffn_kernel(d_ref, f_ref, x_ref, mod_ref, wo_ref, g_ref, wu_ref, cw_ref, cb_ref, wd_ref, fg_ref,
                o_ref, tail_scr, u_scr, x1_scr, h_scr, *, width, d_ff, chunk, final_norm):
    i = pl.program_id(1)
    ts = x_ref.shape[1]
    lane_tiles = chunk // LANES

    @pl.when(i == 0)
    def _():
        tail_scr[...] = jnp.zeros(tail_scr.shape, F32)

    mix = (jnp.dot(d_ref[0], wo_ref[:width, :], preferred_element_type=F32)
           + jnp.dot(f_ref[0], wo_ref[width:, :], preferred_element_type=F32))
    x1 = x_ref[0] + mod_ref[0, 2:3, :] * mix
    x1_scr[...] = x1
    h = (x1 * _rms_scale(x1) * g_ref[...]) * (1.0 + mod_ref[0, 4:5, :]) + mod_ref[0, 3:4, :]
    h_scr[...] = h.astype(BF16)

    def conv_cols(c0, slot):
        u = jnp.dot(h_scr[...], wu_ref[:, c0:c0 + chunk], preferred_element_type=F32)
        taps = []
        for t in range(lane_tiles):
            cols = slice(c0 + t * LANES, c0 + (t + 1) * LANES)
            buf = u_scr.at[slot * lane_tiles + t]
            buf[0:SUBLANES, :] = tail_scr[:, cols]
            buf[SUBLANES:, :] = u[:, t * LANES:(t + 1) * LANES]
            tail_scr[:, cols] = u[ts - SUBLANES:, t * LANES:(t + 1) * LANES]
            y = cb_ref[:, cols]
            for tap in range(CONV_WIDTH):
                lo = SUBLANES - (CONV_WIDTH - 1) + tap
                y = y + cw_ref[tap:tap + 1, cols] * buf[pl.ds(lo, ts), :]
            taps.append(y)
        return jnp.concatenate(taps, axis=1)

    acc = jnp.zeros((ts, o_ref.shape[2]), F32)
    for c in range(d_ff // chunk):
        gate = conv_cols(c * chunk, (2 * c) % FFN_SLOTS)
        val = conv_cols(d_ff + c * chunk, (2 * c + 1) % FFN_SLOTS)
        act = (gate * jax.nn.sigmoid(gate) * val).astype(BF16)
        acc = acc + jnp.dot(act, wd_ref[c * chunk:(c + 1) * chunk, :], preferred_element_type=F32)
    x2 = x1_scr[...] + mod_ref[0, 5:6, :] * acc
    if final_norm:
        x2 = x2 * _rms_scale(x2) * fg_ref[...]
    o_ref[0] = x2


def _out_ffn(d_out, f_out, x, mod3, w_out, ffn_g, w_up, conv_w, conv_b, w_down, final_g, final_norm):
    bsz, s, d = x.shape
    width = d_out.shape[2]
    d_ff = w_down.shape[0]
    ts = ROW_TILE
    row_spec = pl.BlockSpec((1, ts, d), lambda b, i: (b, i, 0))
    attn_spec = pl.BlockSpec((1, ts, width), lambda b, i: (b, i, 0))
    kern = functools.partial(_ffn_kernel, width=width, d_ff=d_ff, chunk=FFN_CHUNK,
                             final_norm=final_norm)
    const = lambda b, i: (0, 0)
    resident = functools.partial(pl.BlockSpec, index_map=const, pipeline_mode=pl.Buffered(1))
    return pl.pallas_call(
        kern,
        grid=(bsz, s // ts),
        in_specs=[attn_spec, attn_spec, row_spec,
                  pl.BlockSpec((1, N_MOD, d), lambda b, i: (b, 0, 0)),
                  resident(w_out.shape),
                  pl.BlockSpec((1, d), const),
                  resident(w_up.shape),
                  pl.BlockSpec(conv_w.shape, const),
                  pl.BlockSpec((1, 2 * d_ff), const),
                  resident(w_down.shape),
                  pl.BlockSpec((1, d), const)],
        out_specs=row_spec,
        out_shape=jax.ShapeDtypeStruct((bsz, s, d), F32),
        scratch_shapes=[pltpu.VMEM((SUBLANES, 2 * d_ff), F32),
                        pltpu.VMEM((FFN_SLOTS * FFN_CHUNK // LANES, ts + SUBLANES, LANES), F32),
                        pltpu.VMEM((ts, d), F32), pltpu.VMEM((ts, d), BF16)],
        compiler_params=_params(("arbitrary", "arbitrary")),
        name="out_ffn",
    )(d_out, f_out, x, mod3, w_out, ffn_g.reshape(1, d), w_up, conv_w,
      conv_b.reshape(1, 2 * d_ff), w_down, final_g.reshape(1, d))


def kernel(x, c, ada_w, ada_b, attn_norm_g, w_in, forget_b, lambda_q1, lambda_k1, lambda_q2,
           lambda_k2, subln_g, rel_bias, w_out, ffn_norm_g, w_up, conv_w, conv_b, w_down,
           final_norm_g):
    bsz, s, d = x.shape
    depth = ada_w.shape[0]
    n_fox_heads = forget_b.shape[1]
    width = (w_in.shape[2] - n_fox_heads) // 6
    assert width % LANES == 0 and subln_g.shape[1] == DIFF_V_DIM == LANES
    assert n_fox_heads * FOX_HEAD_DIM == width and s % ATTN_TILE == 0 and s % ROW_TILE == 0
    n_pairs = n_fox_heads // 2
    qk_scale = DIFF_QK_DIM ** -0.5 * LOG2E

    bias = _bias_tiles(rel_bias.T, s, ATTN_TILE)

    for l in range(depth):
        lambda_init = 0.8 - 0.6 * math.exp(-0.3 * l)
        mod3 = _modulation(c, ada_w[l], ada_b[l]).reshape(bsz, N_MOD, d)

        w = w_in[l]
        cols = lambda n: w[:, n * width:(n + 1) * width]
        w_fm_t = jnp.concatenate([cols(0) * qk_scale, cols(2), cols(3) * qk_scale, cols(5)],
                                 axis=1).T.astype(BF16)
        w_tm = jnp.concatenate([cols(1), cols(4),
                                jnp.pad(w[:, 6 * width:], ((0, 0), (0, LANES - n_fox_heads)))],
                               axis=1).astype(BF16)
        fm, dk, fk, fl = _in_projection(x, mod3, attn_norm_g[l], w_fm_t, w_tm,
                                        width=width, n_fox_heads=n_fox_heads)

        q_aug, k_aug_t = _forget_aug(jnp.transpose(fl, (0, 2, 1)), forget_b[l])
        k_aug = jnp.transpose(k_aug_t.reshape(bsz * n_pairs, 2 * BF16_ROWS, s), (0, 2, 1))
        k_aug = jnp.pad(k_aug, ((0, 0), (0, 0), (0, LANES - 2 * BF16_ROWS)))

        lam_vecs = jnp.stack([lambda_q1[l], lambda_k1[l], lambda_q2[l], lambda_k2[l]]).astype(F32)
        d_out = _diff_attention(fm, dk, bias, lam_vecs, subln_g[l], lambda_init)
        f_out = _fox_attention(fm, fk, q_aug, k_aug)

        x = _out_ffn(d_out, f_out, x, mod3, w_out[l].astype(BF16), ffn_norm_g[l],
                     w_up[l].astype(BF16), conv_w[l], conv_b[l], w_down[l].astype(BF16),
                     final_norm_g, final_norm=(l == depth - 1))
    return x
```

```python
import functools
import math

import jax
import jax.numpy as jnp
from jax import lax
from jax.experimental import pallas as pl
from jax.experimental.pallas import tpu as pltpu

F32 = jnp.float32
BF16 = jnp.bfloat16

DIFF_QK_DIM = 64
DIFF_V_DIM = 2 * DIFF_QK_DIM
FOX_HEAD_DIM = 64
CONV_WIDTH = 3
NUM_BUCKETS = 32
MAX_EXACT = NUM_BUCKETS // 2
MAX_DISTANCE = 128
N_MOD = 6
NORM_EPS = 1e-6
NEG_INF = -1e30
LOG2E = math.log2(math.e)

LANES = 128
SUBLANES = 8
BF16_ROWS = 16
MXU_DEPTH = 256
VMEM_LIMIT_BYTES = 56 * 1024 * 1024

ATTN_TILE = 512
ROW_TILE = 512
FFN_CHUNK = 256
FFN_SLOTS = 4
MOD_COL_TILE = 1536

NT_DIMS = (((1,), (1,)), ((), ()))


def _params(semantics):
    return pltpu.CompilerParams(dimension_semantics=semantics, vmem_limit_bytes=VMEM_LIMIT_BYTES)


def _rms_scale(x, axis=-1):
    return lax.rsqrt(jnp.mean(x * x, axis=axis, keepdims=True) + NORM_EPS)


def _mod_kernel(c_ref, w_ref, b_ref, o_ref):
    c = c_ref[...]
    act = c * jax.nn.sigmoid(c)
    o_ref[...] = jnp.dot(act.astype(BF16), w_ref[...].astype(BF16),
                         preferred_element_type=F32) + b_ref[...]


def _modulation(c, w, b):
    bsz, d = c.shape
    n = w.shape[1]
    tn = MOD_COL_TILE
    return pl.pallas_call(
        _mod_kernel,
        grid=(n // tn,),
        in_specs=[pl.BlockSpec((bsz, d), lambda j: (0, 0)),
                  pl.BlockSpec((d, tn), lambda j: (0, j)),
                  pl.BlockSpec((1, tn), lambda j: (0, j))],
        out_specs=pl.BlockSpec((bsz, tn), lambda j: (0, j)),
        out_shape=jax.ShapeDtypeStruct((bsz, n), F32),
        compiler_params=_params(("arbitrary",)),
        name="modulation",
    )(c, w, b.reshape(1, n))


def _proj_kernel(x_ref, mod_ref, g_ref, wt_ref, wk_ref, fm_ref, dk_ref, fk_ref, fl_ref,
                 *, width, n_fox_heads):
    x = x_ref[0]
    h = (x * _rms_scale(x) * g_ref[...]) * (1.0 + mod_ref[0, 1:2, :]) + mod_ref[0, 0:1, :]
    hb = h.astype(BF16)
    fm_ref[0, 0] = lax.dot_general(wt_ref[...], hb, NT_DIMS,
                                   preferred_element_type=F32).astype(BF16)
    r = jnp.dot(hb, wk_ref[...], preferred_element_type=F32)
    dk_ref[0] = r[:, :width].astype(BF16)
    fk_ref[0] = r[:, width:2 * width].astype(BF16)
    fl_ref[0] = r[:, 2 * width:2 * width + n_fox_heads]


def _in_projection(x, mod3, g, w_fm_t, w_tm, *, width, n_fox_heads):
    bsz, s, d = x.shape
    ts = ATTN_TILE
    k_shape = jax.ShapeDtypeStruct((bsz, s, width), BF16)
    k_spec = pl.BlockSpec((1, ts, width), lambda b, i: (b, i, 0))
    kern = functools.partial(_proj_kernel, width=width, n_fox_heads=n_fox_heads)
    return pl.pallas_call(
        kern,
        grid=(bsz, s // ts),
        in_specs=[pl.BlockSpec((1, ts, d), lambda b, i: (b, i, 0)),
                  pl.BlockSpec((1, N_MOD, d), lambda b, i: (b, 0, 0)),
                  pl.BlockSpec((1, d), lambda b, i: (0, 0)),
                  pl.BlockSpec(w_fm_t.shape, lambda b, i: (0, 0)),
                  pl.BlockSpec(w_tm.shape, lambda b, i: (0, 0))],
        out_specs=[pl.BlockSpec((1, 1, 4 * width, ts), lambda b, i: (b, i, 0, 0)),
                   k_spec, k_spec,
                   pl.BlockSpec((1, ts, n_fox_heads), lambda b, i: (b, i, 0))],
        out_shape=[jax.ShapeDtypeStruct((bsz, s // ts, 4 * width, ts), BF16),
                   k_shape, k_shape,
                   jax.ShapeDtypeStruct((bsz, s, n_fox_heads), F32)],
        compiler_params=_params(("arbitrary", "arbitrary")),
        name="in_projection",
    )(x, mod3, g.reshape(1, d), w_fm_t, w_tm)


def _split3(x):
    hi = x.astype(BF16)
    r1 = x - hi.astype(F32)
    mid = r1.astype(BF16)
    lo = (r1 - mid.astype(F32)).astype(BF16)
    return hi, mid, lo


def _cum_kernel(fl_ref, fb_ref, qa_ref, ka_ref):
    z = fl_ref[0] + fb_ref[...]
    acc = jnp.minimum(z, 0.0) - jnp.log1p(jnp.exp(-jnp.abs(z)))
    n_heads, s = acc.shape
    pos = lax.broadcasted_iota(jnp.int32, acc.shape, 1)
    shift = 1
    while shift < s:
        acc = acc + jnp.where(pos >= shift, pltpu.roll(acc, shift, 1), 0.0)
        shift *= 2
    acc = acc * LOG2E
    q_parts = _split3(acc)
    k_parts = _split3(-acc)
    ones = jnp.ones((3, s), BF16)
    zeros = jnp.zeros((BF16_ROWS - 6, s), BF16)
    for h in range(n_heads):
        qa_ref[0, h] = jnp.concatenate([p[h:h + 1] for p in q_parts] + [ones, zeros], axis=0)
        ka_ref[0, h] = jnp.concatenate([ones] + [p[h:h + 1] for p in k_parts] + [zeros], axis=0)


def _forget_aug(fl_t, forget_b):
    bsz, h, s = fl_t.shape
    aug_shape = jax.ShapeDtypeStruct((bsz, h, BF16_ROWS, s), BF16)
    aug_spec = pl.BlockSpec((1, h, BF16_ROWS, s), lambda b: (b, 0, 0, 0))
    return pl.pallas_call(
        _cum_kernel,
        grid=(bsz,),
        in_specs=[pl.BlockSpec((1, h, s), lambda b: (b, 0, 0)),
                  pl.BlockSpec((h, 1), lambda b: (0, 0))],
        out_specs=[aug_spec, aug_spec],
        out_shape=[aug_shape, aug_shape],
        compiler_params=_params(("arbitrary",)),
        name="forget_cumsum",
    )(fl_t, forget_b.reshape(h, 1))


def _bias_kernel(rb_ref, o_ref, *, tile, n_delta):
    h = pl.program_id(0)
    u = lax.broadcasted_iota(jnp.int32, (1, 2 * tile), 1)
    for dd in range(n_delta):
        n = jnp.maximum(dd * tile + u - tile, 0)
        nf = jnp.maximum(n, 1).astype(F32)
        large = MAX_EXACT + (jnp.log(nf / MAX_EXACT) / math.log(MAX_DISTANCE / MAX_EXACT)
                             * (NUM_BUCKETS - MAX_EXACT)).astype(jnp.int32)
        large = jnp.minimum(large, NUM_BUCKETS - 1)
        bucket = jnp.where(n < MAX_EXACT, n, large)
        row = jnp.zeros((1, 2 * tile), F32)
        for b in range(NUM_BUCKETS):
            row = jnp.where(bucket == b, rb_ref[h, b], row)
        row = row * LOG2E
        full = pltpu.roll(jnp.broadcast_to(row, (tile, 2 * tile)), 0, 1, stride=1, stride_axis=0)
        o_ref[0, dd] = full[:, tile:]


def _bias_tiles(rel_bias_t, seq, tile):
    n_heads = rel_bias_t.shape[0]
    n_delta = seq // tile
    kern = functools.partial(_bias_kernel, tile=tile, n_delta=n_delta)
    return pl.pallas_call(
        kern,
        grid=(n_heads,),
        in_specs=[pl.BlockSpec(memory_space=pltpu.SMEM)],
        out_specs=pl.BlockSpec((1, n_delta, tile, tile), lambda h: (h, 0, 0, 0)),
        out_shape=jax.ShapeDtypeStruct((n_heads, n_delta, tile, tile), F32),
        compiler_params=_params(("arbitrary",)),
        name="bias_tiles",
    )(rel_bias_t)


def _causal_flash(n_tiles, tile, scratch, w_maps_fn, keys_fn, values_fn, bias_fn, finish_fn):
    views = [[r.at[pl.ds(2 * parity, 2)] for r in scratch] for parity in range(2)]

    def start(i):
        m_scr, acc_scr = views[i % 2][:2]
        m_scr[...] = jnp.full(m_scr.shape, NEG_INF, F32)
        acc_scr[...] = jnp.zeros(acc_scr.shape, F32)

    def scores(i, w_maps, j):
        s_scr, mt_scr = views[i % 2][2], views[i % 2][4]
        keys = keys_fn(j)
        bias = bias_fn(i, j)
        for idx in range(2):
            st = jnp.dot(keys, w_maps[idx], preferred_element_type=F32)
            if bias is not None:
                st = st + bias
            s_scr[idx] = st
            mt_scr[idx] = jnp.max(st, axis=0, keepdims=True)

    def softmax(i):
        m_scr, _, s_scr, p_scr, mt_scr, al_scr = views[i % 2]
        for idx in range(2):
            m_prev = m_scr[idx]
            m_new = jnp.maximum(m_prev, mt_scr[idx])
            p_scr[idx] = jnp.exp2(s_scr[idx] - m_new).astype(BF16)
            al_scr[idx] = jnp.exp2(m_prev - m_new)
            m_scr[idx] = m_new

    def accumulate(i, j):
        _, acc_scr, _, p_scr, _, al_scr = views[i % 2]
        for idx in range(2):
            acc_scr[idx] = al_scr[idx] * acc_scr[idx] + jnp.dot(
                values_fn(j, idx), p_scr[idx], preferred_element_type=F32)

    half = tile // 2
    diag_parts = ((0, half), (half, tile))

    def softmax_diag(i):
        m_scr, _, s_scr, p_scr, _, al_scr = views[i % 2]
        for q0, n_keys in diag_parts:
            lanes = slice(q0, q0 + half)
            key_pos = lax.broadcasted_iota(jnp.int32, (n_keys, half), 0)
            qry_pos = lax.broadcasted_iota(jnp.int32, (n_keys, half), 1) + q0
            causal = key_pos <= qry_pos
            for idx in range(2):
                st = jnp.where(causal, s_scr[idx, :n_keys, lanes], NEG_INF)
                m_prev = m_scr[idx, :, lanes]
                m_new = jnp.maximum(m_prev, jnp.max(st, axis=0, keepdims=True))
                p_scr[idx, :n_keys, lanes] = jnp.exp2(st - m_new).astype(BF16)
                al_scr[idx, :, lanes] = jnp.exp2(m_prev - m_new)
                m_scr[idx, :, lanes] = m_new

    def accumulate_diag(i):
        _, acc_scr, _, p_scr, _, al_scr = views[i % 2]
        for q0, n_keys in diag_parts:
            lanes = slice(q0, q0 + half)
            for idx in range(2):
                acc_scr[idx, :, lanes] = al_scr[idx, :, lanes] * acc_scr[idx, :, lanes] + jnp.dot(
                    values_fn(i, idx)[:, :n_keys], p_scr[idx, :n_keys, lanes],
                    preferred_element_type=F32)

    w_maps = w_maps_fn(0)
    start(0)
    scores(0, w_maps, 0)
    for i in range(n_tiles):
        def body(j, carry, i=i, w_maps=w_maps):
            softmax(i)
            scores(i, w_maps, j + 1)
            accumulate(i, j)
            return carry

        if i > 0:
            lax.fori_loop(0, i, body, 0)
            finish_fn(i - 1, views[(i - 1) % 2][1])
        if i + 1 < n_tiles:
            w_maps = w_maps_fn(i + 1)
            start(i + 1)
            scores(i + 1, w_maps, 0)
        softmax_diag(i)
        accumulate_diag(i)
    finish_fn(n_tiles - 1, views[(n_tiles - 1) % 2][1])


def _masked_maps(qt):
    feat = lax.broadcasted_iota(jnp.int32, qt.shape, 0)
    zero = jnp.zeros_like(qt)
    return jnp.where(feat < DIFF_QK_DIM, qt, zero), jnp.where(feat >= DIFF_QK_DIM, qt, zero)


def _ones_rows(tile):
    row = lax.broadcasted_iota(jnp.int32, (BF16_ROWS, tile), 0)
    return jnp.where(row == 0, 1.0, 0.0).astype(BF16)


def _normalized(acc_ref, idx, rows):
    return acc_ref[idx, :rows, :] / acc_ref[idx, rows:rows + 1, :]


def _diff_attn_kernel(qt_ref, k_ref, vt_ref, bias_ref, lam_ref, g_ref, o_ref, *scratch,
                      lambda_init):
    n_tiles, _, tile = qt_ref.shape[1:]
    ones = _ones_rows(tile)
    lam = (jnp.exp(jnp.sum(lam_ref[0:1, :] * lam_ref[1:2, :], axis=-1, keepdims=True))
           - jnp.exp(jnp.sum(lam_ref[2:3, :] * lam_ref[3:4, :], axis=-1, keepdims=True))
           + lambda_init)

    def keys_fn(j):
        return k_ref[0, pl.ds(pl.multiple_of(j * tile, tile), tile), :]

    def values_fn(j, idx):
        return jnp.concatenate([vt_ref[0, j], ones], axis=0)

    def finish(i, acc_scr):
        o = (_normalized(acc_scr, 0, DIFF_V_DIM)
             - lam * _normalized(acc_scr, 1, DIFF_V_DIM))
        o = (o * _rms_scale(o, axis=0) * g_ref[...]) * (1.0 - lambda_init)
        o_ref[0, i * tile:(i + 1) * tile, :] = o.T.astype(o_ref.dtype)

    _causal_flash(n_tiles, tile, scratch,
                  lambda i: _masked_maps(qt_ref[0, i]),
                  keys_fn, values_fn, lambda i, j: bias_ref[0, i - j], finish)


def _fox_attn_kernel(qt_ref, qa_ref, k_ref, ka_ref, vt_ref, o_ref, *scratch):
    n_tiles, _, tile = qt_ref.shape[1:]
    ones = _ones_rows(tile)
    aug_zero = jnp.zeros((BF16_ROWS, tile), BF16)
    pad = jnp.zeros((MXU_DEPTH - LANES - 2 * BF16_ROWS, tile), BF16)

    def keys_fn(j):
        rows = pl.ds(pl.multiple_of(j * tile, tile), tile)
        return jnp.concatenate([k_ref[0, rows, :], ka_ref[0, rows, :]], axis=1)

    def values_fn(j, idx):
        rows = vt_ref[0, j, idx * FOX_HEAD_DIM:(idx + 1) * FOX_HEAD_DIM, :]
        return jnp.concatenate([rows, ones], axis=0)

    def finish(i, acc_scr):
        o = jnp.concatenate([_normalized(acc_scr, 0, FOX_HEAD_DIM),
                             _normalized(acc_scr, 1, FOX_HEAD_DIM)], axis=0)
        o_ref[0, i * tile:(i + 1) * tile, :] = o.T.astype(o_ref.dtype)

    def w_maps_fn(i):
        q_a, q_b = _masked_maps(qt_ref[0, i])
        cols = slice(i * tile, (i + 1) * tile)
        return (jnp.concatenate([q_a, qa_ref[0, 0, :, cols], aug_zero, pad], axis=0),
                jnp.concatenate([q_b, aug_zero, qa_ref[0, 1, :, cols], pad], axis=0))

    _causal_flash(n_tiles, tile, scratch, w_maps_fn, keys_fn, values_fn, lambda i, j: None, finish)


def _attn_scratch(tile, value_rows):
    stat = pltpu.VMEM((4, 1, tile), F32)
    return [stat, pltpu.VMEM((4, value_rows + BF16_ROWS, tile), F32),
            pltpu.VMEM((4, tile, tile), F32), pltpu.VMEM((4, tile, tile), BF16), stat, stat]


def _diff_attention(fm, dk, bias_tiles, lam_vecs, subln_g, lambda_init):
    bsz, n_tiles, _, tile = fm.shape
    s = n_tiles * tile
    width = dk.shape[2]
    n_groups = width // LANES
    kern = functools.partial(_diff_attn_kernel, lambda_init=lambda_init)
    return pl.pallas_call(
        kern,
        grid=(n_groups, bsz),
        in_specs=[pl.BlockSpec((1, n_tiles, LANES, tile), lambda g, b: (b, 0, g, 0)),
                  pl.BlockSpec((1, s, LANES), lambda g, b: (b, 0, g)),
                  pl.BlockSpec((1, n_tiles, LANES, tile), lambda g, b: (b, 0, n_groups + g, 0)),
                  pl.BlockSpec((1, n_tiles, tile, tile), lambda g, b: (g, 0, 0, 0)),
                  pl.BlockSpec(lam_vecs.shape, lambda g, b: (0, 0)),
                  pl.BlockSpec((LANES, 1), lambda g, b: (0, 0))],
        out_specs=pl.BlockSpec((1, s, LANES), lambda g, b: (b, 0, g)),
        out_shape=jax.ShapeDtypeStruct((bsz, s, width), BF16),
        scratch_shapes=_attn_scratch(tile, DIFF_V_DIM),
        compiler_params=_params(("arbitrary", "arbitrary")),
        name="diff_attention",
    )(fm, dk, fm, bias_tiles, lam_vecs, subln_g.reshape(LANES, 1))


def _fox_attention(fm, fk, q_aug, k_aug):
    bsz, n_tiles, _, tile = fm.shape
    s = n_tiles * tile
    width = fk.shape[2]
    n_groups = width // LANES
    return pl.pallas_call(
        _fox_attn_kernel,
        grid=(n_groups, bsz),
        in_specs=[pl.BlockSpec((1, n_tiles, LANES, tile), lambda g, b: (b, 0, 2 * n_groups + g, 0)),
                  pl.BlockSpec((1, 2, BF16_ROWS, s), lambda g, b: (b, g, 0, 0)),
                  pl.BlockSpec((1, s, LANES), lambda g, b: (b, 0, g)),
                  pl.BlockSpec((1, s, LANES), lambda g, b: (b * n_groups + g, 0, 0)),
                  pl.BlockSpec((1, n_tiles, LANES, tile), lambda g, b: (b, 0, 3 * n_groups + g, 0))],
        out_specs=pl.BlockSpec((1, s, LANES), lambda g, b: (b, 0, g)),
        out_shape=jax.ShapeDtypeStruct((bsz, s, width), BF16),
        scratch_shapes=_attn_scratch(tile, FOX_HEAD_DIM),
        compiler_params=_params(("arbitrary", "arbitrary")),
        name="fox_attention",
    )(fm, q_aug, fk, k_aug, fm)


def _ffn_kernel(d_ref, f_ref, x_ref, mod_ref, wo_ref, g_ref, wu_ref, cw_ref, cb_ref, wd_ref, fg_ref,
                o_ref, tail_scr, u_scr, x1_scr, h_scr, *, width, d_ff, chunk, final_norm):
    i = pl.program_id(1)
    ts = x_ref.shape[1]
    lane_tiles = chunk // LANES

    @pl.when(i == 0)
    def _():
        tail_scr[...] = jnp.zeros(tail_scr.shape, F32)

    mix = (jnp.dot(d_ref[0], wo_ref[:width, :], preferred_element_type=F32)
           + jnp.dot(f_ref[0], wo_ref[width:, :], preferred_element_type=F32))
    x1 = x_ref[0] + mod_ref[0, 2:3, :] * mix
    x1_scr[...] = x1
    h = (x1 * _rms_scale(x1) * g_ref[...]) * (1.0 + mod_ref[0, 4:5, :]) + mod_ref[0, 3:4, :]
    h_scr[...] = h.astype(BF16)

    def conv_cols(c0, slot):
        u = jnp.dot(h_scr[...], wu_ref[:, c0:c0 + chunk], preferred_element_type=F32)
        taps = []
        for t in range(lane_tiles):
            cols = slice(c0 + t * LANES, c0 + (t + 1) * LANES)
            buf = u_scr.at[slot * lane_tiles + t]
            buf[0:SUBLANES, :] = tail_scr[:, cols]
            buf[SUBLANES:, :] = u[:, t * LANES:(t + 1) * LANES]
            tail_scr[:, cols] = u[ts - SUBLANES:, t * LANES:(t + 1) * LANES]
            y = cb_ref[:, cols]
            for tap in range(CONV_WIDTH):
                lo = SUBLANES - (CONV_WIDTH - 1) + tap
                y = y + cw_ref[tap:tap + 1, cols] * buf[pl.ds(lo, ts), :]
            taps.append(y)
        return jnp.concatenate(taps, axis=1)

    acc = jnp.zeros((ts, o_ref.shape[2]), F32)
    for c in range(d_ff // chunk):
        gate = conv_cols(c * chunk, (2 * c) % FFN_SLOTS)
        val = conv_cols(d_ff + c * chunk, (2 * c + 1) % FFN_SLOTS)
        act = (gate * jax.nn.sigmoid(gate) * val).astype(BF16)
        acc = acc + jnp.dot(act, wd_ref[c * chunk:(c + 1) * chunk, :], preferred_element_type=F32)
    x2 = x1_scr[...] + mod_ref[0, 5:6, :] * acc
    if final_norm:
        x2 = x2 * _rms_scale(x2) * fg_ref[...]
    o_ref[0] = x2


def _out_ffn(d_out, f_out, x, mod3, w_out, ffn_g, w_up, conv_w, conv_b, w_down, final_g, final_norm):
    bsz, s, d = x.shape
    width = d_out.shape[2]
    d_ff = w_down.shape[0]
    ts = ROW_TILE
    row_spec = pl.BlockSpec((1, ts, d), lambda b, i: (b, i, 0))
    attn_spec = pl.BlockSpec((1, ts, width), lambda b, i: (b, i, 0))
    kern = functools.partial(_ffn_kernel, width=width, d_ff=d_ff, chunk=FFN_CHUNK,
                             final_norm=final_norm)
    const = lambda b, i: (0, 0)
    resident = functools.partial(pl.BlockSpec, index_map=const, pipeline_mode=pl.Buffered(1))
    return pl.pallas_call(
        kern,
        grid=(bsz, s // ts),
        in_specs=[attn_spec, attn_spec, row_spec,
                  pl.BlockSpec((1, N_MOD, d), lambda b, i: (b, 0, 0)),
                  resident(w_out.shape),
                  pl.BlockSpec((1, d), const),
                  resident(w_up.shape),
                  pl.BlockSpec(conv_w.shape, const),
                  pl.BlockSpec((1, 2 * d_ff), const),
                  resident(w_down.shape),
                  pl.BlockSpec((1, d), const)],
        out_specs=row_spec,
        out_shape=jax.ShapeDtypeStruct((bsz, s, d), F32),
        scratch_shapes=[pltpu.VMEM((SUBLANES, 2 * d_ff), F32),
                        pltpu.VMEM((FFN_SLOTS * FFN_CHUNK // LANES, ts + SUBLANES, LANES), F32),
                        pltpu.VMEM((ts, d), F32), pltpu.VMEM((ts, d), BF16)],
        compiler_params=_params(("arbitrary", "arbitrary")),
        name="out_ffn",
    )(d_out, f_out, x, mod3, w_out, ffn_g.reshape(1, d), w_up, conv_w,
      conv_b.reshape(1, 2 * d_ff), w_down, final_g.reshape(1, d))


def kernel(x, c, ada_w, ada_b, attn_norm_g, w_in, forget_b, lambda_q1, lambda_k1, lambda_q2,
           lambda_k2, subln_g, rel_bias, w_out, ffn_norm_g, w_up, conv_w, conv_b, w_down,
           final_norm_g):
    bsz, s, d = x.shape
    depth = ada_w.shape[0]
    n_fox_heads = forget_b.shape[1]
    width = (w_in.shape[2] - n_fox_heads) // 6
    assert width % LANES == 0 and subln_g.shape[1] == DIFF_V_DIM == LANES
    assert n_fox_heads * FOX_HEAD_DIM == width and s % ATTN_TILE == 0 and s % ROW_TILE == 0
    n_pairs = n_fox_heads // 2
    qk_scale = DIFF_QK_DIM ** -0.5 * LOG2E

    bias = _bias_tiles(rel_bias.T, s, ATTN_TILE)

    for l in range(depth):
        lambda_init = 0.8 - 0.6 * math.exp(-0.3 * l)
        mod3 = _modulation(c, ada_w[l], ada_b[l]).reshape(bsz, N_MOD, d)

        w = w_in[l]
        cols = lambda n: w[:, n * width:(n + 1) * width]
        w_fm_t = jnp.concatenate([cols(0) * qk_scale, cols(2), cols(3) * qk_scale, cols(5)],
                                 axis=1).T.astype(BF16)
        w_tm = jnp.concatenate([cols(1), cols(4),
                                jnp.pad(w[:, 6 * width:], ((0, 0), (0, LANES - n_fox_heads)))],
                               axis=1).astype(BF16)
        fm, dk, fk, fl = _in_projection(x, mod3, attn_norm_g[l], w_fm_t, w_tm,
                                        width=width, n_fox_heads=n_fox_heads)

        q_aug, k_aug_t = _forget_aug(jnp.transpose(fl, (0, 2, 1)), forget_b[l])
        k_aug = jnp.transpose(k_aug_t.reshape(bsz * n_pairs, 2 * BF16_ROWS, s), (0, 2, 1))
        k_aug = jnp.pad(k_aug, ((0, 0), (0, 0), (0, LANES - 2 * BF16_ROWS)))

        lam_vecs = jnp.stack([lambda_q1[l], lambda_k1[l], lambda_q2[l], lambda_k2[l]]).astype(F32)
        d_out = _diff_attention(fm, dk, bias, lam_vecs, subln_g[l], lambda_init)
        f_out = _fox_attention(fm, fk, q_aug, k_aug)

        x = _out_ffn(d_out, f_out, x, mod3, w_out[l].astype(BF16), ffn_norm_g[l],
                     w_up[l].astype(BF16), conv_w[l], conv_b[l], w_down[l].astype(BF16),
                     final_norm_g, final_norm=(l == depth - 1))
    return x
```

```python
import functools
import math

import jax
import jax.numpy as jnp
from jax import lax
from jax.experimental import pallas as pl
from jax.experimental.pallas import tpu as pltpu

F32 = jnp.float32
BF16 = jnp.bfloat16

DIFF_QK_DIM = 64
DIFF_V_DIM = 2 * DIFF_QK_DIM
FOX_HEAD_DIM = 64
CONV_WIDTH = 3
NUM_BUCKETS = 32
MAX_EXACT = NUM_BUCKETS // 2
MAX_DISTANCE = 128
N_MOD = 6
NORM_EPS = 1e-6
NEG_INF = -1e30
LOG2E = math.log2(math.e)

LANES = 128
SUBLANES = 8
BF16_ROWS = 16
MXU_DEPTH = 256
VMEM_LIMIT_BYTES = 56 * 1024 * 1024

ATTN_TILE = 512
ATTN_SEQS = 2
ROW_TILE = 256
FFN_SEQS = 2
FFN_CHUNK = 256
MOD_COL_TILE = 1536

NT_DIMS = (((1,), (1,)), ((), ()))


def _params(semantics):
    return pltpu.CompilerParams(dimension_semantics=semantics, vmem_limit_bytes=VMEM_LIMIT_BYTES)


def _rms_scale(x, axis=-1):
    return lax.rsqrt(jnp.mean(x * x, axis=axis, keepdims=True) + NORM_EPS)


def _mod_kernel(c_ref, w_ref, b_ref, o_ref):
    c = c_ref[...]
    act = c * jax.nn.sigmoid(c)
    o_ref[...] = jnp.dot(act.astype(BF16), w_ref[...].astype(BF16),
                         preferred_element_type=F32) + b_ref[...]


def _modulation(c, w, b):
    bsz, d = c.shape
    n = w.shape[1]
    tn = MOD_COL_TILE
    return pl.pallas_call(
        _mod_kernel,
        grid=(n // tn,),
        in_specs=[pl.BlockSpec((bsz, d), lambda j: (0, 0)),
                  pl.BlockSpec((d, tn), lambda j: (0, j)),
                  pl.BlockSpec((1, tn), lambda j: (0, j))],
        out_specs=pl.BlockSpec((bsz, tn), lambda j: (0, j)),
        out_shape=jax.ShapeDtypeStruct((bsz, n), F32),
        compiler_params=_params(("arbitrary",)),
        name="modulation",
    )(c, w, b.reshape(1, n))


def _proj_kernel(x_ref, mod_ref, g_ref, wt_ref, wk_ref, fm_ref, dk_ref, fk_ref, fl_ref,
                 *, width, n_fox_heads):
    x = x_ref[0]
    h = (x * _rms_scale(x) * g_ref[...]) * (1.0 + mod_ref[0, 1:2, :]) + mod_ref[0, 0:1, :]
    hb = h.astype(BF16)
    fm_ref[0, 0] = lax.dot_general(wt_ref[...], hb, NT_DIMS,
                                   preferred_element_type=F32).astype(BF16)
    r = jnp.dot(hb, wk_ref[...], preferred_element_type=F32)
    dk_ref[0] = r[:, :width].astype(BF16)
    fk_ref[0] = r[:, width:2 * width].astype(BF16)
    fl_ref[0] = r[:, 2 * width:2 * width + n_fox_heads]


def _in_projection(x, mod3, g, w_fm_t, w_tm, *, width, n_fox_heads):
    bsz, s, d = x.shape
    ts = ATTN_TILE
    k_shape = jax.ShapeDtypeStruct((bsz, s, width), BF16)
    k_spec = pl.BlockSpec((1, ts, width), lambda b, i: (b, i, 0))
    kern = functools.partial(_proj_kernel, width=width, n_fox_heads=n_fox_heads)
    return pl.pallas_call(
        kern,
        grid=(bsz, s // ts),
        in_specs=[pl.BlockSpec((1, ts, d), lambda b, i: (b, i, 0)),
                  pl.BlockSpec((1, N_MOD, d), lambda b, i: (b, 0, 0)),
                  pl.BlockSpec((1, d), lambda b, i: (0, 0)),
                  pl.BlockSpec(w_fm_t.shape, lambda b, i: (0, 0)),
                  pl.BlockSpec(w_tm.shape, lambda b, i: (0, 0))],
        out_specs=[pl.BlockSpec((1, 1, 4 * width, ts), lambda b, i: (b, i, 0, 0)),
                   k_spec, k_spec,
                   pl.BlockSpec((1, ts, n_fox_heads), lambda b, i: (b, i, 0))],
        out_shape=[jax.ShapeDtypeStruct((bsz, s // ts, 4 * width, ts), BF16),
                   k_shape, k_shape,
                   jax.ShapeDtypeStruct((bsz, s, n_fox_heads), F32)],
        compiler_params=_params(("arbitrary", "arbitrary")),
        name="in_projection",
    )(x, mod3, g.reshape(1, d), w_fm_t, w_tm)


def _split3(x):
    hi = x.astype(BF16)
    r1 = x - hi.astype(F32)
    mid = r1.astype(BF16)
    lo = (r1 - mid.astype(F32)).astype(BF16)
    return hi, mid, lo


def _cum_kernel(fl_ref, fb_ref, qa_ref, ka_ref):
    z = fl_ref[0] + fb_ref[...]
    acc = jnp.minimum(z, 0.0) - jnp.log1p(jnp.exp(-jnp.abs(z)))
    n_heads, s = acc.shape
    pos = lax.broadcasted_iota(jnp.int32, acc.shape, 1)
    shift = 1
    while shift < s:
        acc = acc + jnp.where(pos >= shift, pltpu.roll(acc, shift, 1), 0.0)
        shift *= 2
    acc = acc * LOG2E
    q_parts = _split3(acc)
    k_parts = _split3(-acc)
    ones = jnp.ones((3, s), BF16)
    zeros = jnp.zeros((BF16_ROWS - 6, s), BF16)
    for h in range(n_heads):
        qa_ref[0, h] = jnp.concatenate([p[h:h + 1] for p in q_parts] + [ones, zeros], axis=0)
        ka_ref[0, h] = jnp.concatenate([ones] + [p[h:h + 1] for p in k_parts] + [zeros], axis=0)


def _forget_aug(fl_t, forget_b):
    bsz, h, s = fl_t.shape
    aug_shape = jax.ShapeDtypeStruct((bsz, h, BF16_ROWS, s), BF16)
    aug_spec = pl.BlockSpec((1, h, BF16_ROWS, s), lambda b: (b, 0, 0, 0))
    return pl.pallas_call(
        _cum_kernel,
        grid=(bsz,),
        in_specs=[pl.BlockSpec((1, h, s), lambda b: (b, 0, 0)),
                  pl.BlockSpec((h, 1), lambda b: (0, 0))],
        out_specs=[aug_spec, aug_spec],
        out_shape=[aug_shape, aug_shape],
        compiler_params=_params(("arbitrary",)),
        name="forget_cumsum",
    )(fl_t, forget_b.reshape(h, 1))


def _bias_kernel(rb_ref, o_ref, *, tile, n_delta):
    h = pl.program_id(0)
    u = lax.broadcasted_iota(jnp.int32, (1, 2 * tile), 1)
    for dd in range(n_delta):
        n = jnp.maximum(dd * tile + u - tile, 0)
        nf = jnp.maximum(n, 1).astype(F32)
        large = MAX_EXACT + (jnp.log(nf / MAX_EXACT) / math.log(MAX_DISTANCE / MAX_EXACT)
                             * (NUM_BUCKETS - MAX_EXACT)).astype(jnp.int32)
        large = jnp.minimum(large, NUM_BUCKETS - 1)
        bucket = jnp.where(n < MAX_EXACT, n, large)
        row = jnp.zeros((1, 2 * tile), F32)
        for b in range(NUM_BUCKETS):
            row = jnp.where(bucket == b, rb_ref[h, b], row)
        row = row * LOG2E
        full = pltpu.roll(jnp.broadcast_to(row, (tile, 2 * tile)), 0, 1, stride=1, stride_axis=0)
        o_ref[0, dd] = full[:, tile:]


def _bias_tiles(rel_bias_t, seq, tile):
    n_heads = rel_bias_t.shape[0]
    n_delta = seq // tile
    kern = functools.partial(_bias_kernel, tile=tile, n_delta=n_delta)
    return pl.pallas_call(
        kern,
        grid=(n_heads,),
        in_specs=[pl.BlockSpec(memory_space=pltpu.SMEM)],
        out_specs=pl.BlockSpec((1, n_delta, tile, tile), lambda h: (h, 0, 0, 0)),
        out_shape=jax.ShapeDtypeStruct((n_heads, n_delta, tile, tile), F32),
        compiler_params=_params(("arbitrary",)),
        name="bias_tiles",
    )(rel_bias_t)


def _causal_flash(n_tiles, tile, scratch, streams):
    views = [[[r.at[pl.ds(4 * s + 2 * parity, 2)] for r in scratch] for parity in range(2)]
             for s in range(len(streams))]

    def start(i):
        for seq in views:
            m_scr, acc_scr = seq[i % 2][:2]
            m_scr[...] = jnp.full(m_scr.shape, NEG_INF, F32)
            acc_scr[...] = jnp.zeros(acc_scr.shape, F32)

    def scores(i, w_maps, j):
        for seq, stream, w in zip(views, streams, w_maps):
            s_scr, mt_scr = seq[i % 2][2], seq[i % 2][4]
            keys = stream[1](j)
            bias = stream[3](i, j)
            for idx in range(2):
                st = jnp.dot(keys, w[idx], preferred_element_type=F32)
                if bias is not None:
                    st = st + bias
                s_scr[idx] = st
                mt_scr[idx] = jnp.max(st, axis=0, keepdims=True)

    def softmax(i):
        for seq in views:
            m_scr, _, s_scr, p_scr, mt_scr, al_scr = seq[i % 2]
            for idx in range(2):
                m_prev = m_scr[idx]
                m_new = jnp.maximum(m_prev, mt_scr[idx])
                p_scr[idx] = jnp.exp2(s_scr[idx] - m_new).astype(BF16)
                al_scr[idx] = jnp.exp2(m_prev - m_new)
                m_scr[idx] = m_new

    def accumulate(i, j):
        for seq, stream in zip(views, streams):
            _, acc_scr, _, p_scr, _, al_scr = seq[i % 2]
            for idx in range(2):
                acc_scr[idx] = al_scr[idx] * acc_scr[idx] + jnp.dot(
                    stream[2](j, idx), p_scr[idx], preferred_element_type=F32)

    half = tile // 2
    diag_parts = ((0, half), (half, tile))

    def softmax_diag(i):
        for q0, n_keys in diag_parts:
            lanes = slice(q0, q0 + half)
            key_pos = lax.broadcasted_iota(jnp.int32, (n_keys, half), 0)
            qry_pos = lax.broadcasted_iota(jnp.int32, (n_keys, half), 1) + q0
            causal = key_pos <= qry_pos
            for seq in views:
                m_scr, _, s_scr, p_scr, _, al_scr = seq[i % 2]
                for idx in range(2):
                    st = jnp.where(causal, s_scr[idx, :n_keys, lanes], NEG_INF)
                    m_prev = m_scr[idx, :, lanes]
                    m_new = jnp.maximum(m_prev, jnp.max(st, axis=0, keepdims=True))
                    p_scr[idx, :n_keys, lanes] = jnp.exp2(st - m_new).astype(BF16)
                    al_scr[idx, :, lanes] = jnp.exp2(m_prev - m_new)
                    m_scr[idx, :, lanes] = m_new

    def accumulate_diag(i):
        for q0, n_keys in diag_parts:
            lanes = slice(q0, q0 + half)
            for seq, stream in zip(views, streams):
                _, acc_scr, _, p_scr, _, al_scr = seq[i % 2]
                for idx in range(2):
                    acc_scr[idx, :, lanes] = (
                        al_scr[idx, :, lanes] * acc_scr[idx, :, lanes]
                        + jnp.dot(stream[2](i, idx)[:, :n_keys], p_scr[idx, :n_keys, lanes],
                                  preferred_element_type=F32))

    def finish(i):
        for seq, stream in zip(views, streams):
            stream[4](i, seq[i % 2][1])

    w_maps = [stream[0](0) for stream in streams]
    start(0)
    scores(0, w_maps, 0)
    for i in range(n_tiles):
        def body(j, carry, i=i, w_maps=w_maps):
            softmax(i)
            scores(i, w_maps, j + 1)
            accumulate(i, j)
            return carry

        if i > 0:
            lax.fori_loop(0, i, body, 0)
            finish(i - 1)
        if i + 1 < n_tiles:
            w_maps = [stream[0](i + 1) for stream in streams]
            start(i + 1)
            scores(i + 1, w_maps, 0)
        softmax_diag(i)
        accumulate_diag(i)
    finish(n_tiles - 1)


def _masked_maps(qt):
    feat = lax.broadcasted_iota(jnp.int32, qt.shape, 0)
    zero = jnp.zeros_like(qt)
    return jnp.where(feat < DIFF_QK_DIM, qt, zero), jnp.where(feat >= DIFF_QK_DIM, qt, zero)


def _ones_rows(tile):
    row = lax.broadcasted_iota(jnp.int32, (BF16_ROWS, tile), 0)
    return jnp.where(row == 0, 1.0, 0.0).astype(BF16)


def _normalized(acc_ref, idx, rows):
    return acc_ref[idx, :rows, :] / acc_ref[idx, rows:rows + 1, :]


def _diff_attn_kernel(qt_ref, k_ref, vt_ref, bias_ref, lam_ref, g_ref, o_ref, *scratch,
                      lambda_init):
    n_seqs, n_tiles, _, tile = qt_ref.shape
    ones = _ones_rows(tile)
    lam = (jnp.exp(jnp.sum(lam_ref[0:1, :] * lam_ref[1:2, :], axis=-1, keepdims=True))
           - jnp.exp(jnp.sum(lam_ref[2:3, :] * lam_ref[3:4, :], axis=-1, keepdims=True))
           + lambda_init)

    def stream(b):
        def keys_fn(j):
            return k_ref[b, pl.ds(pl.multiple_of(j * tile, tile), tile), :]

        def values_fn(j, idx):
            return jnp.concatenate([vt_ref[b, j], ones], axis=0)

        def finish(i, acc_scr):
            o = (_normalized(acc_scr, 0, DIFF_V_DIM)
                 - lam * _normalized(acc_scr, 1, DIFF_V_DIM))
            o = (o * _rms_scale(o, axis=0) * g_ref[...]) * (1.0 - lambda_init)
            o_ref[b, i * tile:(i + 1) * tile, :] = o.T.astype(o_ref.dtype)

        return (lambda i: _masked_maps(qt_ref[b, i]),
                keys_fn, values_fn, lambda i, j: bias_ref[0, i - j], finish)

    _causal_flash(n_tiles, tile, scratch, [stream(b) for b in range(n_seqs)])


def _fox_attn_kernel(qt_ref, qa_ref, k_ref, ka_ref, vt_ref, o_ref, *scratch):
    n_seqs, n_tiles, _, tile = qt_ref.shape
    ones = _ones_rows(tile)
    aug_zero = jnp.zeros((BF16_ROWS, tile), BF16)
    pad = jnp.zeros((MXU_DEPTH - LANES - 2 * BF16_ROWS, tile), BF16)

    def stream(b):
        def keys_fn(j):
            rows = pl.ds(pl.multiple_of(j * tile, tile), tile)
            return jnp.concatenate([k_ref[b, rows, :], ka_ref[b, 0, rows, :]], axis=1)

        def values_fn(j, idx):
            rows = vt_ref[b, j, idx * FOX_HEAD_DIM:(idx + 1) * FOX_HEAD_DIM, :]
            return jnp.concatenate([rows, ones], axis=0)

        def finish(i, acc_scr):
            o = jnp.concatenate([_normalized(acc_scr, 0, FOX_HEAD_DIM),
                                 _normalized(acc_scr, 1, FOX_HEAD_DIM)], axis=0)
            o_ref[b, i * tile:(i + 1) * tile, :] = o.T.astype(o_ref.dtype)

        def w_maps_fn(i):
            q_a, q_b = _masked_maps(qt_ref[b, i])
            cols = slice(i * tile, (i + 1) * tile)
            return (jnp.concatenate([q_a, qa_ref[b, 0, :, cols], aug_zero, pad], axis=0),
                    jnp.concatenate([q_b, aug_zero, qa_ref[b, 1, :, cols], pad], axis=0))

        return (w_maps_fn, keys_fn, values_fn, lambda i, j: None, finish)

    _causal_flash(n_tiles, tile, scratch, [stream(b) for b in range(n_seqs)])


def _attn_scratch(tile, value_rows):
    n = 4 * ATTN_SEQS
    stat = pltpu.VMEM((n, 1, tile), F32)
    return [stat, pltpu.VMEM((n, value_rows + BF16_ROWS, tile), F32),
            pltpu.VMEM((n, tile, tile), F32), pltpu.VMEM((n, tile, tile), BF16), stat, stat]


def _diff_attention(fm, dk, bias_tiles, lam_vecs, subln_g, lambda_init):
    bsz, n_tiles, _, tile = fm.shape
    s = n_tiles * tile
    width = dk.shape[2]
    n_groups = width // LANES
    nb = ATTN_SEQS
    kern = functools.partial(_diff_attn_kernel, lambda_init=lambda_init)
    return pl.pallas_call(
        kern,
        grid=(n_groups, bsz // nb),
        in_specs=[pl.BlockSpec((nb, n_tiles, LANES, tile), lambda g, b: (b, 0, g, 0)),
                  pl.BlockSpec((nb, s, LANES), lambda g, b: (b, 0, g)),
                  pl.BlockSpec((nb, n_tiles, LANES, tile), lambda g, b: (b, 0, n_groups + g, 0)),
                  pl.BlockSpec((1, n_tiles, tile, tile), lambda g, b: (g, 0, 0, 0),
                               pipeline_mode=pl.Buffered(1)),
                  pl.BlockSpec(lam_vecs.shape, lambda g, b: (0, 0)),
                  pl.BlockSpec((LANES, 1), lambda g, b: (0, 0))],
        out_specs=pl.BlockSpec((nb, s, LANES), lambda g, b: (b, 0, g)),
        out_shape=jax.ShapeDtypeStruct((bsz, s, width), BF16),
        scratch_shapes=_attn_scratch(tile, DIFF_V_DIM),
        compiler_params=_params(("arbitrary", "arbitrary")),
        name="diff_attention",
    )(fm, dk, fm, bias_tiles, lam_vecs, subln_g.reshape(LANES, 1))


def _fox_attention(fm, fk, q_aug, k_aug):
    bsz, n_tiles, _, tile = fm.shape
    s = n_tiles * tile
    width = fk.shape[2]
    n_groups = width // LANES
    nb = ATTN_SEQS
    return pl.pallas_call(
        _fox_attn_kernel,
        grid=(n_groups, bsz // nb),
        in_specs=[pl.BlockSpec((nb, n_tiles, LANES, tile), lambda g, b: (b, 0, 2 * n_groups + g, 0)),
                  pl.BlockSpec((nb, 2, BF16_ROWS, s), lambda g, b: (b, g, 0, 0)),
                  pl.BlockSpec((nb, s, LANES), lambda g, b: (b, 0, g)),
                  pl.BlockSpec((nb, 1, s, LANES), lambda g, b: (b, g, 0, 0)),
                  pl.BlockSpec((nb, n_tiles, LANES, tile), lambda g, b: (b, 0, 3 * n_groups + g, 0))],
        out_specs=pl.BlockSpec((nb, s, LANES), lambda g, b: (b, 0, g)),
        out_shape=jax.ShapeDtypeStruct((bsz, s, width), BF16),
        scratch_shapes=_attn_scratch(tile, FOX_HEAD_DIM),
        compiler_params=_params(("arbitrary", "arbitrary")),
        name="fox_attention",
    )(fm, q_aug, fk, k_aug, fm)


def _ffn_kernel(d_ref, f_ref, x_ref, mod_ref, wo_ref, g_ref, wu_ref, cw_ref, cb_ref, wd_ref, fg_ref,
                o_ref, tail_scr, u_scr, x1_scr, h_scr, *, width, d_ff, chunk, final_norm):
    i = pl.program_id(1)
    n_seqs, ts = x_ref.shape[:2]
    lane_tiles = chunk // LANES
    n_chunks = d_ff // chunk

    @pl.when(i == 0)
    def _():
        tail_scr[...] = jnp.zeros(tail_scr.shape, F32)

    def project(s):
        mix = (jnp.dot(d_ref[s], wo_ref[:width, :], preferred_element_type=F32)
               + jnp.dot(f_ref[s], wo_ref[width:, :], preferred_element_type=F32))
        x1 = x_ref[s] + mod_ref[s, 2:3, :] * mix
        x1_scr[s] = x1
        h = (x1 * _rms_scale(x1) * g_ref[...]) * (1.0 + mod_ref[s, 4:5, :]) + mod_ref[s, 3:4, :]
        h_scr[s] = h.astype(BF16)

    def buffers(s, c, half):
        base = ((s * 2 + c % 2) * 2 + half) * lane_tiles
        return [u_scr.at[base + t] for t in range(lane_tiles)]

    def up(s, c):
        for half in range(2):
            c0 = half * d_ff + c * chunk
            u = jnp.dot(h_scr[s], wu_ref[:, c0:c0 + chunk], preferred_element_type=F32)
            for t, buf in enumerate(buffers(s, c, half)):
                cols = slice(c0 + t * LANES, c0 + (t + 1) * LANES)
                buf[0:SUBLANES, :] = tail_scr[s, :, cols]
                buf[SUBLANES:, :] = u[:, t * LANES:(t + 1) * LANES]
                tail_scr[s, :, cols] = u[ts - SUBLANES:, t * LANES:(t + 1) * LANES]

    def conv(s, c, half):
        c0 = half * d_ff + c * chunk
        tiles = []
        for t, buf in enumerate(buffers(s, c, half)):
            cols = slice(c0 + t * LANES, c0 + (t + 1) * LANES)
            y = cb_ref[:, cols]
            for tap in range(CONV_WIDTH):
                lo = SUBLANES - (CONV_WIDTH - 1) + tap
                y = y + cw_ref[tap:tap + 1, cols] * buf[pl.ds(lo, ts), :]
            tiles.append(y)
        return jnp.concatenate(tiles, axis=1)

    acc = []
    for s in range(n_seqs):
        project(s)
        acc.append(jnp.zeros((ts, o_ref.shape[2]), F32))
    for s in range(n_seqs):
        up(s, 0)
    for c in range(n_chunks):
        for s in range(n_seqs):
            gate = conv(s, c, 0)
            act = (gate * jax.nn.sigmoid(gate) * conv(s, c, 1)).astype(BF16)
            acc[s] = acc[s] + jnp.dot(act, wd_ref[c * chunk:(c + 1) * chunk, :],
                                      preferred_element_type=F32)
            if c + 1 < n_chunks:
                up(s, c + 1)
    for s in range(n_seqs):
        x2 = x1_scr[s] + mod_ref[s, 5:6, :] * acc[s]
        if final_norm:
            x2 = x2 * _rms_scale(x2) * fg_ref[...]
        o_ref[s] = x2


def _out_ffn(d_out, f_out, x, mod3, w_out, ffn_g, w_up, conv_w, conv_b, w_down, final_g, final_norm):
    bsz, s, d = x.shape
    width = d_out.shape[2]
    d_ff = w_down.shape[0]
    ts = ROW_TILE
    nb = FFN_SEQS
    row_spec = pl.BlockSpec((nb, ts, d), lambda b, i: (b, i, 0))
    attn_spec = pl.BlockSpec((nb, ts, width), lambda b, i: (b, i, 0))
    kern = functools.partial(_ffn_kernel, width=width, d_ff=d_ff, chunk=FFN_CHUNK,
                             final_norm=final_norm)
    const = lambda b, i: (0, 0)
    resident = functools.partial(pl.BlockSpec, index_map=const, pipeline_mode=pl.Buffered(1))
    return pl.pallas_call(
        kern,
        grid=(bsz // nb, s // ts),
        in_specs=[attn_spec, attn_spec, row_spec,
                  pl.BlockSpec((nb, N_MOD, d), lambda b, i: (b, 0, 0)),
                  resident(w_out.shape),
                  pl.BlockSpec((1, d), const),
                  resident(w_up.shape),
                  pl.BlockSpec(conv_w.shape, const),
                  pl.BlockSpec((1, 2 * d_ff), const),
                  resident(w_down.shape),
                  pl.BlockSpec((1, d), const)],
        out_specs=row_spec,
        out_shape=jax.ShapeDtypeStruct((bsz, s, d), F32),
        scratch_shapes=[pltpu.VMEM((nb, SUBLANES, 2 * d_ff), F32),
                        pltpu.VMEM((nb * 4 * FFN_CHUNK // LANES, ts + SUBLANES, LANES), F32),
                        pltpu.VMEM((nb, ts, d), F32), pltpu.VMEM((nb, ts, d), BF16)],
        compiler_params=_params(("arbitrary", "arbitrary")),
        name="out_ffn",
    )(d_out, f_out, x, mod3, w_out, ffn_g.reshape(1, d), w_up, conv_w,
      conv_b.reshape(1, 2 * d_ff), w_down, final_g.reshape(1, d))


def kernel(x, c, ada_w, ada_b, attn_norm_g, w_in, forget_b, lambda_q1, lambda_k1, lambda_q2,
           lambda_k2, subln_g, rel_bias, w_out, ffn_norm_g, w_up, conv_w, conv_b, w_down,
           final_norm_g):
    bsz, s, d = x.shape
    depth = ada_w.shape[0]
    n_fox_heads = forget_b.shape[1]
    width = (w_in.shape[2] - n_fox_heads) // 6
    assert width % LANES == 0 and subln_g.shape[1] == DIFF_V_DIM == LANES
    assert n_fox_heads * FOX_HEAD_DIM == width and s % ATTN_TILE == 0 and s % ROW_TILE == 0
    assert bsz % ATTN_SEQS == 0 and bsz % FFN_SEQS == 0
    n_pairs = n_fox_heads // 2
    qk_scale = DIFF_QK_DIM ** -0.5 * LOG2E

    bias = _bias_tiles(rel_bias.T, s, ATTN_TILE)

    for l in range(depth):
        lambda_init = 0.8 - 0.6 * math.exp(-0.3 * l)
        mod3 = _modulation(c, ada_w[l], ada_b[l]).reshape(bsz, N_MOD, d)

        w = w_in[l]
        cols = lambda n: w[:, n * width:(n + 1) * width]
        w_fm_t = jnp.concatenate([cols(0) * qk_scale, cols(2), cols(3) * qk_scale, cols(5)],
                                 axis=1).T.astype(BF16)
        w_tm = jnp.concatenate([cols(1), cols(4),
                                jnp.pad(w[:, 6 * width:], ((0, 0), (0, LANES - n_fox_heads)))],
                               axis=1).astype(BF16)
        fm, dk, fk, fl = _in_projection(x, mod3, attn_norm_g[l], w_fm_t, w_tm,
                                        width=width, n_fox_heads=n_fox_heads)

        q_aug, k_aug_t = _forget_aug(jnp.transpose(fl, (0, 2, 1)), forget_b[l])
        k_aug = jnp.transpose(k_aug_t.reshape(bsz, n_pairs, 2 * BF16_ROWS, s), (0, 1, 3, 2))
        k_aug = jnp.pad(k_aug, ((0, 0),) * 3 + ((0, LANES - 2 * BF16_ROWS),))

        lam_vecs = jnp.stack([lambda_q1[l], lambda_k1[l], lambda_q2[l], lambda_k2[l]]).astype(F32)
        d_out = _diff_attention(fm, dk, bias, lam_vecs, subln_g[l], lambda_init)
        f_out = _fox_attention(fm, fk, q_aug, k_aug)

        x = _out_ffn(d_out, f_out, x, mod3, w_out[l].astype(BF16), ffn_norm_g[l],
                     w_up[l].astype(BF16), conv_w[l], conv_b[l], w_down[l].astype(BF16),
                     final_norm_g, final_norm=(l == depth - 1))
    return x
```

```python
import functools
import math

import jax
import jax.numpy as jnp
from jax import lax
from jax.experimental import pallas as pl
from jax.experimental.pallas import tpu as pltpu

F32 = jnp.float32
BF16 = jnp.bfloat16

DIFF_QK_DIM = 64
DIFF_V_DIM = 2 * DIFF_QK_DIM
FOX_HEAD_DIM = 64
CONV_WIDTH = 3
NUM_BUCKETS = 32
MAX_EXACT = NUM_BUCKETS // 2
MAX_DISTANCE = 128
N_MOD = 6
NORM_EPS = 1e-6
NEG_INF = -1e30
LOG2E = math.log2(math.e)

LANES = 128
SUBLANES = 8
BF16_ROWS = 16
MXU_DEPTH = 256
VMEM_LIMIT_BYTES = 56 * 1024 * 1024

ATTN_TILE = 512
ATTN_SEQS = 2
ROW_TILE = 256
FFN_SEQS = 2
FFN_CHUNK = 256
MOD_COL_TILE = 1536

NT_DIMS = (((1,), (1,)), ((), ()))


def _params(semantics):
    return pltpu.CompilerParams(dimension_semantics=semantics, vmem_limit_bytes=VMEM_LIMIT_BYTES)


def _rms_scale(x, axis=-1):
    return lax.rsqrt(jnp.mean(x * x, axis=axis, keepdims=True) + NORM_EPS)


def _mod_kernel(c_ref, w_ref, b_ref, o_ref):
    c = c_ref[...]
    act = c * jax.nn.sigmoid(c)
    o_ref[...] = jnp.dot(act.astype(BF16), w_ref[...].astype(BF16),
                         preferred_element_type=F32) + b_ref[...]


def _modulation(c, w, b):
    bsz, d = c.shape
    n = w.shape[1]
    tn = MOD_COL_TILE
    return pl.pallas_call(
        _mod_kernel,
        grid=(n // tn,),
        in_specs=[pl.BlockSpec((bsz, d), lambda j: (0, 0)),
                  pl.BlockSpec((d, tn), lambda j: (0, j)),
                  pl.BlockSpec((1, tn), lambda j: (0, j))],
        out_specs=pl.BlockSpec((bsz, tn), lambda j: (0, j)),
        out_shape=jax.ShapeDtypeStruct((bsz, n), F32),
        compiler_params=_params(("arbitrary",)),
        name="modulation",
    )(c, w, b.reshape(1, n))


def _proj_kernel(x_ref, mod_ref, g_ref, wt_ref, wk_ref, fm_ref, dk_ref, fk_ref, fl_ref,
                 *, width, n_fox_heads):
    x = x_ref[0]
    h = (x * _rms_scale(x) * g_ref[...]) * (1.0 + mod_ref[0, 1:2, :]) + mod_ref[0, 0:1, :]
    hb = h.astype(BF16)
    r = lax.dot_general(wt_ref[...], hb, NT_DIMS, preferred_element_type=F32)
    fm_ref[0, 0] = r[:4 * width].astype(BF16)
    fl_ref[0] = r[4 * width:4 * width + n_fox_heads]
    r = jnp.dot(hb, wk_ref[...], preferred_element_type=F32)
    dk_ref[0] = r[:, :width].astype(BF16)
    fk_ref[0] = r[:, width:].astype(BF16)


def _in_projection(x, mod3, g, w_fm_t, w_tm, *, width, n_fox_heads):
    bsz, s, d = x.shape
    ts = ATTN_TILE
    k_shape = jax.ShapeDtypeStruct((bsz, s, width), BF16)
    k_spec = pl.BlockSpec((1, ts, width), lambda b, i: (b, i, 0))
    kern = functools.partial(_proj_kernel, width=width, n_fox_heads=n_fox_heads)
    return pl.pallas_call(
        kern,
        grid=(bsz, s // ts),
        in_specs=[pl.BlockSpec((1, ts, d), lambda b, i: (b, i, 0)),
                  pl.BlockSpec((1, N_MOD, d), lambda b, i: (b, 0, 0)),
                  pl.BlockSpec((1, d), lambda b, i: (0, 0)),
                  pl.BlockSpec(w_fm_t.shape, lambda b, i: (0, 0)),
                  pl.BlockSpec(w_tm.shape, lambda b, i: (0, 0))],
        out_specs=[pl.BlockSpec((1, 1, 4 * width, ts), lambda b, i: (b, i, 0, 0)),
                   k_spec, k_spec,
                   pl.BlockSpec((1, n_fox_heads, ts), lambda b, i: (b, 0, i))],
        out_shape=[jax.ShapeDtypeStruct((bsz, s // ts, 4 * width, ts), BF16),
                   k_shape, k_shape,
                   jax.ShapeDtypeStruct((bsz, n_fox_heads, s), F32)],
        compiler_params=_params(("arbitrary", "arbitrary")),
        name="in_projection",
    )(x, mod3, g.reshape(1, d), w_fm_t, w_tm)


def _split3(x):
    hi = x.astype(BF16)
    r1 = x - hi.astype(F32)
    mid = r1.astype(BF16)
    lo = (r1 - mid.astype(F32)).astype(BF16)
    return hi, mid, lo


def _cum_kernel(fl_ref, fb_ref, qa_ref, ka_ref):
    z = fl_ref[0] + fb_ref[...]
    acc = jnp.minimum(z, 0.0) - jnp.log1p(jnp.exp(-jnp.abs(z)))
    n_heads, s = acc.shape
    pos = lax.broadcasted_iota(jnp.int32, acc.shape, 1)
    shift = 1
    while shift < s:
        acc = acc + jnp.where(pos >= shift, pltpu.roll(acc, shift, 1), 0.0)
        shift *= 2
    acc = acc * LOG2E
    q_parts = _split3(acc)
    k_parts = _split3(-acc)
    ones = jnp.ones((3, s), BF16)
    zeros = jnp.zeros((BF16_ROWS - 6, s), BF16)
    for h in range(n_heads):
        qa_ref[0, h] = jnp.concatenate([p[h:h + 1] for p in q_parts] + [ones, zeros], axis=0)
        ka_ref[0, h] = jnp.concatenate([ones] + [p[h:h + 1] for p in k_parts] + [zeros], axis=0)


def _forget_aug(fl_t, forget_b):
    bsz, h, s = fl_t.shape
    aug_shape = jax.ShapeDtypeStruct((bsz, h, BF16_ROWS, s), BF16)
    aug_spec = pl.BlockSpec((1, h, BF16_ROWS, s), lambda b: (b, 0, 0, 0))
    return pl.pallas_call(
        _cum_kernel,
        grid=(bsz,),
        in_specs=[pl.BlockSpec((1, h, s), lambda b: (b, 0, 0)),
                  pl.BlockSpec((h, 1), lambda b: (0, 0))],
        out_specs=[aug_spec, aug_spec],
        out_shape=[aug_shape, aug_shape],
        compiler_params=_params(("arbitrary",)),
        name="forget_cumsum",
    )(fl_t, forget_b.reshape(h, 1))


def _bias_kernel(rb_ref, o_ref, *, tile, n_delta):
    h = pl.program_id(0)
    u = lax.broadcasted_iota(jnp.int32, (1, 2 * tile), 1)
    for dd in range(n_delta):
        n = jnp.maximum(dd * tile + u - tile, 0)
        nf = jnp.maximum(n, 1).astype(F32)
        large = MAX_EXACT + (jnp.log(nf / MAX_EXACT) / math.log(MAX_DISTANCE / MAX_EXACT)
                             * (NUM_BUCKETS - MAX_EXACT)).astype(jnp.int32)
        large = jnp.minimum(large, NUM_BUCKETS - 1)
        bucket = jnp.where(n < MAX_EXACT, n, large)
        row = jnp.zeros((1, 2 * tile), F32)
        for b in range(NUM_BUCKETS):
            row = jnp.where(bucket == b, rb_ref[h, b], row)
        row = row * LOG2E
        full = pltpu.roll(jnp.broadcast_to(row, (tile, 2 * tile)), 0, 1, stride=1, stride_axis=0)
        o_ref[0, dd] = full[:, tile:]


def _bias_tiles(rel_bias_t, seq, tile):
    n_heads = rel_bias_t.shape[0]
    n_delta = seq // tile
    kern = functools.partial(_bias_kernel, tile=tile, n_delta=n_delta)
    return pl.pallas_call(
        kern,
        grid=(n_heads,),
        in_specs=[pl.BlockSpec(memory_space=pltpu.SMEM)],
        out_specs=pl.BlockSpec((1, n_delta, tile, tile), lambda h: (h, 0, 0, 0)),
        out_shape=jax.ShapeDtypeStruct((n_heads, n_delta, tile, tile), F32),
        compiler_params=_params(("arbitrary",)),
        name="bias_tiles",
    )(rel_bias_t)


def _causal_flash(n_tiles, tile, scratch, streams):
    views = [[[r.at[pl.ds(4 * s + 2 * parity, 2)] for r in scratch] for parity in range(2)]
             for s in range(len(streams))]

    def start(i):
        for seq in views:
            m_scr, acc_scr = seq[i % 2][:2]
            m_scr[...] = jnp.full(m_scr.shape, NEG_INF, F32)
            acc_scr[...] = jnp.zeros(acc_scr.shape, F32)

    def scores(i, w_maps, j):
        for seq, stream, w in zip(views, streams, w_maps):
            s_scr, mt_scr = seq[i % 2][2], seq[i % 2][4]
            keys = stream[1](j)
            for idx in range(2):
                st = jnp.dot(keys, w[idx], preferred_element_type=F32)
                bias = stream[3](i, j)
                if bias is not None:
                    st = st + bias
                s_scr[idx] = st
                mt_scr[idx] = jnp.max(st, axis=0, keepdims=True)

    def softmax(i):
        for seq in views:
            m_scr, _, s_scr, p_scr, mt_scr, al_scr = seq[i % 2]
            for idx in range(2):
                m_prev = m_scr[idx]
                m_new = jnp.maximum(m_prev, mt_scr[idx])
                p_scr[idx] = jnp.exp2(s_scr[idx] - m_new).astype(BF16)
                al_scr[idx] = jnp.exp2(m_prev - m_new)
                m_scr[idx] = m_new

    def accumulate(i, j):
        for seq, stream in zip(views, streams):
            _, acc_scr, _, p_scr, _, al_scr = seq[i % 2]
            for idx in range(2):
                acc_scr[idx] = al_scr[idx] * acc_scr[idx] + jnp.dot(
                    stream[2](j, idx), p_scr[idx], preferred_element_type=F32)

    half = tile // 2
    diag_parts = ((0, half), (half, tile))

    def softmax_diag(i):
        for q0, n_keys in diag_parts:
            lanes = slice(q0, q0 + half)
            key_pos = lax.broadcasted_iota(jnp.int32, (n_keys, half), 0)
            qry_pos = lax.broadcasted_iota(jnp.int32, (n_keys, half), 1) + q0
            causal = key_pos <= qry_pos
            for seq in views:
                m_scr, _, s_scr, p_scr, _, al_scr = seq[i % 2]
                for idx in range(2):
                    st = jnp.where(causal, s_scr[idx, :n_keys, lanes], NEG_INF)
                    m_prev = m_scr[idx, :, lanes]
                    m_new = jnp.maximum(m_prev, jnp.max(st, axis=0, keepdims=True))
                    p_scr[idx, :n_keys, lanes] = jnp.exp2(st - m_new).astype(BF16)
                    al_scr[idx, :, lanes] = jnp.exp2(m_prev - m_new)
                    m_scr[idx, :, lanes] = m_new

    def accumulate_diag(i):
        for q0, n_keys in diag_parts:
            lanes = slice(q0, q0 + half)
            for seq, stream in zip(views, streams):
                _, acc_scr, _, p_scr, _, al_scr = seq[i % 2]
                for idx in range(2):
                    acc_scr[idx, :, lanes] = (
                        al_scr[idx, :, lanes] * acc_scr[idx, :, lanes]
                        + jnp.dot(stream[2](i, idx)[:, :n_keys], p_scr[idx, :n_keys, lanes],
                                  preferred_element_type=F32))

    def finish(i):
        for seq, stream in zip(views, streams):
            stream[4](i, seq[i % 2][1])

    w_maps = [stream[0](0) for stream in streams]
    start(0)
    scores(0, w_maps, 0)
    for i in range(n_tiles):
        def body(j, carry, i=i, w_maps=w_maps):
            softmax(i)
            scores(i, w_maps, j + 1)
            accumulate(i, j)
            return carry

        if i > 0:
            lax.fori_loop(0, i, body, 0)
            finish(i - 1)
        if i + 1 < n_tiles:
            w_maps = [stream[0](i + 1) for stream in streams]
            start(i + 1)
            scores(i + 1, w_maps, 0)
        softmax_diag(i)
        accumulate_diag(i)
    finish(n_tiles - 1)


def _masked_maps(qt):
    feat = lax.broadcasted_iota(jnp.int32, qt.shape, 0)
    zero = jnp.zeros_like(qt)
    return jnp.where(feat < DIFF_QK_DIM, qt, zero), jnp.where(feat >= DIFF_QK_DIM, qt, zero)


def _ones_rows(tile):
    row = lax.broadcasted_iota(jnp.int32, (BF16_ROWS, tile), 0)
    return jnp.where(row == 0, 1.0, 0.0).astype(BF16)


def _normalized(acc_ref, idx, rows):
    return acc_ref[idx, :rows, :] / acc_ref[idx, rows:rows + 1, :]


def _diff_attn_kernel(qt_ref, k_ref, vt_ref, bias_ref, lam_ref, g_ref, o_ref, *scratch,
                      lambda_init):
    n_seqs, n_tiles, _, tile = qt_ref.shape
    ones = _ones_rows(tile)
    lam = (jnp.exp(jnp.sum(lam_ref[0:1, :] * lam_ref[1:2, :], axis=-1, keepdims=True))
           - jnp.exp(jnp.sum(lam_ref[2:3, :] * lam_ref[3:4, :], axis=-1, keepdims=True))
           + lambda_init)

    def stream(b):
        def keys_fn(j):
            return k_ref[b, pl.ds(pl.multiple_of(j * tile, tile), tile), :]

        def values_fn(j, idx):
            return jnp.concatenate([vt_ref[b, j], ones], axis=0)

        def finish(i, acc_scr):
            o = (_normalized(acc_scr, 0, DIFF_V_DIM)
                 - lam * _normalized(acc_scr, 1, DIFF_V_DIM))
            o = (o * _rms_scale(o, axis=0) * g_ref[...]) * (1.0 - lambda_init)
            o_ref[b, i * tile:(i + 1) * tile, :] = o.T.astype(o_ref.dtype)

        return (lambda i: _masked_maps(qt_ref[b, i]),
                keys_fn, values_fn, lambda i, j: bias_ref[0, i - j], finish)

    _causal_flash(n_tiles, tile, scratch, [stream(b) for b in range(n_seqs)])


def _fox_attn_kernel(qt_ref, qa_ref, k_ref, ka_ref, vt_ref, o_ref, *scratch):
    n_seqs, n_tiles, _, tile = qt_ref.shape
    ones = _ones_rows(tile)
    aug_zero = jnp.zeros((BF16_ROWS, tile), BF16)
    pad = jnp.zeros((MXU_DEPTH - LANES - 2 * BF16_ROWS, tile), BF16)

    def stream(b):
        def keys_fn(j):
            rows = pl.ds(pl.multiple_of(j * tile, tile), tile)
            return jnp.concatenate([k_ref[b, rows, :], ka_ref[b, 0, rows, :]], axis=1)

        def values_fn(j, idx):
            rows = vt_ref[b, j, idx * FOX_HEAD_DIM:(idx + 1) * FOX_HEAD_DIM, :]
            return jnp.concatenate([rows, ones], axis=0)

        def finish(i, acc_scr):
            o = jnp.concatenate([_normalized(acc_scr, 0, FOX_HEAD_DIM),
                                 _normalized(acc_scr, 1, FOX_HEAD_DIM)], axis=0)
            o_ref[b, i * tile:(i + 1) * tile, :] = o.T.astype(o_ref.dtype)

        def w_maps_fn(i):
            q_a, q_b = _masked_maps(qt_ref[b, i])
            cols = slice(i * tile, (i + 1) * tile)
            return (jnp.concatenate([q_a, qa_ref[b, 0, :, cols], aug_zero, pad], axis=0),
                    jnp.concatenate([q_b, aug_zero, qa_ref[b, 1, :, cols], pad], axis=0))

        return (w_maps_fn, keys_fn, values_fn, lambda i, j: None, finish)

    _causal_flash(n_tiles, tile, scratch, [stream(b) for b in range(n_seqs)])


def _attn_scratch(tile, value_rows):
    n = 4 * ATTN_SEQS
    stat = pltpu.VMEM((n, 1, tile), F32)
    return [stat, pltpu.VMEM((n, value_rows + BF16_ROWS, tile), F32),
            pltpu.VMEM((n, tile, tile), F32), pltpu.VMEM((n, tile, tile), BF16), stat, stat]


def _diff_attention(fm, dk, bias_tiles, lam_vecs, subln_g, lambda_init):
    bsz, n_tiles, _, tile = fm.shape
    s = n_tiles * tile
    width = dk.shape[2]
    n_groups = width // LANES
    nb = ATTN_SEQS
    kern = functools.partial(_diff_attn_kernel, lambda_init=lambda_init)
    return pl.pallas_call(
        kern,
        grid=(n_groups, bsz // nb),
        in_specs=[pl.BlockSpec((nb, n_tiles, LANES, tile), lambda g, b: (b, 0, g, 0)),
                  pl.BlockSpec((nb, s, LANES), lambda g, b: (b, 0, g)),
                  pl.BlockSpec((nb, n_tiles, LANES, tile), lambda g, b: (b, 0, n_groups + g, 0)),
                  pl.BlockSpec((1, n_tiles, tile, tile), lambda g, b: (g, 0, 0, 0),
                               pipeline_mode=pl.Buffered(1)),
                  pl.BlockSpec(lam_vecs.shape, lambda g, b: (0, 0)),
                  pl.BlockSpec((LANES, 1), lambda g, b: (0, 0))],
        out_specs=pl.BlockSpec((nb, s, LANES), lambda g, b: (b, 0, g)),
        out_shape=jax.ShapeDtypeStruct((bsz, s, width), BF16),
        scratch_shapes=_attn_scratch(tile, DIFF_V_DIM),
        compiler_params=_params(("arbitrary", "arbitrary")),
        name="diff_attention",
    )(fm, dk, fm, bias_tiles, lam_vecs, subln_g.reshape(LANES, 1))


def _fox_attention(fm, fk, q_aug, k_aug):
    bsz, n_tiles, _, tile = fm.shape
    s = n_tiles * tile
    width = fk.shape[2]
    n_groups = width // LANES
    nb = ATTN_SEQS
    return pl.pallas_call(
        _fox_attn_kernel,
        grid=(n_groups, bsz // nb),
        in_specs=[pl.BlockSpec((nb, n_tiles, LANES, tile), lambda g, b: (b, 0, 2 * n_groups + g, 0)),
                  pl.BlockSpec((nb, 2, BF16_ROWS, s), lambda g, b: (b, g, 0, 0)),
                  pl.BlockSpec((nb, s, LANES), lambda g, b: (b, 0, g)),
                  pl.BlockSpec((nb, 1, s, LANES), lambda g, b: (b, g, 0, 0)),
                  pl.BlockSpec((nb, n_tiles, LANES, tile), lambda g, b: (b, 0, 3 * n_groups + g, 0))],
        out_specs=pl.BlockSpec((nb, s, LANES), lambda g, b: (b, 0, g)),
        out_shape=jax.ShapeDtypeStruct((bsz, s, width), BF16),
        scratch_shapes=_attn_scratch(tile, FOX_HEAD_DIM),
        compiler_params=_params(("arbitrary", "arbitrary")),
        name="fox_attention",
    )(fm, q_aug, fk, k_aug, fm)


def _ffn_kernel(d_ref, f_ref, x_ref, mod_ref, wo_ref, g_ref, wu_ref, cw_ref, cb_ref, wd_ref, fg_ref,
                o_ref, tail_scr, u_scr, x1_scr, h_scr, *, width, d_ff, chunk, final_norm):
    i = pl.program_id(1)
    n_seqs, ts = x_ref.shape[:2]
    lane_tiles = chunk // LANES
    n_chunks = d_ff // chunk

    @pl.when(i == 0)
    def _():
        tail_scr[...] = jnp.zeros(tail_scr.shape, F32)

    def project(s):
        mix = (jnp.dot(d_ref[s], wo_ref[:width, :], preferred_element_type=F32)
               + jnp.dot(f_ref[s], wo_ref[width:, :], preferred_element_type=F32))
        x1 = x_ref[s] + mod_ref[s, 2:3, :] * mix
        x1_scr[s] = x1
        h = (x1 * _rms_scale(x1) * g_ref[...]) * (1.0 + mod_ref[s, 4:5, :]) + mod_ref[s, 3:4, :]
        h_scr[s] = h.astype(BF16)

    def buffers(s, c, half):
        base = ((s * 2 + c % 2) * 2 + half) * lane_tiles
        return [u_scr.at[base + t] for t in range(lane_tiles)]

    def up(s, c):
        for half in range(2):
            c0 = half * d_ff + c * chunk
            u = jnp.dot(h_scr[s], wu_ref[:, c0:c0 + chunk], preferred_element_type=F32)
            for t, buf in enumerate(buffers(s, c, half)):
                cols = slice(c0 + t * LANES, c0 + (t + 1) * LANES)
                buf[0:SUBLANES, :] = tail_scr[s, :, cols]
                buf[SUBLANES:, :] = u[:, t * LANES:(t + 1) * LANES]
                tail_scr[s, :, cols] = u[ts - SUBLANES:, t * LANES:(t + 1) * LANES]

    def conv(s, c, half):
        c0 = half * d_ff + c * chunk
        tiles = []
        for t, buf in enumerate(buffers(s, c, half)):
            cols = slice(c0 + t * LANES, c0 + (t + 1) * LANES)
            y = cb_ref[:, cols]
            for tap in range(CONV_WIDTH):
                lo = SUBLANES - (CONV_WIDTH - 1) + tap
                y = y + cw_ref[tap:tap + 1, cols] * buf[pl.ds(lo, ts), :]
            tiles.append(y)
        return jnp.concatenate(tiles, axis=1)

    acc = []
    for s in range(n_seqs):
        project(s)
        acc.append(jnp.zeros((ts, o_ref.shape[2]), F32))
    for s in range(n_seqs):
        up(s, 0)
    for c in range(n_chunks):
        for s in range(n_seqs):
            gate = conv(s, c, 0)
            act = (gate * jax.nn.sigmoid(gate) * conv(s, c, 1)).astype(BF16)
            acc[s] = acc[s] + jnp.dot(act, wd_ref[c * chunk:(c + 1) * chunk, :],
                                      preferred_element_type=F32)
            if c + 1 < n_chunks:
                up(s, c + 1)
    for s in range(n_seqs):
        x2 = x1_scr[s] + mod_ref[s, 5:6, :] * acc[s]
        if final_norm:
            x2 = x2 * _rms_scale(x2) * fg_ref[...]
        o_ref[s] = x2


def _out_ffn(d_out, f_out, x, mod3, w_out, ffn_g, w_up, conv_w, conv_b, w_down, final_g, final_norm):
    bsz, s, d = x.shape
    width = d_out.shape[2]
    d_ff = w_down.shape[0]
    ts = ROW_TILE
    nb = FFN_SEQS
    row_spec = pl.BlockSpec((nb, ts, d), lambda b, i: (b, i, 0))
    attn_spec = pl.BlockSpec((nb, ts, width), lambda b, i: (b, i, 0))
    kern = functools.partial(_ffn_kernel, width=width, d_ff=d_ff, chunk=FFN_CHUNK,
                             final_norm=final_norm)
    const = lambda b, i: (0, 0)
    resident = functools.partial(pl.BlockSpec, index_map=const, pipeline_mode=pl.Buffered(1))
    return pl.pallas_call(
        kern,
        grid=(bsz // nb, s // ts),
        in_specs=[attn_spec, attn_spec, row_spec,
                  pl.BlockSpec((nb, N_MOD, d), lambda b, i: (b, 0, 0)),
                  resident(w_out.shape),
                  pl.BlockSpec((1, d), const),
                  resident(w_up.shape),
                  pl.BlockSpec(conv_w.shape, const),
                  pl.BlockSpec((1, 2 * d_ff), const),
                  resident(w_down.shape),
                  pl.BlockSpec((1, d), const)],
        out_specs=row_spec,
        out_shape=jax.ShapeDtypeStruct((bsz, s, d), F32),
        scratch_shapes=[pltpu.VMEM((nb, SUBLANES, 2 * d_ff), F32),
                        pltpu.VMEM((nb * 4 * FFN_CHUNK // LANES, ts + SUBLANES, LANES), F32),
                        pltpu.VMEM((nb, ts, d), F32), pltpu.VMEM((nb, ts, d), BF16)],
        compiler_params=_params(("arbitrary", "arbitrary")),
        name="out_ffn",
    )(d_out, f_out, x, mod3, w_out, ffn_g.reshape(1, d), w_up, conv_w,
      conv_b.reshape(1, 2 * d_ff), w_down, final_g.reshape(1, d))


def kernel(x, c, ada_w, ada_b, attn_norm_g, w_in, forget_b, lambda_q1, lambda_k1, lambda_q2,
           lambda_k2, subln_g, rel_bias, w_out, ffn_norm_g, w_up, conv_w, conv_b, w_down,
           final_norm_g):
    bsz, s, d = x.shape
    depth = ada_w.shape[0]
    n_fox_heads = forget_b.shape[1]
    width = (w_in.shape[2] - n_fox_heads) // 6
    assert width % LANES == 0 and subln_g.shape[1] == DIFF_V_DIM == LANES
    assert n_fox_heads * FOX_HEAD_DIM == width and s % ATTN_TILE == 0 and s % ROW_TILE == 0
    assert bsz % ATTN_SEQS == 0 and bsz % FFN_SEQS == 0
    n_pairs = n_fox_heads // 2
    qk_scale = DIFF_QK_DIM ** -0.5 * LOG2E

    bias = _bias_tiles(rel_bias.T, s, ATTN_TILE)

    for l in range(depth):
        lambda_init = 0.8 - 0.6 * math.exp(-0.3 * l)
        mod3 = _modulation(c, ada_w[l], ada_b[l]).reshape(bsz, N_MOD, d)

        w = w_in[l]
        cols = lambda n: w[:, n * width:(n + 1) * width]
        w_fl = jnp.pad(w[:, 6 * width:], ((0, 0), (0, BF16_ROWS - n_fox_heads)))
        w_fm_t = jnp.concatenate([cols(0) * qk_scale, cols(2), cols(3) * qk_scale, cols(5), w_fl],
                                 axis=1).T.astype(BF16)
        w_tm = jnp.concatenate([cols(1), cols(4)], axis=1).astype(BF16)
        fm, dk, fk, fl_t = _in_projection(x, mod3, attn_norm_g[l], w_fm_t, w_tm,
                                          width=width, n_fox_heads=n_fox_heads)

        q_aug, k_aug_t = _forget_aug(fl_t, forget_b[l])
        k_aug = jnp.transpose(k_aug_t.reshape(bsz, n_pairs, 2 * BF16_ROWS, s), (0, 1, 3, 2))
        k_aug = jnp.pad(k_aug, ((0, 0),) * 3 + ((0, LANES - 2 * BF16_ROWS),))

        lam_vecs = jnp.stack([lambda_q1[l], lambda_k1[l], lambda_q2[l], lambda_k2[l]]).astype(F32)
        d_out = _diff_attention(fm, dk, bias, lam_vecs, subln_g[l], lambda_init)
        f_out = _fox_attention(fm, fk, q_aug, k_aug)

        x = _out_ffn(d_out, f_out, x, mod3, w_out[l].astype(BF16), ffn_norm_g[l],
                     w_up[l].astype(BF16), conv_w[l], conv_b[l], w_down[l].astype(BF16),
                     final_norm_g, final_norm=(l == depth - 1))
    return x
```

```python
import functools
import math

import jax
import jax.numpy as jnp
from jax import lax
from jax.experimental import pallas as pl
from jax.experimental.pallas import tpu as pltpu

F32 = jnp.float32
BF16 = jnp.bfloat16

DIFF_QK_DIM = 64
DIFF_V_DIM = 2 * DIFF_QK_DIM
FOX_HEAD_DIM = 64
CONV_WIDTH = 3
NUM_BUCKETS = 32
MAX_EXACT = NUM_BUCKETS // 2
MAX_DISTANCE = 128
N_MOD = 6
NORM_EPS = 1e-6
NEG_INF = -1e30
LOG2E = math.log2(math.e)

LANES = 128
SUBLANES = 8
BF16_ROWS = 16
MXU_DEPTH = 256
VMEM_LIMIT_BYTES = 56 * 1024 * 1024

ATTN_TILE = 512
ATTN_SEQS = 2
ROW_TILE = 256
FFN_SEQS = 2
FFN_CHUNK = 256
MOD_COL_TILE = 1536

NT_DIMS = (((1,), (1,)), ((), ()))


def _params(semantics):
    return pltpu.CompilerParams(dimension_semantics=semantics, vmem_limit_bytes=VMEM_LIMIT_BYTES)


def _rms_scale(x, axis=-1):
    return lax.rsqrt(jnp.mean(x * x, axis=axis, keepdims=True) + NORM_EPS)


def _mod_kernel(c_ref, w_ref, b_ref, o_ref):
    c = c_ref[...]
    act = c * jax.nn.sigmoid(c)
    o_ref[...] = jnp.dot(act.astype(BF16), w_ref[...].astype(BF16),
                         preferred_element_type=F32) + b_ref[...]


def _modulation(c, w, b):
    bsz, d = c.shape
    n = w.shape[1]
    tn = MOD_COL_TILE
    return pl.pallas_call(
        _mod_kernel,
        grid=(n // tn,),
        in_specs=[pl.BlockSpec((bsz, d), lambda j: (0, 0)),
                  pl.BlockSpec((d, tn), lambda j: (0, j)),
                  pl.BlockSpec((1, tn), lambda j: (0, j))],
        out_specs=pl.BlockSpec((bsz, tn), lambda j: (0, j)),
        out_shape=jax.ShapeDtypeStruct((bsz, n), F32),
        compiler_params=_params(("arbitrary",)),
        name="modulation",
    )(c, w, b.reshape(1, n))


def _proj_kernel(x_ref, mod_ref, g_ref, wt_ref, wk_ref, fm_ref, dk_ref, fk_ref, fl_ref,
                 *, width, n_fox_heads):
    x = x_ref[0]
    h = (x * _rms_scale(x) * g_ref[...]) * (1.0 + mod_ref[0, 1:2, :]) + mod_ref[0, 0:1, :]
    hb = h.astype(BF16)
    r = lax.dot_general(wt_ref[...], hb, NT_DIMS, preferred_element_type=F32)
    fm_ref[0, 0] = r[:4 * width].astype(BF16)
    fl_ref[0] = r[4 * width:4 * width + n_fox_heads]
    r = jnp.dot(hb, wk_ref[...], preferred_element_type=F32)
    dk_ref[0] = r[:, :width].astype(BF16)
    fk_ref[0] = r[:, width:].astype(BF16)


def _in_projection(x, mod3, g, w_fm_t, w_tm, *, width, n_fox_heads):
    bsz, s, d = x.shape
    ts = ATTN_TILE
    k_shape = jax.ShapeDtypeStruct((bsz, s, width), BF16)
    k_spec = pl.BlockSpec((1, ts, width), lambda b, i: (b, i, 0))
    kern = functools.partial(_proj_kernel, width=width, n_fox_heads=n_fox_heads)
    return pl.pallas_call(
        kern,
        grid=(bsz, s // ts),
        in_specs=[pl.BlockSpec((1, ts, d), lambda b, i: (b, i, 0)),
                  pl.BlockSpec((1, N_MOD, d), lambda b, i: (b, 0, 0)),
                  pl.BlockSpec((1, d), lambda b, i: (0, 0)),
                  pl.BlockSpec(w_fm_t.shape, lambda b, i: (0, 0)),
                  pl.BlockSpec(w_tm.shape, lambda b, i: (0, 0))],
        out_specs=[pl.BlockSpec((1, 1, 4 * width, ts), lambda b, i: (b, i, 0, 0)),
                   k_spec, k_spec,
                   pl.BlockSpec((1, n_fox_heads, ts), lambda b, i: (b, 0, i))],
        out_shape=[jax.ShapeDtypeStruct((bsz, s // ts, 4 * width, ts), BF16),
                   k_shape, k_shape,
                   jax.ShapeDtypeStruct((bsz, n_fox_heads, s), F32)],
        compiler_params=_params(("arbitrary", "arbitrary")),
        name="in_projection",
    )(x, mod3, g.reshape(1, d), w_fm_t, w_tm)


def _split3(x):
    hi = x.astype(BF16)
    r1 = x - hi.astype(F32)
    mid = r1.astype(BF16)
    lo = (r1 - mid.astype(F32)).astype(BF16)
    return hi, mid, lo


def _cum_kernel(fl_ref, fb_ref, qa_ref, ka_ref):
    z = fl_ref[0] + fb_ref[...]
    acc = jnp.minimum(z, 0.0) - jnp.log1p(jnp.exp(-jnp.abs(z)))
    n_heads, s = acc.shape
    pos = lax.broadcasted_iota(jnp.int32, acc.shape, 1)
    shift = 1
    while shift < s:
        acc = acc + jnp.where(pos >= shift, pltpu.roll(acc, shift, 1), 0.0)
        shift *= 2
    acc = acc * LOG2E
    q_parts = _split3(acc)
    k_parts = _split3(-acc)
    ones = jnp.ones((3, s), BF16)
    zeros = jnp.zeros((BF16_ROWS - 6, s), BF16)
    for h in range(n_heads):
        qa_ref[0, h] = jnp.concatenate([p[h:h + 1] for p in q_parts] + [ones, zeros], axis=0)
        ka_ref[0, h] = jnp.concatenate([ones] + [p[h:h + 1] for p in k_parts] + [zeros], axis=0)


def _forget_aug(fl_t, forget_b):
    bsz, h, s = fl_t.shape
    aug_shape = jax.ShapeDtypeStruct((bsz, h, BF16_ROWS, s), BF16)
    aug_spec = pl.BlockSpec((1, h, BF16_ROWS, s), lambda b: (b, 0, 0, 0))
    return pl.pallas_call(
        _cum_kernel,
        grid=(bsz,),
        in_specs=[pl.BlockSpec((1, h, s), lambda b: (b, 0, 0)),
                  pl.BlockSpec((h, 1), lambda b: (0, 0))],
        out_specs=[aug_spec, aug_spec],
        out_shape=[aug_shape, aug_shape],
        compiler_params=_params(("arbitrary",)),
        name="forget_cumsum",
    )(fl_t, forget_b.reshape(h, 1))


def _bias_kernel(rb_ref, o_ref, *, tile, n_delta):
    h = pl.program_id(0)
    u = lax.broadcasted_iota(jnp.int32, (1, 2 * tile), 1)
    for dd in range(n_delta):
        n = jnp.maximum(dd * tile + u - tile, 0)
        nf = jnp.maximum(n, 1).astype(F32)
        large = MAX_EXACT + (jnp.log(nf / MAX_EXACT) / math.log(MAX_DISTANCE / MAX_EXACT)
                             * (NUM_BUCKETS - MAX_EXACT)).astype(jnp.int32)
        large = jnp.minimum(large, NUM_BUCKETS - 1)
        bucket = jnp.where(n < MAX_EXACT, n, large)
        row = jnp.zeros((1, 2 * tile), F32)
        for b in range(NUM_BUCKETS):
            row = jnp.where(bucket == b, rb_ref[h, b], row)
        row = row * LOG2E
        full = pltpu.roll(jnp.broadcast_to(row, (tile, 2 * tile)), 0, 1, stride=1, stride_axis=0)
        o_ref[0, dd] = full[:, tile:]


def _bias_tiles(rel_bias_t, seq, tile):
    n_heads = rel_bias_t.shape[0]
    n_delta = seq // tile
    kern = functools.partial(_bias_kernel, tile=tile, n_delta=n_delta)
    return pl.pallas_call(
        kern,
        grid=(n_heads,),
        in_specs=[pl.BlockSpec(memory_space=pltpu.SMEM)],
        out_specs=pl.BlockSpec((1, n_delta, tile, tile), lambda h: (h, 0, 0, 0)),
        out_shape=jax.ShapeDtypeStruct((n_heads, n_delta, tile, tile), F32),
        compiler_params=_params(("arbitrary",)),
        name="bias_tiles",
    )(rel_bias_t)


def _causal_flash(n_tiles, tile, scratch, streams):
    views = [[[r.at[pl.ds(4 * s + 2 * parity, 2)] for r in scratch] for parity in range(2)]
             for s in range(len(streams))]

    def start(i):
        for seq in views:
            m_scr, acc_scr = seq[i % 2][:2]
            m_scr[...] = jnp.full(m_scr.shape, NEG_INF, F32)
            acc_scr[...] = jnp.zeros(acc_scr.shape, F32)

    def scores(i, w_maps, j):
        for seq, stream, w in zip(views, streams, w_maps):
            s_scr, mt_scr = seq[i % 2][2], seq[i % 2][4]
            keys = stream[1](j)
            for idx in range(2):
                st = jnp.dot(keys, w[idx], preferred_element_type=F32)
                bias = stream[3](i, j)
                if bias is not None:
                    st = st + bias
                s_scr[idx] = st
                mt_scr[idx] = jnp.max(st, axis=0, keepdims=True)

    def softmax(i):
        for seq in views:
            m_scr, _, s_scr, p_scr, mt_scr, al_scr = seq[i % 2]
            for idx in range(2):
                m_prev = m_scr[idx]
                m_new = jnp.maximum(m_prev, mt_scr[idx])
                p_scr[idx] = jnp.exp2(s_scr[idx] - m_new).astype(BF16)
                al_scr[idx] = jnp.exp2(m_prev - m_new)
                m_scr[idx] = m_new

    def accumulate(i, j):
        for seq, stream in zip(views, streams):
            _, acc_scr, _, p_scr, _, al_scr = seq[i % 2]
            for idx in range(2):
                acc_scr[idx] = al_scr[idx] * acc_scr[idx] + jnp.dot(
                    stream[2](j, idx), p_scr[idx], preferred_element_type=F32)

    half = tile // 2
    diag_parts = ((0, half), (half, tile))

    def softmax_diag(i):
        for q0, n_keys in diag_parts:
            lanes = slice(q0, q0 + half)
            key_pos = lax.broadcasted_iota(jnp.int32, (n_keys, half), 0)
            qry_pos = lax.broadcasted_iota(jnp.int32, (n_keys, half), 1) + q0
            causal = key_pos <= qry_pos
            for seq in views:
                m_scr, _, s_scr, p_scr, _, al_scr = seq[i % 2]
                for idx in range(2):
                    st = jnp.where(causal, s_scr[idx, :n_keys, lanes], NEG_INF)
                    m_prev = m_scr[idx, :, lanes]
                    m_new = jnp.maximum(m_prev, jnp.max(st, axis=0, keepdims=True))
                    p_scr[idx, :n_keys, lanes] = jnp.exp2(st - m_new).astype(BF16)
                    al_scr[idx, :, lanes] = jnp.exp2(m_prev - m_new)
                    m_scr[idx, :, lanes] = m_new

    def accumulate_diag(i):
        for q0, n_keys in diag_parts:
            lanes = slice(q0, q0 + half)
            for seq, stream in zip(views, streams):
                _, acc_scr, _, p_scr, _, al_scr = seq[i % 2]
                for idx in range(2):
                    acc_scr[idx, :, lanes] = (
                        al_scr[idx, :, lanes] * acc_scr[idx, :, lanes]
                        + jnp.dot(stream[2](i, idx)[:, :n_keys], p_scr[idx, :n_keys, lanes],
                                  preferred_element_type=F32))

    def finish(i):
        for seq, stream in zip(views, streams):
            stream[4](i, seq[i % 2][1])

    w_maps = [stream[0](0) for stream in streams]
    start(0)
    scores(0, w_maps, 0)
    for i in range(n_tiles):
        def body(j, carry, i=i, w_maps=w_maps):
            softmax(i)
            scores(i, w_maps, j + 1)
            accumulate(i, j)
            return carry

        if i > 0:
            lax.fori_loop(0, i, body, 0)
            finish(i - 1)
        if i + 1 < n_tiles:
            w_maps = [stream[0](i + 1) for stream in streams]
            start(i + 1)
            scores(i + 1, w_maps, 0)
        softmax_diag(i)
        accumulate_diag(i)
    finish(n_tiles - 1)


def _masked_maps(qt):
    feat = lax.broadcasted_iota(jnp.int32, qt.shape, 0)
    zero = jnp.zeros_like(qt)
    return jnp.where(feat < DIFF_QK_DIM, qt, zero), jnp.where(feat >= DIFF_QK_DIM, qt, zero)


def _ones_rows(tile):
    row = lax.broadcasted_iota(jnp.int32, (BF16_ROWS, tile), 0)
    return jnp.where(row == 0, 1.0, 0.0).astype(BF16)


def _normalized(acc_ref, idx, rows):
    return acc_ref[idx, :rows, :] / acc_ref[idx, rows:rows + 1, :]


def _diff_attn_kernel(qt_ref, k_ref, vt_ref, bias_ref, lam_ref, g_ref, o_ref, *scratch,
                      lambda_init):
    n_seqs, n_tiles, _, tile = qt_ref.shape
    ones = _ones_rows(tile)
    lam = (jnp.exp(jnp.sum(lam_ref[0:1, :] * lam_ref[1:2, :], axis=-1, keepdims=True))
           - jnp.exp(jnp.sum(lam_ref[2:3, :] * lam_ref[3:4, :], axis=-1, keepdims=True))
           + lambda_init)

    def stream(b):
        def keys_fn(j):
            return k_ref[b, pl.ds(pl.multiple_of(j * tile, tile), tile), :]

        def values_fn(j, idx):
            return jnp.concatenate([vt_ref[b, j], ones], axis=0)

        def finish(i, acc_scr):
            o = (_normalized(acc_scr, 0, DIFF_V_DIM)
                 - lam * _normalized(acc_scr, 1, DIFF_V_DIM))
            o = (o * _rms_scale(o, axis=0) * g_ref[...]) * (1.0 - lambda_init)
            o_ref[b, i * tile:(i + 1) * tile, :] = o.T.astype(o_ref.dtype)

        return (lambda i: _masked_maps(qt_ref[b, i]),
                keys_fn, values_fn, lambda i, j: bias_ref[0, i - j], finish)

    _causal_flash(n_tiles, tile, scratch, [stream(b) for b in range(n_seqs)])


def _fox_attn_kernel(qt_ref, qa_ref, k_ref, ka_ref, vt_ref, o_ref, kat_scr, *scratch):
    n_seqs, n_tiles, _, tile = qt_ref.shape
    ones = _ones_rows(tile)
    aug_zero = jnp.zeros((BF16_ROWS, tile), BF16)
    pad = jnp.zeros((MXU_DEPTH - LANES - 2 * BF16_ROWS, tile), BF16)
    lane_pad = jnp.zeros((LANES - 2 * BF16_ROWS, tile), BF16)

    for b in range(n_seqs):
        for t in range(n_tiles):
            cols = slice(t * tile, (t + 1) * tile)
            rows = jnp.concatenate([ka_ref[b, 0, :, cols], ka_ref[b, 1, :, cols], lane_pad], axis=0)
            kat_scr[b, cols, :] = rows.astype(F32).T.astype(BF16)

    def stream(b):
        def keys_fn(j):
            rows = pl.ds(pl.multiple_of(j * tile, tile), tile)
            return jnp.concatenate([k_ref[b, rows, :], kat_scr[b, rows, :]], axis=1)

        def values_fn(j, idx):
            rows = vt_ref[b, j, idx * FOX_HEAD_DIM:(idx + 1) * FOX_HEAD_DIM, :]
            return jnp.concatenate([rows, ones], axis=0)

        def finish(i, acc_scr):
            o = jnp.concatenate([_normalized(acc_scr, 0, FOX_HEAD_DIM),
                                 _normalized(acc_scr, 1, FOX_HEAD_DIM)], axis=0)
            o_ref[b, i * tile:(i + 1) * tile, :] = o.T.astype(o_ref.dtype)

        def w_maps_fn(i):
            q_a, q_b = _masked_maps(qt_ref[b, i])
            cols = slice(i * tile, (i + 1) * tile)
            return (jnp.concatenate([q_a, qa_ref[b, 0, :, cols], aug_zero, pad], axis=0),
                    jnp.concatenate([q_b, aug_zero, qa_ref[b, 1, :, cols], pad], axis=0))

        return (w_maps_fn, keys_fn, values_fn, lambda i, j: None, finish)

    _causal_flash(n_tiles, tile, scratch, [stream(b) for b in range(n_seqs)])


def _attn_scratch(tile, value_rows):
    n = 4 * ATTN_SEQS
    stat = pltpu.VMEM((n, 1, tile), F32)
    return [stat, pltpu.VMEM((n, value_rows + BF16_ROWS, tile), F32),
            pltpu.VMEM((n, tile, tile), F32), pltpu.VMEM((n, tile, tile), BF16), stat, stat]


def _diff_attention(fm, dk, bias_tiles, lam_vecs, subln_g, lambda_init):
    bsz, n_tiles, _, tile = fm.shape
    s = n_tiles * tile
    width = dk.shape[2]
    n_groups = width // LANES
    nb = ATTN_SEQS
    kern = functools.partial(_diff_attn_kernel, lambda_init=lambda_init)
    return pl.pallas_call(
        kern,
        grid=(n_groups, bsz // nb),
        in_specs=[pl.BlockSpec((nb, n_tiles, LANES, tile), lambda g, b: (b, 0, g, 0)),
                  pl.BlockSpec((nb, s, LANES), lambda g, b: (b, 0, g)),
                  pl.BlockSpec((nb, n_tiles, LANES, tile), lambda g, b: (b, 0, n_groups + g, 0)),
                  pl.BlockSpec((1, n_tiles, tile, tile), lambda g, b: (g, 0, 0, 0),
                               pipeline_mode=pl.Buffered(1)),
                  pl.BlockSpec(lam_vecs.shape, lambda g, b: (0, 0)),
                  pl.BlockSpec((LANES, 1), lambda g, b: (0, 0))],
        out_specs=pl.BlockSpec((nb, s, LANES), lambda g, b: (b, 0, g)),
        out_shape=jax.ShapeDtypeStruct((bsz, s, width), BF16),
        scratch_shapes=_attn_scratch(tile, DIFF_V_DIM),
        compiler_params=_params(("arbitrary", "arbitrary")),
        name="diff_attention",
    )(fm, dk, fm, bias_tiles, lam_vecs, subln_g.reshape(LANES, 1))


def _fox_attention(fm, fk, q_aug, k_aug):
    bsz, n_tiles, _, tile = fm.shape
    s = n_tiles * tile
    width = fk.shape[2]
    n_groups = width // LANES
    nb = ATTN_SEQS
    return pl.pallas_call(
        _fox_attn_kernel,
        grid=(n_groups, bsz // nb),
        in_specs=[pl.BlockSpec((nb, n_tiles, LANES, tile), lambda g, b: (b, 0, 2 * n_groups + g, 0)),
                  pl.BlockSpec((nb, 2, BF16_ROWS, s), lambda g, b: (b, g, 0, 0)),
                  pl.BlockSpec((nb, s, LANES), lambda g, b: (b, 0, g)),
                  pl.BlockSpec((nb, 2, BF16_ROWS, s), lambda g, b: (b, g, 0, 0)),
                  pl.BlockSpec((nb, n_tiles, LANES, tile), lambda g, b: (b, 0, 3 * n_groups + g, 0))],
        out_specs=pl.BlockSpec((nb, s, LANES), lambda g, b: (b, 0, g)),
        out_shape=jax.ShapeDtypeStruct((bsz, s, width), BF16),
        scratch_shapes=[pltpu.VMEM((nb, s, LANES), BF16)] + _attn_scratch(tile, FOX_HEAD_DIM),
        compiler_params=_params(("arbitrary", "arbitrary")),
        name="fox_attention",
    )(fm, q_aug, fk, k_aug, fm)


def _ffn_kernel(d_ref, f_ref, x_ref, mod_ref, wo_ref, g_ref, wu_ref, cw_ref, cb_ref, wd_ref, fg_ref,
                o_ref, tail_scr, u_scr, x1_scr, h_scr, *, width, d_ff, chunk, final_norm):
    i = pl.program_id(1)
    n_seqs, ts = x_ref.shape[:2]
    lane_tiles = chunk // LANES
    n_chunks = d_ff // chunk

    @pl.when(i == 0)
    def _():
        tail_scr[...] = jnp.zeros(tail_scr.shape, F32)

    def project(s):
        mix = (jnp.dot(d_ref[s], wo_ref[:width, :], preferred_element_type=F32)
               + jnp.dot(f_ref[s], wo_ref[width:, :], preferred_element_type=F32))
        x1 = x_ref[s] + mod_ref[s, 2:3, :] * mix
        x1_scr[s] = x1
        h = (x1 * _rms_scale(x1) * g_ref[...]) * (1.0 + mod_ref[s, 4:5, :]) + mod_ref[s, 3:4, :]
        h_scr[s] = h.astype(BF16)

    def buffers(s, c, half):
        base = ((s * 2 + c % 2) * 2 + half) * lane_tiles
        return [u_scr.at[base + t] for t in range(lane_tiles)]

    def up(s, c):
        for half in range(2):
            c0 = half * d_ff + c * chunk
            u = jnp.dot(h_scr[s], wu_ref[:, c0:c0 + chunk], preferred_element_type=F32)
            for t, buf in enumerate(buffers(s, c, half)):
                cols = slice(c0 + t * LANES, c0 + (t + 1) * LANES)
                buf[0:SUBLANES, :] = tail_scr[s, :, cols]
                buf[SUBLANES:, :] = u[:, t * LANES:(t + 1) * LANES]
                tail_scr[s, :, cols] = u[ts - SUBLANES:, t * LANES:(t + 1) * LANES]

    def conv(s, c, half):
        c0 = half * d_ff + c * chunk
        tiles = []
        for t, buf in enumerate(buffers(s, c, half)):
            cols = slice(c0 + t * LANES, c0 + (t + 1) * LANES)
            y = cb_ref[:, cols]
            for tap in range(CONV_WIDTH):
                lo = SUBLANES - (CONV_WIDTH - 1) + tap
                y = y + cw_ref[tap:tap + 1, cols] * buf[pl.ds(lo, ts), :]
            tiles.append(y)
        return jnp.concatenate(tiles, axis=1)

    acc = []
    for s in range(n_seqs):
        project(s)
        acc.append(jnp.zeros((ts, o_ref.shape[2]), F32))
    for s in range(n_seqs):
        up(s, 0)
    for c in range(n_chunks):
        for s in range(n_seqs):
            gate = conv(s, c, 0)
            act = (gate * jax.nn.sigmoid(gate) * conv(s, c, 1)).astype(BF16)
            acc[s] = acc[s] + jnp.dot(act, wd_ref[c * chunk:(c + 1) * chunk, :],
                                      preferred_element_type=F32)
            if c + 1 < n_chunks:
                up(s, c + 1)
    for s in range(n_seqs):
        x2 = x1_scr[s] + mod_ref[s, 5:6, :] * acc[s]
        if final_norm:
            x2 = x2 * _rms_scale(x2) * fg_ref[...]
        o_ref[s] = x2


def _out_ffn(d_out, f_out, x, mod3, w_out, ffn_g, w_up, conv_w, conv_b, w_down, final_g, final_norm):
    bsz, s, d = x.shape
    width = d_out.shape[2]
    d_ff = w_down.shape[0]
    ts = ROW_TILE
    nb = FFN_SEQS
    row_spec = pl.BlockSpec((nb, ts, d), lambda b, i: (b, i, 0))
    attn_spec = pl.BlockSpec((nb, ts, width), lambda b, i: (b, i, 0))
    kern = functools.partial(_ffn_kernel, width=width, d_ff=d_ff, chunk=FFN_CHUNK,
                             final_norm=final_norm)
    const = lambda b, i: (0, 0)
    resident = functools.partial(pl.BlockSpec, index_map=const, pipeline_mode=pl.Buffered(1))
    return pl.pallas_call(
        kern,
        grid=(bsz // nb, s // ts),
        in_specs=[attn_spec, attn_spec, row_spec,
                  pl.BlockSpec((nb, N_MOD, d), lambda b, i: (b, 0, 0)),
                  resident(w_out.shape),
                  pl.BlockSpec((1, d), const),
                  resident(w_up.shape),
                  pl.BlockSpec(conv_w.shape, const),
                  pl.BlockSpec((1, 2 * d_ff), const),
                  resident(w_down.shape),
                  pl.BlockSpec((1, d), const)],
        out_specs=row_spec,
        out_shape=jax.ShapeDtypeStruct((bsz, s, d), F32),
        scratch_shapes=[pltpu.VMEM((nb, SUBLANES, 2 * d_ff), F32),
                        pltpu.VMEM((nb * 4 * FFN_CHUNK // LANES, ts + SUBLANES, LANES), F32),
                        pltpu.VMEM((nb, ts, d), F32), pltpu.VMEM((nb, ts, d), BF16)],
        compiler_params=_params(("arbitrary", "arbitrary")),
        name="out_ffn",
    )(d_out, f_out, x, mod3, w_out, ffn_g.reshape(1, d), w_up, conv_w,
      conv_b.reshape(1, 2 * d_ff), w_down, final_g.reshape(1, d))


def kernel(x, c, ada_w, ada_b, attn_norm_g, w_in, forget_b, lambda_q1, lambda_k1, lambda_q2,
           lambda_k2, subln_g, rel_bias, w_out, ffn_norm_g, w_up, conv_w, conv_b, w_down,
           final_norm_g):
    bsz, s, d = x.shape
    depth = ada_w.shape[0]
    n_fox_heads = forget_b.shape[1]
    width = (w_in.shape[2] - n_fox_heads) // 6
    assert width % LANES == 0 and subln_g.shape[1] == DIFF_V_DIM == LANES
    assert n_fox_heads * FOX_HEAD_DIM == width and s % ATTN_TILE == 0 and s % ROW_TILE == 0
    assert bsz % ATTN_SEQS == 0 and bsz % FFN_SEQS == 0
    qk_scale = DIFF_QK_DIM ** -0.5 * LOG2E

    bias = _bias_tiles(rel_bias.T, s, ATTN_TILE)

    for l in range(depth):
        lambda_init = 0.8 - 0.6 * math.exp(-0.3 * l)
        mod3 = _modulation(c, ada_w[l], ada_b[l]).reshape(bsz, N_MOD, d)

        w = w_in[l]
        cols = lambda n: w[:, n * width:(n + 1) * width]
        w_fl = jnp.pad(w[:, 6 * width:], ((0, 0), (0, BF16_ROWS - n_fox_heads)))
        w_fm_t = jnp.concatenate([cols(0) * qk_scale, cols(2), cols(3) * qk_scale, cols(5), w_fl],
                                 axis=1).T.astype(BF16)
        w_tm = jnp.concatenate([cols(1), cols(4)], axis=1).astype(BF16)
        fm, dk, fk, fl_t = _in_projection(x, mod3, attn_norm_g[l], w_fm_t, w_tm,
                                          width=width, n_fox_heads=n_fox_heads)

        q_aug, k_aug = _forget_aug(fl_t, forget_b[l])

        lam_vecs = jnp.stack([lambda_q1[l], lambda_k1[l], lambda_q2[l], lambda_k2[l]]).astype(F32)
        d_out = _diff_attention(fm, dk, bias, lam_vecs, subln_g[l], lambda_init)
        f_out = _fox_attention(fm, fk, q_aug, k_aug)

        x = _out_ffn(d_out, f_out, x, mod3, w_out[l].astype(BF16), ffn_norm_g[l],
                     w_up[l].astype(BF16), conv_w[l], conv_b[l], w_down[l].astype(BF16),
                     final_norm_g, final_norm=(l == depth - 1))
    return x
```

```python
import functools
import math

import jax
import jax.numpy as jnp
from jax import lax
from jax.experimental import pallas as pl
from jax.experimental.pallas import tpu as pltpu

F32 = jnp.float32
BF16 = jnp.bfloat16

DIFF_QK_DIM = 64
DIFF_V_DIM = 2 * DIFF_QK_DIM
FOX_HEAD_DIM = 64
CONV_WIDTH = 3
NUM_BUCKETS = 32
MAX_EXACT = NUM_BUCKETS // 2
MAX_DISTANCE = 128
N_MOD = 6
NORM_EPS = 1e-6
NEG_INF = -1e30
LOG2E = math.log2(math.e)

LANES = 128
SUBLANES = 8
BF16_ROWS = 16
MXU_DEPTH = 256
VMEM_LIMIT_BYTES = 56 * 1024 * 1024

ATTN_TILE = 512
ATTN_SEQS = 2
ROW_TILE = 256
FFN_SEQS = 2
FFN_CHUNK = 256
MOD_COL_TILE = 1536

NT_DIMS = (((1,), (1,)), ((), ()))


def _params(semantics):
    return pltpu.CompilerParams(dimension_semantics=semantics, vmem_limit_bytes=VMEM_LIMIT_BYTES)


def _rms_scale(x, axis=-1):
    return lax.rsqrt(jnp.mean(x * x, axis=axis, keepdims=True) + NORM_EPS)


def _mod_kernel(c_ref, w_ref, b_ref, o_ref):
    c = c_ref[...]
    act = c * jax.nn.sigmoid(c)
    o_ref[...] = jnp.dot(act.astype(BF16), w_ref[...].astype(BF16),
                         preferred_element_type=F32) + b_ref[...]


def _modulation(c, w, b):
    bsz, d = c.shape
    n = w.shape[1]
    tn = MOD_COL_TILE
    return pl.pallas_call(
        _mod_kernel,
        grid=(n // tn,),
        in_specs=[pl.BlockSpec((bsz, d), lambda j: (0, 0)),
                  pl.BlockSpec((d, tn), lambda j: (0, j)),
                  pl.BlockSpec((1, tn), lambda j: (0, j))],
        out_specs=pl.BlockSpec((bsz, tn), lambda j: (0, j)),
        out_shape=jax.ShapeDtypeStruct((bsz, n), F32),
        compiler_params=_params(("arbitrary",)),
        name="modulation",
    )(c, w, b.reshape(1, n))


def _proj_kernel(x_ref, mod_ref, g_ref, wt_ref, wk_ref, fm_ref, dk_ref, fk_ref, fl_ref,
                 *, width, n_fox_heads):
    x = x_ref[0]
    h = (x * _rms_scale(x) * g_ref[...]) * (1.0 + mod_ref[0, 1:2, :]) + mod_ref[0, 0:1, :]
    hb = h.astype(BF16)
    r = lax.dot_general(wt_ref[...], hb, NT_DIMS, preferred_element_type=F32)
    fm_ref[0, 0] = r[:4 * width].astype(BF16)
    fl_ref[0] = r[4 * width:4 * width + n_fox_heads]
    r = jnp.dot(hb, wk_ref[...], preferred_element_type=F32)
    dk_ref[0] = r[:, :width].astype(BF16)
    fk_ref[0] = r[:, width:].astype(BF16)


def _in_projection(x, mod3, g, w_fm_t, w_tm, *, width, n_fox_heads):
    bsz, s, d = x.shape
    ts = ATTN_TILE
    k_shape = jax.ShapeDtypeStruct((bsz, s, width), BF16)
    k_spec = pl.BlockSpec((1, ts, width), lambda b, i: (b, i, 0))
    kern = functools.partial(_proj_kernel, width=width, n_fox_heads=n_fox_heads)
    return pl.pallas_call(
        kern,
        grid=(bsz, s // ts),
        in_specs=[pl.BlockSpec((1, ts, d), lambda b, i: (b, i, 0)),
                  pl.BlockSpec((1, N_MOD, d), lambda b, i: (b, 0, 0)),
                  pl.BlockSpec((1, d), lambda b, i: (0, 0)),
                  pl.BlockSpec(w_fm_t.shape, lambda b, i: (0, 0)),
                  pl.BlockSpec(w_tm.shape, lambda b, i: (0, 0))],
        out_specs=[pl.BlockSpec((1, 1, 4 * width, ts), lambda b, i: (b, i, 0, 0)),
                   k_spec, k_spec,
                   pl.BlockSpec((1, n_fox_heads, ts), lambda b, i: (b, 0, i))],
        out_shape=[jax.ShapeDtypeStruct((bsz, s // ts, 4 * width, ts), BF16),
                   k_shape, k_shape,
                   jax.ShapeDtypeStruct((bsz, n_fox_heads, s), F32)],
        compiler_params=_params(("arbitrary", "arbitrary")),
        name="in_projection",
    )(x, mod3, g.reshape(1, d), w_fm_t, w_tm)


def _split3(x):
    hi = x.astype(BF16)
    r1 = x - hi.astype(F32)
    mid = r1.astype(BF16)
    lo = (r1 - mid.astype(F32)).astype(BF16)
    return hi, mid, lo


def _cum_kernel(fl_ref, fb_ref, qa_ref, ka_ref):
    z = fl_ref[0] + fb_ref[...]
    acc = jnp.minimum(z, 0.0) - jnp.log1p(jnp.exp(-jnp.abs(z)))
    n_heads, s = acc.shape
    pos = lax.broadcasted_iota(jnp.int32, acc.shape, 1)
    shift = 1
    while shift < s:
        acc = acc + jnp.where(pos >= shift, pltpu.roll(acc, shift, 1), 0.0)
        shift *= 2
    acc = acc * LOG2E
    q_parts = _split3(acc)
    k_parts = _split3(-acc)
    ones = jnp.ones((len(q_parts), s), BF16)
    zeros = jnp.zeros((BF16_ROWS - 2 * len(q_parts), s), BF16)
    for h in range(n_heads):
        qa_ref[0, h] = jnp.concatenate([p[h:h + 1] for p in q_parts] + [ones, zeros], axis=0)
        ka_ref[0, h] = jnp.concatenate([ones] + [p[h:h + 1] for p in k_parts] + [zeros], axis=0)


def _forget_aug(fl_t, forget_b):
    bsz, h, s = fl_t.shape
    aug_shape = jax.ShapeDtypeStruct((bsz, h, BF16_ROWS, s), BF16)
    aug_spec = pl.BlockSpec((1, h, BF16_ROWS, s), lambda b: (b, 0, 0, 0))
    return pl.pallas_call(
        _cum_kernel,
        grid=(bsz,),
        in_specs=[pl.BlockSpec((1, h, s), lambda b: (b, 0, 0)),
                  pl.BlockSpec((h, 1), lambda b: (0, 0))],
        out_specs=[aug_spec, aug_spec],
        out_shape=[aug_shape, aug_shape],
        compiler_params=_params(("arbitrary",)),
        name="forget_cumsum",
    )(fl_t, forget_b.reshape(h, 1))


def _bias_kernel(rb_ref, o_ref, *, tile, n_delta):
    h = pl.program_id(0)
    u = lax.broadcasted_iota(jnp.int32, (1, 2 * tile), 1)
    for dd in range(n_delta):
        n = jnp.maximum(dd * tile + u - tile, 0)
        nf = jnp.maximum(n, 1).astype(F32)
        large = MAX_EXACT + (jnp.log(nf / MAX_EXACT) / math.log(MAX_DISTANCE / MAX_EXACT)
                             * (NUM_BUCKETS - MAX_EXACT)).astype(jnp.int32)
        large = jnp.minimum(large, NUM_BUCKETS - 1)
        bucket = jnp.where(n < MAX_EXACT, n, large)
        row = jnp.zeros((1, 2 * tile), F32)
        for b in range(NUM_BUCKETS):
            row = jnp.where(bucket == b, rb_ref[h, b], row)
        row = row * LOG2E
        full = pltpu.roll(jnp.broadcast_to(row, (tile, 2 * tile)), 0, 1, stride=1, stride_axis=0)
        o_ref[0, dd] = full[:, tile:]


def _bias_tiles(rel_bias_t, seq, tile):
    n_heads = rel_bias_t.shape[0]
    n_delta = seq // tile
    kern = functools.partial(_bias_kernel, tile=tile, n_delta=n_delta)
    return pl.pallas_call(
        kern,
        grid=(n_heads,),
        in_specs=[pl.BlockSpec(memory_space=pltpu.SMEM)],
        out_specs=pl.BlockSpec((1, n_delta, tile, tile), lambda h: (h, 0, 0, 0)),
        out_shape=jax.ShapeDtypeStruct((n_heads, n_delta, tile, tile), F32),
        compiler_params=_params(("arbitrary",)),
        name="bias_tiles",
    )(rel_bias_t)


def _causal_flash(n_tiles, tile, scratch, streams):
    views = [[[r.at[pl.ds(4 * s + 2 * parity, 2)] for r in scratch] for parity in range(2)]
             for s in range(len(streams))]

    def start(i):
        for seq in views:
            m_scr, acc_scr = seq[i % 2][:2]
            m_scr[...] = jnp.full(m_scr.shape, NEG_INF, F32)
            acc_scr[...] = jnp.zeros(acc_scr.shape, F32)

    def scores(i, w_maps, j):
        for seq, stream, w in zip(views, streams, w_maps):
            s_scr, mt_scr = seq[i % 2][2], seq[i % 2][4]
            keys = stream[1](j)
            for idx in range(2):
                st = jnp.dot(keys, w[idx], preferred_element_type=F32)
                bias = stream[3](i, j)
                if bias is not None:
                    st = st + bias
                s_scr[idx] = st
                mt_scr[idx] = jnp.max(st, axis=0, keepdims=True)

    def softmax(i):
        for seq in views:
            m_scr, _, s_scr, p_scr, mt_scr, al_scr = seq[i % 2]
            for idx in range(2):
                m_prev = m_scr[idx]
                m_new = jnp.maximum(m_prev, mt_scr[idx])
                p_scr[idx] = jnp.exp2(s_scr[idx] - m_new).astype(BF16)
                al_scr[idx] = jnp.exp2(m_prev - m_new)
                m_scr[idx] = m_new

    def accumulate(i, j):
        for seq, stream in zip(views, streams):
            _, acc_scr, _, p_scr, _, al_scr = seq[i % 2]
            for idx in range(2):
                acc_scr[idx] = al_scr[idx] * acc_scr[idx] + jnp.dot(
                    stream[2](j, idx), p_scr[idx], preferred_element_type=F32)

    half = tile // 2
    diag_parts = ((0, half), (half, tile))

    def softmax_diag(i):
        for q0, n_keys in diag_parts:
            lanes = slice(q0, q0 + half)
            key_pos = lax.broadcasted_iota(jnp.int32, (n_keys, half), 0)
            qry_pos = lax.broadcasted_iota(jnp.int32, (n_keys, half), 1) + q0
            causal = key_pos <= qry_pos
            for seq in views:
                m_scr, _, s_scr, p_scr, _, al_scr = seq[i % 2]
                for idx in range(2):
                    st = jnp.where(causal, s_scr[idx, :n_keys, lanes], NEG_INF)
                    m_prev = m_scr[idx, :, lanes]
                    m_new = jnp.maximum(m_prev, jnp.max(st, axis=0, keepdims=True))
                    p_scr[idx, :n_keys, lanes] = jnp.exp2(st - m_new).astype(BF16)
                    al_scr[idx, :, lanes] = jnp.exp2(m_prev - m_new)
                    m_scr[idx, :, lanes] = m_new

    def accumulate_diag(i):
        for q0, n_keys in diag_parts:
            lanes = slice(q0, q0 + half)
            for seq, stream in zip(views, streams):
                _, acc_scr, _, p_scr, _, al_scr = seq[i % 2]
                for idx in range(2):
                    acc_scr[idx, :, lanes] = (
                        al_scr[idx, :, lanes] * acc_scr[idx, :, lanes]
                        + jnp.dot(stream[2](i, idx)[:, :n_keys], p_scr[idx, :n_keys, lanes],
                                  preferred_element_type=F32))

    def finish(i):
        for seq, stream in zip(views, streams):
            stream[4](i, seq[i % 2][1])

    w_maps = [stream[0](0) for stream in streams]
    start(0)
    scores(0, w_maps, 0)
    for i in range(n_tiles):
        def body(j, carry, i=i, w_maps=w_maps):
            softmax(i)
            scores(i, w_maps, j + 1)
            accumulate(i, j)
            return carry

        if i > 0:
            lax.fori_loop(0, i, body, 0)
            finish(i - 1)
        if i + 1 < n_tiles:
            w_maps = [stream[0](i + 1) for stream in streams]
            start(i + 1)
            scores(i + 1, w_maps, 0)
        softmax_diag(i)
        accumulate_diag(i)
    finish(n_tiles - 1)


def _masked_maps(qt):
    feat = lax.broadcasted_iota(jnp.int32, qt.shape, 0)
    zero = jnp.zeros_like(qt)
    return jnp.where(feat < DIFF_QK_DIM, qt, zero), jnp.where(feat >= DIFF_QK_DIM, qt, zero)


def _ones_rows(tile):
    row = lax.broadcasted_iota(jnp.int32, (BF16_ROWS, tile), 0)
    return jnp.where(row == 0, 1.0, 0.0).astype(BF16)


def _normalized(acc_ref, idx, rows):
    return acc_ref[idx, :rows, :] / acc_ref[idx, rows:rows + 1, :]


def _diff_attn_kernel(qt_ref, k_ref, vt_ref, bias_ref, lam_ref, g_ref, o_ref, *scratch,
                      lambda_init):
    n_seqs, n_tiles, _, tile = qt_ref.shape
    ones = _ones_rows(tile)
    lam = (jnp.exp(jnp.sum(lam_ref[0:1, :] * lam_ref[1:2, :], axis=-1, keepdims=True))
           - jnp.exp(jnp.sum(lam_ref[2:3, :] * lam_ref[3:4, :], axis=-1, keepdims=True))
           + lambda_init)

    def stream(b):
        def keys_fn(j):
            return k_ref[b, pl.ds(pl.multiple_of(j * tile, tile), tile), :]

        def values_fn(j, idx):
            return jnp.concatenate([vt_ref[b, j], ones], axis=0)

        def finish(i, acc_scr):
            o = (_normalized(acc_scr, 0, DIFF_V_DIM)
                 - lam * _normalized(acc_scr, 1, DIFF_V_DIM))
            o = (o * _rms_scale(o, axis=0) * g_ref[...]) * (1.0 - lambda_init)
            o_ref[b, i * tile:(i + 1) * tile, :] = o.T.astype(o_ref.dtype)

        return (lambda i: _masked_maps(qt_ref[b, i]),
                keys_fn, values_fn, lambda i, j: bias_ref[0, i - j], finish)

    _causal_flash(n_tiles, tile, scratch, [stream(b) for b in range(n_seqs)])


def _fox_attn_kernel(qt_ref, qa_ref, k_ref, ka_ref, vt_ref, o_ref, kat_scr, *scratch):
    n_seqs, n_tiles, _, tile = qt_ref.shape
    ones = _ones_rows(tile)
    aug_zero = jnp.zeros((BF16_ROWS, tile), BF16)
    pad = jnp.zeros((MXU_DEPTH - LANES - 2 * BF16_ROWS, tile), BF16)
    lane_pad = jnp.zeros((LANES - 2 * BF16_ROWS, tile), BF16)

    for b in range(n_seqs):
        for t in range(n_tiles):
            cols = slice(t * tile, (t + 1) * tile)
            rows = jnp.concatenate([ka_ref[b, 0, :, cols], ka_ref[b, 1, :, cols], lane_pad], axis=0)
            kat_scr[b, cols, :] = rows.astype(F32).T.astype(BF16)

    def stream(b):
        def keys_fn(j):
            rows = pl.ds(pl.multiple_of(j * tile, tile), tile)
            return jnp.concatenate([k_ref[b, rows, :], kat_scr[b, rows, :]], axis=1)

        def values_fn(j, idx):
            rows = vt_ref[b, j, idx * FOX_HEAD_DIM:(idx + 1) * FOX_HEAD_DIM, :]
            return jnp.concatenate([rows, ones], axis=0)

        def finish(i, acc_scr):
            o = jnp.concatenate([_normalized(acc_scr, 0, FOX_HEAD_DIM),
                                 _normalized(acc_scr, 1, FOX_HEAD_DIM)], axis=0)
            o_ref[b, i * tile:(i + 1) * tile, :] = o.T.astype(o_ref.dtype)

        def w_maps_fn(i):
            q_a, q_b = _masked_maps(qt_ref[b, i])
            cols = slice(i * tile, (i + 1) * tile)
            return (jnp.concatenate([q_a, qa_ref[b, 0, :, cols], aug_zero, pad], axis=0),
                    jnp.concatenate([q_b, aug_zero, qa_ref[b, 1, :, cols], pad], axis=0))

        return (w_maps_fn, keys_fn, values_fn, lambda i, j: None, finish)

    _causal_flash(n_tiles, tile, scratch, [stream(b) for b in range(n_seqs)])


def _attn_scratch(tile, value_rows):
    n = 4 * ATTN_SEQS
    stat = pltpu.VMEM((n, 1, tile), F32)
    return [stat, pltpu.VMEM((n, value_rows + BF16_ROWS, tile), F32),
            pltpu.VMEM((n, tile, tile), F32), pltpu.VMEM((n, tile, tile), BF16), stat, stat]


def _diff_attention(fm, dk, bias_tiles, lam_vecs, subln_g, lambda_init):
    bsz, n_tiles, _, tile = fm.shape
    s = n_tiles * tile
    width = dk.shape[2]
    n_groups = width // LANES
    nb = ATTN_SEQS
    kern = functools.partial(_diff_attn_kernel, lambda_init=lambda_init)
    return pl.pallas_call(
        kern,
        grid=(n_groups, bsz // nb),
        in_specs=[pl.BlockSpec((nb, n_tiles, LANES, tile), lambda g, b: (b, 0, g, 0)),
                  pl.BlockSpec((nb, s, LANES), lambda g, b: (b, 0, g)),
                  pl.BlockSpec((nb, n_tiles, LANES, tile), lambda g, b: (b, 0, n_groups + g, 0)),
                  pl.BlockSpec((1, n_tiles, tile, tile), lambda g, b: (g, 0, 0, 0),
                               pipeline_mode=pl.Buffered(1)),
                  pl.BlockSpec(lam_vecs.shape, lambda g, b: (0, 0)),
                  pl.BlockSpec((LANES, 1), lambda g, b: (0, 0))],
        out_specs=pl.BlockSpec((nb, s, LANES), lambda g, b: (b, 0, g)),
        out_shape=jax.ShapeDtypeStruct((bsz, s, width), BF16),
        scratch_shapes=_attn_scratch(tile, DIFF_V_DIM),
        compiler_params=_params(("arbitrary", "arbitrary")),
        name="diff_attention",
    )(fm, dk, fm, bias_tiles, lam_vecs, subln_g.reshape(LANES, 1))


def _fox_attention(fm, fk, q_aug, k_aug):
    bsz, n_tiles, _, tile = fm.shape
    s = n_tiles * tile
    width = fk.shape[2]
    n_groups = width // LANES
    nb = ATTN_SEQS
    return pl.pallas_call(
        _fox_attn_kernel,
        grid=(n_groups, bsz // nb),
        in_specs=[pl.BlockSpec((nb, n_tiles, LANES, tile), lambda g, b: (b, 0, 2 * n_groups + g, 0)),
                  pl.BlockSpec((nb, 2, BF16_ROWS, s), lambda g, b: (b, g, 0, 0)),
                  pl.BlockSpec((nb, s, LANES), lambda g, b: (b, 0, g)),
                  pl.BlockSpec((nb, 2, BF16_ROWS, s), lambda g, b: (b, g, 0, 0)),
                  pl.BlockSpec((nb, n_tiles, LANES, tile), lambda g, b: (b, 0, 3 * n_groups + g, 0))],
        out_specs=pl.BlockSpec((nb, s, LANES), lambda g, b: (b, 0, g)),
        out_shape=jax.ShapeDtypeStruct((bsz, s, width), BF16),
        scratch_shapes=[pltpu.VMEM((nb, s, LANES), BF16)] + _attn_scratch(tile, FOX_HEAD_DIM),
        compiler_params=_params(("arbitrary", "arbitrary")),
        name="fox_attention",
    )(fm, q_aug, fk, k_aug, fm)


def _ffn_kernel(d_ref, f_ref, x_ref, mod_ref, wo_ref, g_ref, wu_ref, cw_ref, cb_ref, wd_ref, fg_ref,
                o_ref, tail_scr, u_scr, x1_scr, h_scr, *, width, d_ff, chunk, final_norm):
    i = pl.program_id(1)
    n_seqs, ts = x_ref.shape[:2]
    lane_tiles = chunk // LANES
    n_chunks = d_ff // chunk

    @pl.when(i == 0)
    def _():
        tail_scr[...] = jnp.zeros(tail_scr.shape, F32)

    def project(s):
        mix = (jnp.dot(d_ref[s], wo_ref[:width, :], preferred_element_type=F32)
               + jnp.dot(f_ref[s], wo_ref[width:, :], preferred_element_type=F32))
        x1 = x_ref[s] + mod_ref[s, 2:3, :] * mix
        x1_scr[s] = x1
        h = (x1 * _rms_scale(x1) * g_ref[...]) * (1.0 + mod_ref[s, 4:5, :]) + mod_ref[s, 3:4, :]
        h_scr[s] = h.astype(BF16)

    def buffers(s, c, half):
        base = ((s * 2 + c % 2) * 2 + half) * lane_tiles
        return [u_scr.at[base + t] for t in range(lane_tiles)]

    def up(s, c):
        for half in range(2):
            c0 = half * d_ff + c * chunk
            u = jnp.dot(h_scr[s], wu_ref[:, c0:c0 + chunk], preferred_element_type=F32)
            for t, buf in enumerate(buffers(s, c, half)):
                cols = slice(c0 + t * LANES, c0 + (t + 1) * LANES)
                buf[0:SUBLANES, :] = tail_scr[s, :, cols]
                buf[SUBLANES:, :] = u[:, t * LANES:(t + 1) * LANES]
                tail_scr[s, :, cols] = u[ts - SUBLANES:, t * LANES:(t + 1) * LANES]

    def conv(s, c, half):
        c0 = half * d_ff + c * chunk
        tiles = []
        for t, buf in enumerate(buffers(s, c, half)):
            cols = slice(c0 + t * LANES, c0 + (t + 1) * LANES)
            y = cb_ref[:, cols]
            for tap in range(CONV_WIDTH):
                lo = SUBLANES - (CONV_WIDTH - 1) + tap
                y = y + cw_ref[tap:tap + 1, cols] * buf[pl.ds(lo, ts), :]
            tiles.append(y)
        return jnp.concatenate(tiles, axis=1)

    acc = []
    for s in range(n_seqs):
        project(s)
        acc.append(jnp.zeros((ts, o_ref.shape[2]), F32))
    for s in range(n_seqs):
        up(s, 0)
    for c in range(n_chunks):
        for s in range(n_seqs):
            gate = conv(s, c, 0)
            act = (gate * jax.nn.sigmoid(gate) * conv(s, c, 1)).astype(BF16)
            acc[s] = acc[s] + jnp.dot(act, wd_ref[c * chunk:(c + 1) * chunk, :],
                                      preferred_element_type=F32)
            if c + 1 < n_chunks:
                up(s, c + 1)
    for s in range(n_seqs):
        x2 = x1_scr[s] + mod_ref[s, 5:6, :] * acc[s]
        if final_norm:
            x2 = x2 * _rms_scale(x2) * fg_ref[...]
        o_ref[s] = x2


def _out_ffn(d_out, f_out, x, mod3, w_out, ffn_g, w_up, conv_w, conv_b, w_down, final_g, final_norm):
    bsz, s, d = x.shape
    width = d_out.shape[2]
    d_ff = w_down.shape[0]
    ts = ROW_TILE
    nb = FFN_SEQS
    row_spec = pl.BlockSpec((nb, ts, d), lambda b, i: (b, i, 0))
    attn_spec = pl.BlockSpec((nb, ts, width), lambda b, i: (b, i, 0))
    kern = functools.partial(_ffn_kernel, width=width, d_ff=d_ff, chunk=FFN_CHUNK,
                             final_norm=final_norm)
    const = lambda b, i: (0, 0)
    resident = functools.partial(pl.BlockSpec, index_map=const, pipeline_mode=pl.Buffered(1))
    return pl.pallas_call(
        kern,
        grid=(bsz // nb, s // ts),
        in_specs=[attn_spec, attn_spec, row_spec,
                  pl.BlockSpec((nb, N_MOD, d), lambda b, i: (b, 0, 0)),
                  resident(w_out.shape),
                  pl.BlockSpec((1, d), const),
                  resident(w_up.shape),
                  pl.BlockSpec(conv_w.shape, const),
                  pl.BlockSpec((1, 2 * d_ff), const),
                  resident(w_down.shape),
                  pl.BlockSpec((1, d), const)],
        out_specs=row_spec,
        out_shape=jax.ShapeDtypeStruct((bsz, s, d), F32),
        scratch_shapes=[pltpu.VMEM((nb, SUBLANES, 2 * d_ff), F32),
                        pltpu.VMEM((nb * 4 * FFN_CHUNK // LANES, ts + SUBLANES, LANES), F32),
                        pltpu.VMEM((nb, ts, d), F32), pltpu.VMEM((nb, ts, d), BF16)],
        compiler_params=_params(("arbitrary", "arbitrary")),
        name="out_ffn",
    )(d_out, f_out, x, mod3, w_out, ffn_g.reshape(1, d), w_up, conv_w,
      conv_b.reshape(1, 2 * d_ff), w_down, final_g.reshape(1, d))


def kernel(x, c, ada_w, ada_b, attn_norm_g, w_in, forget_b, lambda_q1, lambda_k1, lambda_q2,
           lambda_k2, subln_g, rel_bias, w_out, ffn_norm_g, w_up, conv_w, conv_b, w_down,
           final_norm_g):
    bsz, s, d = x.shape
    depth = ada_w.shape[0]
    n_fox_heads = forget_b.shape[1]
    width = (w_in.shape[2] - n_fox_heads) // 6
    assert width % LANES == 0 and subln_g.shape[1] == DIFF_V_DIM == LANES
    assert n_fox_heads * FOX_HEAD_DIM == width and s % ATTN_TILE == 0 and s % ROW_TILE == 0
    assert bsz % ATTN_SEQS == 0 and bsz % FFN_SEQS == 0
    qk_scale = DIFF_QK_DIM ** -0.5 * LOG2E

    bias = _bias_tiles(rel_bias.T, s, ATTN_TILE)

    for l in range(depth):
        lambda_init = 0.8 - 0.6 * math.exp(-0.3 * l)
        mod3 = _modulation(c, ada_w[l], ada_b[l]).reshape(bsz, N_MOD, d)

        w = w_in[l]
        cols = lambda n: w[:, n * width:(n + 1) * width]
        w_fl = jnp.pad(w[:, 6 * width:], ((0, 0), (0, BF16_ROWS - n_fox_heads)))
        w_fm_t = jnp.concatenate([cols(0) * qk_scale, cols(2), cols(3) * qk_scale, cols(5), w_fl],
                                 axis=1).T.astype(BF16)
        w_tm = jnp.concatenate([cols(1), cols(4)], axis=1).astype(BF16)
        fm, dk, fk, fl_t = _in_projection(x, mod3, attn_norm_g[l], w_fm_t, w_tm,
                                          width=width, n_fox_heads=n_fox_heads)

        q_aug, k_aug = _forget_aug(fl_t, forget_b[l])

        lam_vecs = jnp.stack([lambda_q1[l], lambda_k1[l], lambda_q2[l], lambda_k2[l]]).astype(F32)
        d_out = _diff_attention(fm, dk, bias, lam_vecs, subln_g[l], lambda_init)
        f_out = _fox_attention(fm, fk, q_aug, k_aug)

        x = _out_ffn(d_out, f_out, x, mod3, w_out[l].astype(BF16), ffn_norm_g[l],
                     w_up[l].astype(BF16), conv_w[l], conv_b[l], w_down[l].astype(BF16),
                     final_norm_g, final_norm=(l == depth - 1))
    return x
```

```python
import functools
import math

import jax
import jax.numpy as jnp
from jax import lax
from jax.experimental import pallas as pl
from jax.experimental.pallas import tpu as pltpu

F32 = jnp.float32
BF16 = jnp.bfloat16

DIFF_QK_DIM = 64
DIFF_V_DIM = 2 * DIFF_QK_DIM
FOX_HEAD_DIM = 64
CONV_WIDTH = 3
NUM_BUCKETS = 32
MAX_EXACT = NUM_BUCKETS // 2
MAX_DISTANCE = 128
N_MOD = 6
NORM_EPS = 1e-6
NEG_INF = -1e30
LOG2E = math.log2(math.e)

LANES = 128
SUBLANES = 8
BF16_ROWS = 16
MXU_DEPTH = 256
VMEM_LIMIT_BYTES = 56 * 1024 * 1024

ATTN_TILE = 512
ATTN_SEQS = 2
ROW_TILE = 256
FFN_SEQS = 2
FFN_CHUNK = 256
MOD_COL_TILE = 1536

NT_DIMS = (((1,), (1,)), ((), ()))


def _params(semantics):
    return pltpu.CompilerParams(dimension_semantics=semantics, vmem_limit_bytes=VMEM_LIMIT_BYTES)


def _rms_scale(x, axis=-1):
    return lax.rsqrt(jnp.mean(x * x, axis=axis, keepdims=True) + NORM_EPS)


def _mod_kernel(c_ref, w_ref, b_ref, o_ref):
    c = c_ref[...]
    act = c * jax.nn.sigmoid(c)
    o_ref[...] = jnp.dot(act.astype(BF16), w_ref[...].astype(BF16),
                         preferred_element_type=F32) + b_ref[...]


def _modulation(c, w, b):
    bsz, d = c.shape
    n = w.shape[1]
    tn = MOD_COL_TILE
    return pl.pallas_call(
        _mod_kernel,
        grid=(n // tn,),
        in_specs=[pl.BlockSpec((bsz, d), lambda j: (0, 0)),
                  pl.BlockSpec((d, tn), lambda j: (0, j)),
                  pl.BlockSpec((1, tn), lambda j: (0, j))],
        out_specs=pl.BlockSpec((bsz, tn), lambda j: (0, j)),
        out_shape=jax.ShapeDtypeStruct((bsz, n), F32),
        compiler_params=_params(("arbitrary",)),
        name="modulation",
    )(c, w, b.reshape(1, n))


def _proj_kernel(x_ref, mod_ref, g_ref, wt_ref, wk_ref, fm_ref, dk_ref, fk_ref, fl_ref,
                 *, width, n_fox_heads):
    x = x_ref[0]
    h = (x * _rms_scale(x) * g_ref[...]) * (1.0 + mod_ref[0, 1:2, :]) + mod_ref[0, 0:1, :]
    hb = h.astype(BF16)
    r = lax.dot_general(wt_ref[...], hb, NT_DIMS, preferred_element_type=F32)
    fm_ref[0, 0] = r[:4 * width].astype(BF16)
    fl_ref[0] = r[4 * width:4 * width + n_fox_heads]
    r = jnp.dot(hb, wk_ref[...], preferred_element_type=F32)
    dk_ref[0] = r[:, :width].astype(BF16)
    fk_ref[0] = r[:, width:].astype(BF16)


def _in_projection(x, mod3, g, w_fm_t, w_tm, *, width, n_fox_heads):
    bsz, s, d = x.shape
    ts = ATTN_TILE
    k_shape = jax.ShapeDtypeStruct((bsz, s, width), BF16)
    k_spec = pl.BlockSpec((1, ts, width), lambda b, i: (b, i, 0))
    kern = functools.partial(_proj_kernel, width=width, n_fox_heads=n_fox_heads)
    return pl.pallas_call(
        kern,
        grid=(bsz, s // ts),
        in_specs=[pl.BlockSpec((1, ts, d), lambda b, i: (b, i, 0)),
                  pl.BlockSpec((1, N_MOD, d), lambda b, i: (b, 0, 0)),
                  pl.BlockSpec((1, d), lambda b, i: (0, 0)),
                  pl.BlockSpec(w_fm_t.shape, lambda b, i: (0, 0)),
                  pl.BlockSpec(w_tm.shape, lambda b, i: (0, 0))],
        out_specs=[pl.BlockSpec((1, 1, 4 * width, ts), lambda b, i: (b, i, 0, 0)),
                   k_spec, k_spec,
                   pl.BlockSpec((1, n_fox_heads, ts), lambda b, i: (b, 0, i))],
        out_shape=[jax.ShapeDtypeStruct((bsz, s // ts, 4 * width, ts), BF16),
                   k_shape, k_shape,
                   jax.ShapeDtypeStruct((bsz, n_fox_heads, s), F32)],
        compiler_params=_params(("arbitrary", "arbitrary")),
        name="in_projection",
    )(x, mod3, g.reshape(1, d), w_fm_t, w_tm)


def _split3(x):
    hi = x.astype(BF16)
    r1 = x - hi.astype(F32)
    mid = r1.astype(BF16)
    lo = (r1 - mid.astype(F32)).astype(BF16)
    return hi, mid, lo


def _cum_kernel(fl_ref, fb_ref, qa_ref, ka_ref):
    z = fl_ref[0] + fb_ref[...]
    acc = jnp.minimum(z, 0.0) - jnp.log1p(jnp.exp(-jnp.abs(z)))
    n_heads, s = acc.shape
    pos = lax.broadcasted_iota(jnp.int32, acc.shape, 1)
    shift = 1
    while shift < s:
        acc = acc + jnp.where(pos >= shift, pltpu.roll(acc, shift, 1), 0.0)
        shift *= 2
    acc = acc * LOG2E
    q_parts = _split3(acc)
    k_parts = _split3(-acc)
    ones = jnp.ones((len(q_parts), s), BF16)
    zeros = jnp.zeros((BF16_ROWS - 2 * len(q_parts), s), BF16)
    for h in range(n_heads):
        qa_ref[0, h] = jnp.concatenate([p[h:h + 1] for p in q_parts] + [ones, zeros], axis=0)
        ka_ref[0, h] = jnp.concatenate([ones] + [p[h:h + 1] for p in k_parts] + [zeros], axis=0)


def _forget_aug(fl_t, forget_b):
    bsz, h, s = fl_t.shape
    aug_shape = jax.ShapeDtypeStruct((bsz, h, BF16_ROWS, s), BF16)
    aug_spec = pl.BlockSpec((1, h, BF16_ROWS, s), lambda b: (b, 0, 0, 0))
    return pl.pallas_call(
        _cum_kernel,
        grid=(bsz,),
        in_specs=[pl.BlockSpec((1, h, s), lambda b: (b, 0, 0)),
                  pl.BlockSpec((h, 1), lambda b: (0, 0))],
        out_specs=[aug_spec, aug_spec],
        out_shape=[aug_shape, aug_shape],
        compiler_params=_params(("arbitrary",)),
        name="forget_cumsum",
    )(fl_t, forget_b.reshape(h, 1))


def _bias_kernel(rb_ref, o_ref, *, tile, n_delta):
    h = pl.program_id(0)
    u = lax.broadcasted_iota(jnp.int32, (1, 2 * tile), 1)
    for dd in range(n_delta):
        n = jnp.maximum(dd * tile + u - tile, 0)
        nf = jnp.maximum(n, 1).astype(F32)
        large = MAX_EXACT + (jnp.log(nf / MAX_EXACT) / math.log(MAX_DISTANCE / MAX_EXACT)
                             * (NUM_BUCKETS - MAX_EXACT)).astype(jnp.int32)
        large = jnp.minimum(large, NUM_BUCKETS - 1)
        bucket = jnp.where(n < MAX_EXACT, n, large)
        row = jnp.zeros((1, 2 * tile), F32)
        for b in range(NUM_BUCKETS):
            row = jnp.where(bucket == b, rb_ref[h, b], row)
        row = row * LOG2E
        full = pltpu.roll(jnp.broadcast_to(row, (tile, 2 * tile)), 0, 1, stride=1, stride_axis=0)
        o_ref[0, dd] = full[:, tile:]


def _bias_tiles(rel_bias_t, seq, tile):
    n_heads = rel_bias_t.shape[0]
    n_delta = seq // tile
    kern = functools.partial(_bias_kernel, tile=tile, n_delta=n_delta)
    return pl.pallas_call(
        kern,
        grid=(n_heads,),
        in_specs=[pl.BlockSpec(memory_space=pltpu.SMEM)],
        out_specs=pl.BlockSpec((1, n_delta, tile, tile), lambda h: (h, 0, 0, 0)),
        out_shape=jax.ShapeDtypeStruct((n_heads, n_delta, tile, tile), F32),
        compiler_params=_params(("arbitrary",)),
        name="bias_tiles",
    )(rel_bias_t)


def _causal_flash(n_tiles, tile, scratch, streams):
    views = [[[r.at[pl.ds(4 * s + 2 * parity, 2)] for r in scratch] for parity in range(2)]
             for s in range(len(streams))]

    def start(i):
        for seq in views:
            m_scr, acc_scr = seq[i % 2][:2]
            m_scr[...] = jnp.full(m_scr.shape, NEG_INF, F32)
            acc_scr[...] = jnp.zeros(acc_scr.shape, F32)

    def scores(i, w_maps, j):
        for seq, stream in zip(views, streams):
            s_scr, mt_scr = seq[i % 2][2], seq[i % 2][4]
            w = stream[0](i)
            keys = stream[1](j)
            for idx in range(2):
                st = jnp.dot(keys, w[idx], preferred_element_type=F32)
                bias = stream[3](i, j)
                if bias is not None:
                    st = st + bias
                s_scr[idx] = st
                mt_scr[idx] = jnp.max(st, axis=0, keepdims=True)

    def softmax(i):
        for seq in views:
            m_scr, _, s_scr, p_scr, mt_scr, al_scr = seq[i % 2]
            for idx in range(2):
                m_prev = m_scr[idx]
                m_new = jnp.maximum(m_prev, mt_scr[idx])
                p_scr[idx] = jnp.exp2(s_scr[idx] - m_new).astype(BF16)
                al_scr[idx] = jnp.exp2(m_prev - m_new)
                m_scr[idx] = m_new

    def accumulate(i, j):
        for seq, stream in zip(views, streams):
            _, acc_scr, _, p_scr, _, al_scr = seq[i % 2]
            for idx in range(2):
                acc_scr[idx] = al_scr[idx] * acc_scr[idx] + jnp.dot(
                    stream[2](j, idx), p_scr[idx], preferred_element_type=F32)

    half = tile // 2
    diag_parts = ((0, half), (half, tile))

    def softmax_diag(i):
        for q0, n_keys in diag_parts:
            lanes = slice(q0, q0 + half)
            key_pos = lax.broadcasted_iota(jnp.int32, (n_keys, half), 0)
            qry_pos = lax.broadcasted_iota(jnp.int32, (n_keys, half), 1) + q0
            causal = key_pos <= qry_pos
            for seq in views:
                m_scr, _, s_scr, p_scr, _, al_scr = seq[i % 2]
                for idx in range(2):
                    st = jnp.where(causal, s_scr[idx, :n_keys, lanes], NEG_INF)
                    m_prev = m_scr[idx, :, lanes]
                    m_new = jnp.maximum(m_prev, jnp.max(st, axis=0, keepdims=True))
                    p_scr[idx, :n_keys, lanes] = jnp.exp2(st - m_new).astype(BF16)
                    al_scr[idx, :, lanes] = jnp.exp2(m_prev - m_new)
                    m_scr[idx, :, lanes] = m_new

    def accumulate_diag(i):
        for q0, n_keys in diag_parts:
            lanes = slice(q0, q0 + half)
            for seq, stream in zip(views, streams):
                _, acc_scr, _, p_scr, _, al_scr = seq[i % 2]
                for idx in range(2):
                    acc_scr[idx, :, lanes] = (
                        al_scr[idx, :, lanes] * acc_scr[idx, :, lanes]
                        + jnp.dot(stream[2](i, idx)[:, :n_keys], p_scr[idx, :n_keys, lanes],
                                  preferred_element_type=F32))

    def finish(i):
        for seq, stream in zip(views, streams):
            stream[4](i, seq[i % 2][1])

    w_maps = [stream[0](0) for stream in streams]
    start(0)
    scores(0, w_maps, 0)
    for i in range(n_tiles):
        def body(j, carry, i=i, w_maps=w_maps):
            softmax(i)
            scores(i, w_maps, j + 1)
            accumulate(i, j)
            return carry

        if i > 0:
            lax.fori_loop(0, i, body, 0)
            finish(i - 1)
        if i + 1 < n_tiles:
            w_maps = [stream[0](i + 1) for stream in streams]
            start(i + 1)
            scores(i + 1, w_maps, 0)
        softmax_diag(i)
        accumulate_diag(i)
    finish(n_tiles - 1)


def _masked_maps(qt):
    feat = lax.broadcasted_iota(jnp.int32, qt.shape, 0)
    zero = jnp.zeros_like(qt)
    return jnp.where(feat < DIFF_QK_DIM, qt, zero), jnp.where(feat >= DIFF_QK_DIM, qt, zero)


def _ones_rows(tile):
    row = lax.broadcasted_iota(jnp.int32, (BF16_ROWS, tile), 0)
    return jnp.where(row == 0, 1.0, 0.0).astype(BF16)


def _normalized(acc_ref, idx, rows):
    return acc_ref[idx, :rows, :] / acc_ref[idx, rows:rows + 1, :]


def _diff_attn_kernel(qt_ref, k_ref, vt_ref, bias_ref, lam_ref, g_ref, o_ref, *scratch,
                      lambda_init):
    n_seqs, n_tiles, _, tile = qt_ref.shape
    ones = _ones_rows(tile)
    lam = (jnp.exp(jnp.sum(lam_ref[0:1, :] * lam_ref[1:2, :], axis=-1, keepdims=True))
           - jnp.exp(jnp.sum(lam_ref[2:3, :] * lam_ref[3:4, :], axis=-1, keepdims=True))
           + lambda_init)

    def stream(b):
        def keys_fn(j):
            return k_ref[b, pl.ds(pl.multiple_of(j * tile, tile), tile), :]

        def values_fn(j, idx):
            return jnp.concatenate([vt_ref[b, j], ones], axis=0)

        def finish(i, acc_scr):
            o = (_normalized(acc_scr, 0, DIFF_V_DIM)
                 - lam * _normalized(acc_scr, 1, DIFF_V_DIM))
            o = (o * _rms_scale(o, axis=0) * g_ref[...]) * (1.0 - lambda_init)
            o_ref[b, i * tile:(i + 1) * tile, :] = o.T.astype(o_ref.dtype)

        return (lambda i: _masked_maps(qt_ref[b, i]),
                keys_fn, values_fn, lambda i, j: bias_ref[0, i - j], finish)

    _causal_flash(n_tiles, tile, scratch, [stream(b) for b in range(n_seqs)])


def _fox_attn_kernel(qt_ref, qa_ref, k_ref, ka_ref, vt_ref, o_ref, kat_scr, *scratch):
    n_seqs, n_tiles, _, tile = qt_ref.shape
    ones = _ones_rows(tile)
    aug_zero = jnp.zeros((BF16_ROWS, tile), BF16)
    pad = jnp.zeros((MXU_DEPTH - LANES - 2 * BF16_ROWS, tile), BF16)
    lane_pad = jnp.zeros((LANES - 2 * BF16_ROWS, tile), BF16)

    for b in range(n_seqs):
        for t in range(n_tiles):
            cols = slice(t * tile, (t + 1) * tile)
            rows = jnp.concatenate([ka_ref[b, 0, :, cols], ka_ref[b, 1, :, cols], lane_pad], axis=0)
            kat_scr[b, cols, :] = rows.astype(F32).T.astype(BF16)

    def stream(b):
        def keys_fn(j):
            rows = pl.ds(pl.multiple_of(j * tile, tile), tile)
            return jnp.concatenate([k_ref[b, rows, :], kat_scr[b, rows, :]], axis=1)

        def values_fn(j, idx):
            rows = vt_ref[b, j, idx * FOX_HEAD_DIM:(idx + 1) * FOX_HEAD_DIM, :]
            return jnp.concatenate([rows, ones], axis=0)

        def finish(i, acc_scr):
            o = jnp.concatenate([_normalized(acc_scr, 0, FOX_HEAD_DIM),
                                 _normalized(acc_scr, 1, FOX_HEAD_DIM)], axis=0)
            o_ref[b, i * tile:(i + 1) * tile, :] = o.T.astype(o_ref.dtype)

        def w_maps_fn(i):
            q_a, q_b = _masked_maps(qt_ref[b, i])
            cols = slice(i * tile, (i + 1) * tile)
            return (jnp.concatenate([q_a, qa_ref[b, 0, :, cols], aug_zero, pad], axis=0),
                    jnp.concatenate([q_b, aug_zero, qa_ref[b, 1, :, cols], pad], axis=0))

        return (w_maps_fn, keys_fn, values_fn, lambda i, j: None, finish)

    _causal_flash(n_tiles, tile, scratch, [stream(b) for b in range(n_seqs)])


def _attn_scratch(tile, value_rows):
    n = 4 * ATTN_SEQS
    stat = pltpu.VMEM((n, 1, tile), F32)
    return [stat, pltpu.VMEM((n, value_rows + BF16_ROWS, tile), F32),
            pltpu.VMEM((n, tile, tile), F32), pltpu.VMEM((n, tile, tile), BF16), stat, stat]


def _diff_attention(fm, dk, bias_tiles, lam_vecs, subln_g, lambda_init):
    bsz, n_tiles, _, tile = fm.shape
    s = n_tiles * tile
    width = dk.shape[2]
    n_groups = width // LANES
    nb = ATTN_SEQS
    kern = functools.partial(_diff_attn_kernel, lambda_init=lambda_init)
    return pl.pallas_call(
        kern,
        grid=(n_groups, bsz // nb),
        in_specs=[pl.BlockSpec((nb, n_tiles, LANES, tile), lambda g, b: (b, 0, g, 0)),
                  pl.BlockSpec((nb, s, LANES), lambda g, b: (b, 0, g)),
                  pl.BlockSpec((nb, n_tiles, LANES, tile), lambda g, b: (b, 0, n_groups + g, 0)),
                  pl.BlockSpec((1, n_tiles, tile, tile), lambda g, b: (g, 0, 0, 0),
                               pipeline_mode=pl.Buffered(1)),
                  pl.BlockSpec(lam_vecs.shape, lambda g, b: (0, 0)),
                  pl.BlockSpec((LANES, 1), lambda g, b: (0, 0))],
        out_specs=pl.BlockSpec((nb, s, LANES), lambda g, b: (b, 0, g)),
        out_shape=jax.ShapeDtypeStruct((bsz, s, width), BF16),
        scratch_shapes=_attn_scratch(tile, DIFF_V_DIM),
        compiler_params=_params(("arbitrary", "arbitrary")),
        name="diff_attention",
    )(fm, dk, fm, bias_tiles, lam_vecs, subln_g.reshape(LANES, 1))


def _fox_attention(fm, fk, q_aug, k_aug):
    bsz, n_tiles, _, tile = fm.shape
    s = n_tiles * tile
    width = fk.shape[2]
    n_groups = width // LANES
    nb = ATTN_SEQS
    return pl.pallas_call(
        _fox_attn_kernel,
        grid=(n_groups, bsz // nb),
        in_specs=[pl.BlockSpec((nb, n_tiles, LANES, tile), lambda g, b: (b, 0, 2 * n_groups + g, 0)),
                  pl.BlockSpec((nb, 2, BF16_ROWS, s), lambda g, b: (b, g, 0, 0)),
                  pl.BlockSpec((nb, s, LANES), lambda g, b: (b, 0, g)),
                  pl.BlockSpec((nb, 2, BF16_ROWS, s), lambda g, b: (b, g, 0, 0)),
                  pl.BlockSpec((nb, n_tiles, LANES, tile), lambda g, b: (b, 0, 3 * n_groups + g, 0))],
        out_specs=pl.BlockSpec((nb, s, LANES), lambda g, b: (b, 0, g)),
        out_shape=jax.ShapeDtypeStruct((bsz, s, width), BF16),
        scratch_shapes=[pltpu.VMEM((nb, s, LANES), BF16)] + _attn_scratch(tile, FOX_HEAD_DIM),
        compiler_params=_params(("arbitrary", "arbitrary")),
        name="fox_attention",
    )(fm, q_aug, fk, k_aug, fm)


def _ffn_kernel(d_ref, f_ref, x_ref, mod_ref, wo_ref, g_ref, wu_ref, cw_ref, cb_ref, wd_ref, fg_ref,
                o_ref, tail_scr, u_scr, x1_scr, h_scr, *, width, d_ff, chunk, final_norm):
    i = pl.program_id(1)
    n_seqs, ts = x_ref.shape[:2]
    lane_tiles = chunk // LANES
    n_chunks = d_ff // chunk

    @pl.when(i == 0)
    def _():
        tail_scr[...] = jnp.zeros(tail_scr.shape, F32)

    def project(s):
        mix = (jnp.dot(d_ref[s], wo_ref[:width, :], preferred_element_type=F32)
               + jnp.dot(f_ref[s], wo_ref[width:, :], preferred_element_type=F32))
        x1 = x_ref[s] + mod_ref[s, 2:3, :] * mix
        x1_scr[s] = x1
        h = (x1 * _rms_scale(x1) * g_ref[...]) * (1.0 + mod_ref[s, 4:5, :]) + mod_ref[s, 3:4, :]
        h_scr[s] = h.astype(BF16)

    def buffers(s, c, half):
        base = ((s * 2 + c % 2) * 2 + half) * lane_tiles
        return [u_scr.at[base + t] for t in range(lane_tiles)]

    def up(s, c):
        for half in range(2):
            c0 = half * d_ff + c * chunk
            u = jnp.dot(h_scr[s], wu_ref[:, c0:c0 + chunk], preferred_element_type=F32)
            for t, buf in enumerate(buffers(s, c, half)):
                cols = slice(c0 + t * LANES, c0 + (t + 1) * LANES)
                buf[0:SUBLANES, :] = tail_scr[s, :, cols]
                buf[SUBLANES:, :] = u[:, t * LANES:(t + 1) * LANES]
                tail_scr[s, :, cols] = u[ts - SUBLANES:, t * LANES:(t + 1) * LANES]

    def conv(s, c, half):
        c0 = half * d_ff + c * chunk
        tiles = []
        for t, buf in enumerate(buffers(s, c, half)):
            cols = slice(c0 + t * LANES, c0 + (t + 1) * LANES)
            y = cb_ref[:, cols]
            for tap in range(CONV_WIDTH):
                lo = SUBLANES - (CONV_WIDTH - 1) + tap
                y = y + cw_ref[tap:tap + 1, cols] * buf[pl.ds(lo, ts), :]
            tiles.append(y)
        return jnp.concatenate(tiles, axis=1)

    acc = []
    for s in range(n_seqs):
        project(s)
        acc.append(jnp.zeros((ts, o_ref.shape[2]), F32))
    for s in range(n_seqs):
        up(s, 0)
    for c in range(n_chunks):
        for s in range(n_seqs):
            gate = conv(s, c, 0)
            act = (gate * jax.nn.sigmoid(gate) * conv(s, c, 1)).astype(BF16)
            acc[s] = acc[s] + jnp.dot(act, wd_ref[c * chunk:(c + 1) * chunk, :],
                                      preferred_element_type=F32)
            if c + 1 < n_chunks:
                up(s, c + 1)
    for s in range(n_seqs):
        x2 = x1_scr[s] + mod_ref[s, 5:6, :] * acc[s]
        if final_norm:
            x2 = x2 * _rms_scale(x2) * fg_ref[...]
        o_ref[s] = x2


def _out_ffn(d_out, f_out, x, mod3, w_out, ffn_g, w_up, conv_w, conv_b, w_down, final_g, final_norm):
    bsz, s, d = x.shape
    width = d_out.shape[2]
    d_ff = w_down.shape[0]
    ts = ROW_TILE
    nb = FFN_SEQS
    row_spec = pl.BlockSpec((nb, ts, d), lambda b, i: (b, i, 0))
    attn_spec = pl.BlockSpec((nb, ts, width), lambda b, i: (b, i, 0))
    kern = functools.partial(_ffn_kernel, width=width, d_ff=d_ff, chunk=FFN_CHUNK,
                             final_norm=final_norm)
    const = lambda b, i: (0, 0)
    resident = functools.partial(pl.BlockSpec, index_map=const, pipeline_mode=pl.Buffered(1))
    return pl.pallas_call(
        kern,
        grid=(bsz // nb, s // ts),
        in_specs=[attn_spec, attn_spec, row_spec,
                  pl.BlockSpec((nb, N_MOD, d), lambda b, i: (b, 0, 0)),
                  resident(w_out.shape),
                  pl.BlockSpec((1, d), const),
                  resident(w_up.shape),
                  pl.BlockSpec(conv_w.shape, const),
                  pl.BlockSpec((1, 2 * d_ff), const),
                  resident(w_down.shape),
                  pl.BlockSpec((1, d), const)],
        out_specs=row_spec,
        out_shape=jax.ShapeDtypeStruct((bsz, s, d), F32),
        scratch_shapes=[pltpu.VMEM((nb, SUBLANES, 2 * d_ff), F32),
                        pltpu.VMEM((nb * 4 * FFN_CHUNK // LANES, ts + SUBLANES, LANES), F32),
                        pltpu.VMEM((nb, ts, d), F32), pltpu.VMEM((nb, ts, d), BF16)],
        compiler_params=_params(("arbitrary", "arbitrary")),
        name="out_ffn",
    )(d_out, f_out, x, mod3, w_out, ffn_g.reshape(1, d), w_up, conv_w,
      conv_b.reshape(1, 2 * d_ff), w_down, final_g.reshape(1, d))


def kernel(x, c, ada_w, ada_b, attn_norm_g, w_in, forget_b, lambda_q1, lambda_k1, lambda_q2,
           lambda_k2, subln_g, rel_bias, w_out, ffn_norm_g, w_up, conv_w, conv_b, w_down,
           final_norm_g):
    bsz, s, d = x.shape
    depth = ada_w.shape[0]
    n_fox_heads = forget_b.shape[1]
    width = (w_in.shape[2] - n_fox_heads) // 6
    assert width % LANES == 0 and subln_g.shape[1] == DIFF_V_DIM == LANES
    assert n_fox_heads * FOX_HEAD_DIM == width and s % ATTN_TILE == 0 and s % ROW_TILE == 0
    assert bsz % ATTN_SEQS == 0 and bsz % FFN_SEQS == 0
    qk_scale = DIFF_QK_DIM ** -0.5 * LOG2E

    bias = _bias_tiles(rel_bias.T, s, ATTN_TILE)

    for l in range(depth):
        lambda_init = 0.8 - 0.6 * math.exp(-0.3 * l)
        mod3 = _modulation(c, ada_w[l], ada_b[l]).reshape(bsz, N_MOD, d)

        w = w_in[l]
        cols = lambda n: w[:, n * width:(n + 1) * width]
        w_fl = jnp.pad(w[:, 6 * width:], ((0, 0), (0, BF16_ROWS - n_fox_heads)))
        w_fm_t = jnp.concatenate([cols(0) * qk_scale, cols(2), cols(3) * qk_scale, cols(5), w_fl],
                                 axis=1).T.astype(BF16)
        w_tm = jnp.concatenate([cols(1), cols(4)], axis=1).astype(BF16)
        fm, dk, fk, fl_t = _in_projection(x, mod3, attn_norm_g[l], w_fm_t, w_tm,
                                          width=width, n_fox_heads=n_fox_heads)

        q_aug, k_aug = _forget_aug(fl_t, forget_b[l])

        lam_vecs = jnp.stack([lambda_q1[l], lambda_k1[l], lambda_q2[l], lambda_k2[l]]).astype(F32)
        d_out = _diff_attention(fm, dk, bias, lam_vecs, subln_g[l], lambda_init)
        f_out = _fox_attention(fm, fk, q_aug, k_aug)

        x = _out_ffn(d_out, f_out, x, mod3, w_out[l].astype(BF16), ffn_norm_g[l],
                     w_up[l].astype(BF16), conv_w[l], conv_b[l], w_down[l].astype(BF16),
                     final_norm_g, final_norm=(l == depth - 1))
    return x
```
